```python
import jax
import jax.numpy as jnp
from jax import lax
import numpy as np

D_MODEL = 4096
BATCH = 1
SEQ = 8192
DEPTH = 2

GRID_W = 64
CTX_LEN = 256
N_BRANCH = 4
BRANCH_WIDTH = D_MODEL // N_BRANCH
HEAD_DIM = 128
GM_CHUNK = 128
GM_GROUPS = BRANCH_WIDTH // HEAD_DIM
ATT_HEADS = BRANCH_WIDTH // HEAD_DIM
ATT_KV_HEADS = 2
ATT_REP = ATT_HEADS // ATT_KV_HEADS
KV_WIDTH = ATT_KV_HEADS * HEAD_DIM
ATT_Q_BLOCK = 128
ROPE_THETA = 10000.0
CONV_K = 31
ML_HEADS = BRANCH_WIDTH // HEAD_DIM
ML_HEAD_DIM = HEAD_DIM
ML_CHUNK = 128
ML_N_GATES = 4
N_EXPERTS = 16
EXPERT_FF = D_MODEL // 4
CAPACITY_FACTOR = 2
ALPHA = (2 * DEPTH) ** 0.25
BETA = (8 * DEPTH) ** -0.25
EPS = 1e-6
N_MOD = 6

PROJ_SPLITS = (
    ('gm_u', BRANCH_WIDTH), ('gm_v', BRANCH_WIDTH),
    ('att_q', BRANCH_WIDTH), ('att_k', KV_WIDTH), ('att_v', KV_WIDTH),
    ('cv_a', BRANCH_WIDTH), ('cv_b', BRANCH_WIDTH),
    ('ml_q', BRANCH_WIDTH), ('ml_k', BRANCH_WIDTH), ('ml_v', BRANCH_WIDTH), ('ml_o', BRANCH_WIDTH),
    ('ml_gates', ML_N_GATES * ML_HEADS),
    ('merge', N_BRANCH * D_MODEL),
)
IN_COLS = 9 * BRANCH_WIDTH + 2 * KV_WIDTH + ML_N_GATES * ML_HEADS + N_BRANCH * D_MODEL

kernel_name = 'hybrid_diffusion_trunk'


def _norm_stats(x):
    xf = x.astype(jnp.float32)
    xc = xf - jnp.mean(xf, axis=-1, keepdims=True)
    return xc * lax.rsqrt(jnp.mean(xc * xc, axis=-1, keepdims=True) + EPS)


def layer_norm(x, g, b):
    return (_norm_stats(x) * g.astype(jnp.float32) + b.astype(jnp.float32)).astype(x.dtype)


def rms_norm(x, g):
    xf = x.astype(jnp.float32)
    y = xf * lax.rsqrt(jnp.mean(xf * xf, axis=-1, keepdims=True) + EPS) * g.astype(jnp.float32)
    return y.astype(x.dtype)


def modulate(x, shift, scale):
    return x * (1.0 + scale) + shift


def split_projection(proj):
    out = {}
    start = 0
    for name, width in PROJ_SPLITS:
        out[name] = proj[..., start:start + width]
        start += width
    return out


def axial_rope_tables(n_tokens):
    rows = n_tokens // GRID_W
    row = jnp.repeat(jnp.arange(rows), GRID_W).astype(jnp.float32)
    col = (jnp.arange(n_tokens) % GRID_W).astype(jnp.float32)
    n_freq = HEAD_DIM // 4
    inv = ROPE_THETA ** (-jnp.arange(n_freq, dtype=jnp.float32) / n_freq)
    ang_r = row[:, None] * inv[None, :]
    ang_c = col[:, None] * inv[None, :]
    return (jnp.cos(ang_r), jnp.sin(ang_r), jnp.cos(ang_c), jnp.sin(ang_c))


def apply_axial_rope(x, rope):
    cos_r, sin_r, cos_c, sin_c = rope

    def rotate(xh, cos, sin):
        x1, x2 = jnp.split(xh, 2, axis=-1)
        cos = cos[None, :, None, :]
        sin = sin[None, :, None, :]
        return jnp.concatenate([x1 * cos - x2 * sin, x2 * cos + x1 * sin], axis=-1)

    x_row, x_col = jnp.split(x.astype(jnp.float32), 2, axis=-1)
    out = jnp.concatenate([rotate(x_row, cos_r, sin_r), rotate(x_col, cos_c, sin_c)], axis=-1)
    return out.astype(x.dtype)


def chunk_gmlp(P, p):
    B, T, _ = P['gm_u'].shape
    u = jax.nn.gelu(P['gm_u'])
    v = layer_norm(jax.nn.gelu(P['gm_v']), p['gm_ln_g'], p['gm_ln_b'])
    v = v.reshape(B, T // GM_CHUNK, GM_CHUNK, GM_GROUPS, BRANCH_WIDTH // GM_GROUPS)
    s = jnp.einsum('gpq,bnqgc->bnpgc', p['gm_w_s'].astype(v.dtype), v)
    s = s + p['gm_b_s'].T.astype(v.dtype)[None, None, :, :, None]
    return u * s.reshape(B, T, BRANCH_WIDTH)


def attn_heads(P, p, rope):
    B, T, _ = P['att_q'].shape
    q = rms_norm(P['att_q'].reshape(B, T, ATT_HEADS, HEAD_DIM), p['att_q_gain'])
    k = rms_norm(P['att_k'].reshape(B, T, ATT_KV_HEADS, HEAD_DIM), p['att_k_gain'])
    v = P['att_v'].reshape(B, T, ATT_KV_HEADS, HEAD_DIM)
    if rope is not None:
        q = apply_axial_rope(q, rope)
        k = apply_axial_rope(k, rope)
    return q, k, v


def gqa_attend(q, k, v):
    B, T, _, _ = q.shape
    n_blocks = T // ATT_Q_BLOCK
    qb = jnp.moveaxis(q.reshape(B, n_blocks, ATT_Q_BLOCK, ATT_KV_HEADS, ATT_REP, HEAD_DIM), 1, 0)
    scale = HEAD_DIM ** -0.5

    def one_block(q_blk):
        s = jnp.einsum('bqgrd,bkgd->bgrqk', q_blk, k).astype(jnp.float32) * scale
        w = jax.nn.softmax(s, axis=-1).astype(v.dtype)
        return jnp.einsum('bgrqk,bkgd->bqgrd', w, v)

    o = lax.map(one_block, qb)
    return jnp.moveaxis(o, 0, 1).reshape(B, T, ATT_HEADS * HEAD_DIM)


def conformer_conv(P, p):
    g = P['cv_a'] * jax.nn.sigmoid(P['cv_b'])
    w = p['conv_w'][:, None, :].astype(g.dtype)
    y = lax.conv_general_dilated(g, w, window_strides=(1,), padding=[(CONV_K // 2, CONV_K // 2)],
                                 dimension_numbers=('NWC', 'WIO', 'NWC'),
                                 feature_group_count=BRANCH_WIDTH)
    y = y + p['conv_b'].astype(y.dtype)
    return jax.nn.silu(layer_norm(y, p['conv_ln_g'], p['conv_ln_b']))


def mlstm_prep(P, gate_bias):
    B, T, _ = P['ml_q'].shape

    def heads(a):
        return jnp.swapaxes(a.reshape(B, T, ML_HEADS, ML_HEAD_DIM), 1, 2).astype(jnp.float32)

    q = heads(P['ml_q'])
    k = heads(P['ml_k']) * (ML_HEAD_DIM ** -0.5)
    v = heads(P['ml_v'])
    g = P['ml_gates'].reshape(B, T, ML_N_GATES, ML_HEADS).astype(jnp.float32) + gate_bias.astype(jnp.float32)
    g = jnp.transpose(g, (2, 0, 3, 1))
    fw = (g[0], jax.nn.log_sigmoid(g[1]))
    bw = (g[2], jax.nn.log_sigmoid(g[3]))
    return q, k, v, fw, bw


def mlstm_zero_state(batch):
    return (jnp.zeros((batch, ML_HEADS, ML_HEAD_DIM, ML_HEAD_DIM), jnp.float32),
            jnp.zeros((batch, ML_HEADS, ML_HEAD_DIM), jnp.float32),
            jnp.zeros((batch, ML_HEADS), jnp.float32))


def mlstm_chunk_scan(q, k, v, log_i, log_f, state):
    B, H, T, d = q.shape
    nc = T // ML_CHUNK

    def chunks(a):
        return jnp.moveaxis(a.reshape(B, H, nc, ML_CHUNK, *a.shape[3:]), 2, 0)

    lower = jnp.tril(jnp.ones((ML_CHUNK, ML_CHUNK), dtype=bool))

    def step(carry, inp):
        C, n, m = carry
        qc, kc, vc, ic, fc = inp
        b = jnp.cumsum(fc, axis=-1)
        inter = b + m[..., None]
        dlog = b[..., :, None] - b[..., None, :] + ic[..., None, :]
        dlog = jnp.where(lower, dlog, -jnp.inf)
        mj = jnp.maximum(inter, jnp.max(dlog, axis=-1))
        w_inter = jnp.exp(inter - mj)
        s = jnp.einsum('bhjd,bhsd->bhjs', qc, kc) * jnp.exp(dlog - mj[..., None])
        num = (w_inter[..., None] * jnp.einsum('bhjd,bhde->bhje', qc, C)
               + jnp.einsum('bhjs,bhse->bhje', s, vc))
        den = w_inter * jnp.einsum('bhjd,bhd->bhj', qc, n) + jnp.sum(s, axis=-1)
        h = num / jnp.maximum(jnp.abs(den), jnp.exp(-mj))[..., None]
        m_new = mj[..., -1]
        w_c = jnp.exp(b[..., -1] + m - m_new)
        w_s = jnp.exp(b[..., -1:] - b + ic - m_new[..., None])
        C_new = w_c[..., None, None] * C + jnp.einsum('bhs,bhsd,bhse->bhde', w_s, kc, vc)
        n_new = w_c[..., None] * n + jnp.einsum('bhs,bhsd->bhd', w_s, kc)
        return (C_new, n_new, m_new), h

    final, h = lax.scan(step, state, (chunks(q), chunks(k), chunks(v), chunks(log_i), chunks(log_f)))
    return jnp.moveaxis(h, 0, 2).reshape(B, H, T, d), final


def mlstm_bidirectional(q, k, v, fw, bw, state_f, state_b):
    h_f, fin_f = mlstm_chunk_scan(q, k, v, fw[0], fw[1], state_f)

    def rev(a):
        return jnp.flip(a, axis=2)

    h_b, fin_b = mlstm_chunk_scan(rev(q), rev(k), rev(v), rev(bw[0]), rev(bw[1]), state_b)
    return h_f + rev(h_b), fin_f, fin_b


def mlstm_out(h, o_pre, norm_g):
    B, H, T, d = h.shape
    hn = _norm_stats(jnp.swapaxes(h, 1, 2)) * norm_g.astype(jnp.float32).reshape(H, d)
    return jax.nn.sigmoid(o_pre) * hn.reshape(B, T, H * d).astype(o_pre.dtype)


def merge_branches(gate_pre, branches, w_branch, w_out):
    B, T, _ = gate_pre.shape
    gates = jax.nn.sigmoid(gate_pre.reshape(B, T, N_BRANCH, D_MODEL))
    y = gates[:, :, 0] * (branches[0] @ w_branch[0])
    for i in range(1, N_BRANCH):
        y = y + gates[:, :, i] * (branches[i] @ w_branch[i])
    return y @ w_out


def token_mixer(h_lat, h_ctx, p, rope, need_ctx_out):
    P_lat = split_projection(h_lat @ p['w_in'])
    P_ctx = split_projection(h_ctx @ p['w_in'])
    q_l, k_l, v_l = attn_heads(P_lat, p, rope)
    q_c, k_c, v_c = attn_heads(P_ctx, p, None)
    att_lat = gqa_attend(q_l, jnp.concatenate([k_l, k_c], axis=1), jnp.concatenate([v_l, v_c], axis=1))
    zero = mlstm_zero_state(h_ctx.shape[0])
    mq_c, mk_c, mv_c, fw_c, bw_c = mlstm_prep(P_ctx, p['ml_gate_bias'])
    h_c, st_f, st_b = mlstm_bidirectional(mq_c, mk_c, mv_c, fw_c, bw_c, zero, zero)
    mq_l, mk_l, mv_l, fw_l, bw_l = mlstm_prep(P_lat, p['ml_gate_bias'])
    h_l, _, _ = mlstm_bidirectional(mq_l, mk_l, mv_l, fw_l, bw_l, st_f, st_b)
    y_lat = merge_branches(
        P_lat['merge'],
        [chunk_gmlp(P_lat, p), att_lat, conformer_conv(P_lat, p), mlstm_out(h_l, P_lat['ml_o'], p['ml_norm_g'])],
        p['w_branch'], p['w_out'])
    if not need_ctx_out:
        return y_lat, None
    att_ctx = gqa_attend(q_c, k_c, v_c)
    y_ctx = merge_branches(
        P_ctx['merge'],
        [chunk_gmlp(P_ctx, p), att_ctx, conformer_conv(P_ctx, p), mlstm_out(h_c, P_ctx['ml_o'], p['ml_norm_g'])],
        p['w_branch'], p['w_out'])
    return y_lat, y_ctx


def expert_choice_ffn(h, w_router, w_gate, w_up, w_down):
    B, T, D = h.shape
    cap = CAPACITY_FACTOR * T // N_EXPERTS
    aff = jax.nn.softmax(jnp.einsum('btd,de->bte', h, w_router).astype(jnp.float32), axis=-1)
    g, idx = lax.top_k(jnp.swapaxes(aff, 1, 2), cap)
    xe = jax.vmap(lambda hb, ib: hb[ib])(h, idx)
    hid = jax.nn.silu(jnp.einsum('becd,edf->becf', xe, w_gate)) * jnp.einsum('becd,edf->becf', xe, w_up)
    ye = jnp.einsum('becf,efd->becd', hid, w_down) * g[..., None].astype(h.dtype)

    def combine(ib, yb):
        return jnp.zeros((T, D), yb.dtype).at[ib.reshape(-1)].add(yb.reshape(-1, D))

    return jax.vmap(combine)(idx, ye)


def setup_inputs(seed: int = 0) -> dict:
    key = jax.random.key(seed)
    keys = iter(jax.random.split(key, 32))

    def nrm(shape, scale):
        return jax.random.normal(next(keys), shape, jnp.float32) * scale

    def gain(shape):
        return 1.0 + nrm(shape, 0.01)

    L, D, BW, E, F = DEPTH, D_MODEL, BRANCH_WIDTH, N_EXPERTS, EXPERT_FF
    ml_gate_bias = nrm((L, ML_N_GATES, ML_HEADS), 0.1).at[:, 1::2].add(3.0)
    return {
        'x': nrm((BATCH, SEQ, D), 1.0),
        'c': nrm((BATCH, D), 1.0),
        'ctx': nrm((BATCH, CTX_LEN, D), 1.0),
        'c_ctx': nrm((D,), 1.0),
        'w_mod': nrm((L, D, N_MOD * D), 0.5 * D ** -0.5),
        'b_mod': nrm((L, N_MOD * D), 0.01),
        'w_in': nrm((L, D, IN_COLS), D ** -0.5),
        'att_q_gain': gain((L, HEAD_DIM)),
        'att_k_gain': gain((L, HEAD_DIM)),
        'gm_ln_g': gain((L, BW)),
        'gm_ln_b': nrm((L, BW), 0.01),
        'gm_w_s': nrm((L, GM_GROUPS, GM_CHUNK, GM_CHUNK), 0.5 * GM_CHUNK ** -0.5),
        'gm_b_s': gain((L, GM_GROUPS, GM_CHUNK)),
        'conv_w': nrm((L, CONV_K, BW), CONV_K ** -0.5),
        'conv_b': nrm((L, BW), 0.01),
        'conv_ln_g': gain((L, BW)),
        'conv_ln_b': nrm((L, BW), 0.01),
        'ml_gate_bias': ml_gate_bias,
        'ml_norm_g': gain((L, BW)),
        'w_branch': nrm((L, N_BRANCH, BW, D), BETA * BW ** -0.5),
        'w_out': nrm((L, D, D), BETA * D ** -0.5),
        'ln1_g': gain((L, D)),
        'ln1_b': nrm((L, D), 0.01),
        'w_router': nrm((L, D, E), D ** -0.5),
        'w_gate': nrm((L, E, D, F), D ** -0.5),
        'w_up': nrm((L, E, D, F), D ** -0.5),
        'w_down': nrm((L, E, F, D), BETA * F ** -0.5),
        'ln2_g': gain((L, D)),
        'ln2_b': nrm((L, D), 0.01),
    }


def reference(x, c, ctx, c_ctx, w_mod, b_mod, w_in, att_q_gain, att_k_gain, gm_ln_g, gm_ln_b,
              gm_w_s, gm_b_s, conv_w, conv_b, conv_ln_g, conv_ln_b, ml_gate_bias, ml_norm_g,
              w_branch, w_out, ln1_g, ln1_b, w_router, w_gate, w_up, w_down, ln2_g, ln2_b):
    rope = axial_rope_tables(x.shape[1])
    xc = ctx
    for l in range(DEPTH):
        last = l == DEPTH - 1
        mod_lat = jnp.split((jax.nn.silu(c) @ w_mod[l] + b_mod[l])[:, None, :], N_MOD, axis=-1)
        mod_ctx = jnp.split((jax.nn.silu(c_ctx) @ w_mod[l] + b_mod[l])[None, None, :], N_MOD, axis=-1)
        p = {
            'w_in': w_in[l], 'att_q_gain': att_q_gain[l], 'att_k_gain': att_k_gain[l],
            'gm_ln_g': gm_ln_g[l], 'gm_ln_b': gm_ln_b[l], 'gm_w_s': gm_w_s[l], 'gm_b_s': gm_b_s[l],
            'conv_w': conv_w[l], 'conv_b': conv_b[l], 'conv_ln_g': conv_ln_g[l], 'conv_ln_b': conv_ln_b[l],
            'ml_gate_bias': ml_gate_bias[l], 'ml_norm_g': ml_norm_g[l],
            'w_branch': w_branch[l], 'w_out': w_out[l],
        }
        y_lat, y_ctx = token_mixer(modulate(x, mod_lat[0], mod_lat[1]),
                                   modulate(xc, mod_ctx[0], mod_ctx[1]), p, rope, not last)
        x = layer_norm(ALPHA * x + mod_lat[2] * y_lat, ln1_g[l], ln1_b[l])
        y_lat = expert_choice_ffn(modulate(x, mod_lat[3], mod_lat[4]), w_router[l], w_gate[l], w_up[l], w_down[l])
        x = layer_norm(ALPHA * x + mod_lat[5] * y_lat, ln2_g[l], ln2_b[l])
        if not last:
            xc = layer_norm(ALPHA * xc + mod_ctx[2] * y_ctx, ln1_g[l], ln1_b[l])
            y_c = expert_choice_ffn(modulate(xc, mod_ctx[3], mod_ctx[4]), w_router[l], w_gate[l], w_up[l], w_down[l])
            xc = layer_norm(ALPHA * xc + mod_ctx[5] * y_c, ln2_g[l], ln2_b[l])
    return x
```

```python
import functools

import jax
import jax.numpy as jnp
from jax import lax
from jax.experimental import pallas as pl
from jax.experimental.pallas import tpu as pltpu

F32 = jnp.float32
BF16 = jnp.bfloat16

D_MODEL = 4096
SEQ = 8192
DEPTH = 2
GRID_W = 64
CTX_LEN = 256
N_BRANCH = 4
BRANCH_WIDTH = D_MODEL // N_BRANCH
HEAD_DIM = 128
GM_CHUNK = 128
GM_GROUPS = BRANCH_WIDTH // HEAD_DIM
ATT_HEADS = BRANCH_WIDTH // HEAD_DIM
ATT_KV_HEADS = 2
ATT_REP = ATT_HEADS // ATT_KV_HEADS
KV_WIDTH = ATT_KV_HEADS * HEAD_DIM
ROPE_THETA = 10000.0
CONV_K = 31
ML_HEADS = BRANCH_WIDTH // HEAD_DIM
ML_HEAD_DIM = HEAD_DIM
ML_CHUNK = 128
ML_N_GATES = 4
N_EXPERTS = 16
EXPERT_FF = D_MODEL // 4
CAPACITY_FACTOR = 2
ALPHA = (2 * DEPTH) ** 0.25
EPS = 1e-6
N_MOD = 6

N_TOK = SEQ + CTX_LEN
SEG1 = 9 * BRANCH_WIDTH + 2 * KV_WIDTH
GATE_COLS = ML_N_GATES * ML_HEADS
MERGE_START = SEG1 + GATE_COLS

V7X_VMEM_LIMIT = 56 * 1024 * 1024


def _mm_kernel(a_ref, b_ref, o_ref):
    o_ref[...] = jnp.dot(a_ref[...], b_ref[...], preferred_element_type=F32).astype(o_ref.dtype)


def matmul(a, b, tm, tn, out_dtype=F32):
    M, K = a.shape
    _, N = b.shape
    assert M % tm == 0 and N % tn == 0, (M, N, tm, tn)
    return pl.pallas_call(
        _mm_kernel,
        grid=(N // tn, M // tm),
        in_specs=[pl.BlockSpec((tm, K), lambda j, i: (i, 0)),
                  pl.BlockSpec((K, tn), lambda j, i: (0, j))],
        out_specs=pl.BlockSpec((tm, tn), lambda j, i: (i, j)),
        out_shape=jax.ShapeDtypeStruct((M, N), out_dtype),
        compiler_params=pltpu.CompilerParams(
            dimension_semantics=("parallel", "parallel"), vmem_limit_bytes=V7X_VMEM_LIMIT),
    )(a, b)


def _bmm_kernel(a_ref, b_ref, o_ref):
    o_ref[...] = jnp.dot(a_ref[...], b_ref[...], preferred_element_type=F32).astype(o_ref.dtype)


def batched_matmul(a, b, tn, out_dtype=F32):
    E, M, K = a.shape
    _, _, N = b.shape
    assert N % tn == 0
    return pl.pallas_call(
        _bmm_kernel,
        grid=(E, N // tn),
        in_specs=[pl.BlockSpec((None, M, K), lambda e, j: (e, 0, 0)),
                  pl.BlockSpec((None, K, tn), lambda e, j: (e, 0, j))],
        out_specs=pl.BlockSpec((None, M, tn), lambda e, j: (e, 0, j)),
        out_shape=jax.ShapeDtypeStruct((E, M, N), out_dtype),
        compiler_params=pltpu.CompilerParams(
            dimension_semantics=("parallel", "parallel"), vmem_limit_bytes=V7X_VMEM_LIMIT),
    )(a, b)


def _swiglu_kernel(x_ref, wg_ref, wu_ref, o_ref):
    x = x_ref[...]
    g = jnp.dot(x, wg_ref[...], preferred_element_type=F32)
    u = jnp.dot(x, wu_ref[...], preferred_element_type=F32)
    o_ref[...] = (g * jax.nn.sigmoid(g) * u).astype(o_ref.dtype)


def expert_swiglu(xe, wg, wu, tn):
    E, M, K = xe.shape
    N = wg.shape[2]
    return pl.pallas_call(
        _swiglu_kernel,
        grid=(E, N // tn),
        in_specs=[pl.BlockSpec((None, M, K), lambda e, j: (e, 0, 0)),
                  pl.BlockSpec((None, K, tn), lambda e, j: (e, 0, j)),
                  pl.BlockSpec((None, K, tn), lambda e, j: (e, 0, j))],
        out_specs=pl.BlockSpec((None, M, tn), lambda e, j: (e, 0, j)),
        out_shape=jax.ShapeDtypeStruct((E, M, N), BF16),
        compiler_params=pltpu.CompilerParams(
            dimension_semantics=("parallel", "parallel"), vmem_limit_bytes=V7X_VMEM_LIMIT),
    )(xe, wg, wu)


def _attn_kernel(q_ref, k1_ref, v1_ref, *rest, tq, ck, n_chunks, has_ctx):
    if has_ctx:
        k2_ref, v2_ref, o_ref = rest
    else:
        (o_ref,) = rest
    scale = HEAD_DIM ** -0.5
    rows = ATT_REP * tq
    q = jnp.concatenate([q_ref[:, h * HEAD_DIM:(h + 1) * HEAD_DIM] for h in range(ATT_REP)], axis=0)

    def step(k, v, carry):
        m, l, acc = carry
        s = lax.dot_general(q, k, (((1,), (1,)), ((), ())), preferred_element_type=F32) * scale
        m_new = jnp.maximum(m, jnp.max(s, axis=-1, keepdims=True))
        p = jnp.exp(s - m_new)
        a = jnp.exp(m - m_new)
        l = a * l + jnp.sum(p, axis=-1, keepdims=True)
        acc = a * acc + jnp.dot(p.astype(BF16), v, preferred_element_type=F32)
        return m_new, l, acc

    def body(c, carry):
        off = pl.multiple_of(c * ck, ck)
        return step(k1_ref[pl.ds(off, ck), :], v1_ref[pl.ds(off, ck), :], carry)

    carry = (jnp.full((rows, 1), -jnp.inf, F32), jnp.zeros((rows, 1), F32),
             jnp.zeros((rows, HEAD_DIM), F32))
    carry = lax.fori_loop(0, n_chunks, body, carry)
    if has_ctx:
        carry = step(k2_ref[...], v2_ref[...], carry)
    _, l, acc = carry
    o = acc / l
    for h in range(ATT_REP):
        o_ref[:, h * HEAD_DIM:(h + 1) * HEAD_DIM] = o[h * tq:(h + 1) * tq].astype(o_ref.dtype)


def attention(q, k1, v1, k2=None, v2=None, *, tq, ck):
    T = q.shape[0]
    Tk = k1.shape[0]
    has_ctx = k2 is not None
    gw = ATT_REP * HEAD_DIM
    in_specs = [pl.BlockSpec((tq, gw), lambda g, i: (i, g)),
                pl.BlockSpec((Tk, HEAD_DIM), lambda g, i: (0, g)),
                pl.BlockSpec((Tk, HEAD_DIM), lambda g, i: (0, g))]
    args = [q, k1, v1]
    if has_ctx:
        Tc = k2.shape[0]
        in_specs += [pl.BlockSpec((Tc, HEAD_DIM), lambda g, i: (0, g)),
                     pl.BlockSpec((Tc, HEAD_DIM), lambda g, i: (0, g))]
        args += [k2, v2]
    kern = functools.partial(_attn_kernel, tq=tq, ck=ck, n_chunks=Tk // ck, has_ctx=has_ctx)
    return pl.pallas_call(
        kern,
        grid=(ATT_KV_HEADS, T // tq),
        in_specs=in_specs,
        out_specs=pl.BlockSpec((tq, gw), lambda g, i: (i, g)),
        out_shape=jax.ShapeDtypeStruct((T, ATT_HEADS * HEAD_DIM), BF16),
        compiler_params=pltpu.CompilerParams(
            dimension_semantics=("parallel", "parallel"), vmem_limit_bytes=V7X_VMEM_LIMIT),
    )(*args)


def _norm_stats(x):
    xf = x.astype(F32)
    xc = xf - jnp.mean(xf, axis=-1, keepdims=True)
    return xc * lax.rsqrt(jnp.mean(xc * xc, axis=-1, keepdims=True) + EPS)


def _layer_norm(x, g, b):
    return _norm_stats(x) * g + b


def _rms_norm(x, g):
    return x * lax.rsqrt(jnp.mean(x * x, axis=-1, keepdims=True) + EPS) * g


def _rope_tables(n_tokens):
    rows = n_tokens // GRID_W
    row = jnp.repeat(jnp.arange(rows), GRID_W).astype(F32)
    col = (jnp.arange(n_tokens) % GRID_W).astype(F32)
    n_freq = HEAD_DIM // 4
    inv = ROPE_THETA ** (-jnp.arange(n_freq, dtype=F32) / n_freq)
    ang_r = row[:, None] * inv[None, :]
    ang_c = col[:, None] * inv[None, :]
    return jnp.cos(ang_r), jnp.sin(ang_r), jnp.cos(ang_c), jnp.sin(ang_c)


def _apply_rope(x, rope):
    cos_r, sin_r, cos_c, sin_c = rope

    def rotate(xh, cos, sin):
        x1, x2 = jnp.split(xh, 2, axis=-1)
        cos = cos[:, None, :]
        sin = sin[:, None, :]
        return jnp.concatenate([x1 * cos - x2 * sin, x2 * cos + x1 * sin], axis=-1)

    x_row, x_col = jnp.split(x, 2, axis=-1)
    return jnp.concatenate([rotate(x_row, cos_r, sin_r), rotate(x_col, cos_c, sin_c)], axis=-1)


def _chunk_gmlp(gm_u, gm_v, p):
    T = gm_u.shape[0]
    u = jax.nn.gelu(gm_u)
    v = _layer_norm(jax.nn.gelu(gm_v), p['gm_ln_g'], p['gm_ln_b'])
    v = v.reshape(T // GM_CHUNK, GM_CHUNK, GM_GROUPS, HEAD_DIM)
    s = jnp.einsum('gpq,nqgc->npgc', p['gm_w_s'], v)
    s = s + p['gm_b_s'].T[None, :, :, None]
    return u * s.reshape(T, BRANCH_WIDTH)


def _conformer_conv(cv_a, cv_b, p):
    g = (cv_a * jax.nn.sigmoid(cv_b))[None]
    w = p['conv_w'][:, None, :]
    y = lax.conv_general_dilated(g, w, window_strides=(1,), padding=[(CONV_K // 2, CONV_K // 2)],
                                 dimension_numbers=('NWC', 'WIO', 'NWC'),
                                 feature_group_count=BRANCH_WIDTH)[0]
    y = y + p['conv_b']
    return jax.nn.silu(_layer_norm(y, p['conv_ln_g'], p['conv_ln_b']))


def _mlstm_prep(ml_q, ml_k, ml_v, ml_gates, gate_bias):
    T = ml_q.shape[0]

    def heads(a):
        return jnp.swapaxes(a.reshape(T, ML_HEADS, ML_HEAD_DIM), 0, 1)

    q = heads(ml_q)
    k = heads(ml_k) * (ML_HEAD_DIM ** -0.5)
    v = heads(ml_v)
    g = ml_gates.reshape(T, ML_N_GATES, ML_HEADS) + gate_bias
    g = jnp.transpose(g, (1, 2, 0))
    fw = (g[0], jax.nn.log_sigmoid(g[1]))
    bw = (g[2], jax.nn.log_sigmoid(g[3]))
    return q, k, v, fw, bw


def _mlstm_chunk_scan(q, k, v, log_i, log_f, state):
    H, T, d = q.shape
    nc = T // ML_CHUNK

    def chunks(a):
        return jnp.moveaxis(a.reshape(H, nc, ML_CHUNK, *a.shape[2:]), 1, 0)

    lower = jnp.tril(jnp.ones((ML_CHUNK, ML_CHUNK), dtype=bool))

    def step(carry, inp):
        C, n, m = carry
        qc, kc, vc, ic, fc = inp
        b = jnp.cumsum(fc, axis=-1)
        inter = b + m[..., None]
        dlog = b[..., :, None] - b[..., None, :] + ic[..., None, :]
        dlog = jnp.where(lower, dlog, -jnp.inf)
        mj = jnp.maximum(inter, jnp.max(dlog, axis=-1))
        w_inter = jnp.exp(inter - mj)
        s = jnp.einsum('hjd,hsd->hjs', qc, kc) * jnp.exp(dlog - mj[..., None])
        num = (w_inter[..., None] * jnp.einsum('hjd,hde->hje', qc, C)
               + jnp.einsum('hjs,hse->hje', s, vc))
        den = w_inter * jnp.einsum('hjd,hd->hj', qc, n) + jnp.sum(s, axis=-1)
        h = num / jnp.maximum(jnp.abs(den), jnp.exp(-mj))[..., None]
        m_new = mj[..., -1]
        w_c = jnp.exp(b[..., -1] + m - m_new)
        w_s = jnp.exp(b[..., -1:] - b + ic - m_new[..., None])
        C_new = w_c[..., None, None] * C + jnp.einsum('hs,hsd,hse->hde', w_s, kc, vc)
        n_new = w_c[..., None] * n + jnp.einsum('hs,hsd->hd', w_s, kc)
        return (C_new, n_new, m_new), h

    final, h = lax.scan(step, state, (chunks(q), chunks(k), chunks(v), chunks(log_i), chunks(log_f)))
    return jnp.moveaxis(h, 0, 1).reshape(H, T, d), final


def _mlstm_bidirectional(q, k, v, fw, bw, state_f, state_b):
    h_f, fin_f = _mlstm_chunk_scan(q, k, v, fw[0], fw[1], state_f)

    def rev(a):
        return jnp.flip(a, axis=1)

    h_b, fin_b = _mlstm_chunk_scan(rev(q), rev(k), rev(v), rev(bw[0]), rev(bw[1]), state_b)
    return h_f + rev(h_b), fin_f, fin_b


def _mlstm_out(h, o_pre, norm_g):
    H, T, d = h.shape
    hn = _norm_stats(jnp.swapaxes(h, 0, 1)) * norm_g.reshape(H, d)
    return jax.nn.sigmoid(o_pre) * hn.reshape(T, H * d)


def _expert_choice_route(h, w_router, cap):
    logits = jnp.dot(h, w_router, precision=lax.Precision.HIGHEST)
    aff = jax.nn.softmax(logits, axis=-1)
    return lax.top_k(aff.T, cap)


def kernel(x, c, ctx, c_ctx, w_mod, b_mod, w_in, att_q_gain, att_k_gain, gm_ln_g, gm_ln_b,
           gm_w_s, gm_b_s, conv_w, conv_b, conv_ln_g, conv_ln_b, ml_gate_bias, ml_norm_g,
           w_branch, w_out, ln1_g, ln1_b, w_router, w_gate, w_up, w_down, ln2_g, ln2_b):
    rope = _rope_tables(SEQ)
    xl = x[0]
    xc = ctx[0]
    for l in range(DEPTH):
        last = l == DEPTH - 1
        p = {
            'gm_ln_g': gm_ln_g[l], 'gm_ln_b': gm_ln_b[l], 'gm_w_s': gm_w_s[l], 'gm_b_s': gm_b_s[l],
            'conv_w': conv_w[l], 'conv_b': conv_b[l], 'conv_ln_g': conv_ln_g[l], 'conv_ln_b': conv_ln_b[l],
        }
        cc = jnp.zeros((16, D_MODEL), F32).at[0].set(jax.nn.silu(c[0])).at[1].set(jax.nn.silu(c_ctx))
        mod = matmul(cc.astype(BF16), w_mod[l].astype(BF16), 16, 2048)[:2] + b_mod[l]
        mod_lat = jnp.split(mod[0], N_MOD)
        mod_ctx = jnp.split(mod[1], N_MOD)

        h_all = jnp.concatenate([xl * (1.0 + mod_lat[1]) + mod_lat[0],
                                 xc * (1.0 + mod_ctx[1]) + mod_ctx[0]], axis=0).astype(BF16)
        w1 = w_in[l][:, :SEG1].astype(BF16)
        wg = jnp.pad(w_in[l][:, SEG1:MERGE_START], ((0, 0), (0, 128 - GATE_COLS))).astype(BF16)
        wm = w_in[l][:, MERGE_START:].astype(BF16)
        P1 = matmul(h_all, w1, 1056, 512)
        Pg = matmul(h_all, wg, 1056, 128)[:, :GATE_COLS]
        Pm = matmul(h_all, wm, 1056, 1024)

        def seg(i0, w):
            return P1[:, i0:i0 + w]

        BW = BRANCH_WIDTH
        gm_u, gm_v = seg(0, BW), seg(BW, BW)
        att_q, att_k, att_v = seg(2 * BW, BW), seg(3 * BW, KV_WIDTH), seg(3 * BW + KV_WIDTH, KV_WIDTH)
        o = 3 * BW + 2 * KV_WIDTH
        cv_a, cv_b = seg(o, BW), seg(o + BW, BW)
        ml_q, ml_k, ml_v, ml_o = seg(o + 2 * BW, BW), seg(o + 3 * BW, BW), seg(o + 4 * BW, BW), seg(o + 5 * BW, BW)

        qn = _rms_norm(att_q.reshape(N_TOK, ATT_HEADS, HEAD_DIM), att_q_gain[l])
        kn = _rms_norm(att_k.reshape(N_TOK, ATT_KV_HEADS, HEAD_DIM), att_k_gain[l])
        q_l = _apply_rope(qn[:SEQ], rope).reshape(SEQ, BW).astype(BF16)
        k_l = _apply_rope(kn[:SEQ], rope).reshape(SEQ, KV_WIDTH).astype(BF16)
        q_c = qn[SEQ:].reshape(CTX_LEN, BW).astype(BF16)
        k_c = kn[SEQ:].reshape(CTX_LEN, KV_WIDTH).astype(BF16)
        v_l = att_v[:SEQ].astype(BF16)
        v_c = att_v[SEQ:].astype(BF16)
        att_lat = attention(q_l, k_l, v_l, k_c, v_c, tq=128, ck=512)
        zero = (jnp.zeros((ML_HEADS, ML_HEAD_DIM, ML_HEAD_DIM), F32), jnp.zeros((ML_HEADS, ML_HEAD_DIM), F32),
                jnp.zeros((ML_HEADS,), F32))
        mq, mk, mv, fw, bw = _mlstm_prep(ml_q[SEQ:], ml_k[SEQ:], ml_v[SEQ:], Pg[SEQ:], ml_gate_bias[l])
        h_c, st_f, st_b = _mlstm_bidirectional(mq, mk, mv, fw, bw, zero, zero)
        mq, mk, mv, fw, bw = _mlstm_prep(ml_q[:SEQ], ml_k[:SEQ], ml_v[:SEQ], Pg[:SEQ], ml_gate_bias[l])
        h_l, _, _ = _mlstm_bidirectional(mq, mk, mv, fw, bw, st_f, st_b)

        def merge(rows, branches):
            gates = jax.nn.sigmoid(Pm[rows].reshape(-1, N_BRANCH, D_MODEL))
            n = gates.shape[0]
            tm = 1024 if n % 1024 == 0 else n
            y = 0.0
            for i in range(N_BRANCH):
                y = y + gates[:, i] * matmul(branches[i].astype(BF16), w_branch[l, i].astype(BF16), tm, 1024)
            return matmul(y.astype(BF16), w_out[l].astype(BF16), tm, 1024)

        lat = slice(0, SEQ)
        cx = slice(SEQ, N_TOK)
        y_lat = merge(lat, [_chunk_gmlp(gm_u[lat], gm_v[lat], p), att_lat,
                            _conformer_conv(cv_a[lat], cv_b[lat], p),
                            _mlstm_out(h_l, ml_o[lat], ml_norm_g[l])])
        xl = _layer_norm(ALPHA * xl + mod_lat[2] * y_lat, ln1_g[l], ln1_b[l])
        if not last:
            att_ctx = attention(q_c, k_c, v_c, tq=128, ck=256)
            y_ctx = merge(cx, [_chunk_gmlp(gm_u[cx], gm_v[cx], p), att_ctx,
                               _conformer_conv(cv_a[cx], cv_b[cx], p),
                               _mlstm_out(h_c, ml_o[cx], ml_norm_g[l])])
            xc = _layer_norm(ALPHA * xc + mod_ctx[2] * y_ctx, ln1_g[l], ln1_b[l])

        wgate = w_gate[l].astype(BF16)
        wup = w_up[l].astype(BF16)
        wdown = w_down[l].astype(BF16)
        hm_l = xl * (1.0 + mod_lat[4]) + mod_lat[3]
        g_l, idx_l = _expert_choice_route(hm_l, w_router[l], CAPACITY_FACTOR * SEQ // N_EXPERTS)
        xe = hm_l[idx_l]
        if not last:
            hm_c = xc * (1.0 + mod_ctx[4]) + mod_ctx[3]
            g_c, idx_c = _expert_choice_route(hm_c, w_router[l], CAPACITY_FACTOR * CTX_LEN // N_EXPERTS)
            xe = jnp.concatenate([xe, hm_c[idx_c]], axis=1)
        hid = expert_swiglu(xe.astype(BF16), wgate, wup, 512)
        ye = batched_matmul(hid, wdown, 1024)
        cap_l = idx_l.shape[1]
        ye_l = ye[:, :cap_l] * g_l[..., None]
        y_moe = jnp.zeros((SEQ, D_MODEL), F32).at[idx_l.reshape(-1)].add(ye_l.reshape(-1, D_MODEL))
        xl = _layer_norm(ALPHA * xl + mod_lat[5] * y_moe, ln2_g[l], ln2_b[l])
        if not last:
            ye_c = ye[:, cap_l:] * g_c[..., None]
            y_moe_c = jnp.zeros((CTX_LEN, D_MODEL), F32).at[idx_c.reshape(-1)].add(ye_c.reshape(-1, D_MODEL))
            xc = _layer_norm(ALPHA * xc + mod_ctx[5] * y_moe_c, ln2_g[l], ln2_b[l])
    return xl[None]
```

```python
import functools

import jax
import jax.numpy as jnp
from jax import lax
from jax.experimental import pallas as pl
from jax.experimental.pallas import tpu as pltpu

F32 = jnp.float32
BF16 = jnp.bfloat16

D_MODEL = 4096
SEQ = 8192
DEPTH = 2
GRID_W = 64
CTX_LEN = 256
N_BRANCH = 4
BRANCH_WIDTH = D_MODEL // N_BRANCH
HEAD_DIM = 128
GM_CHUNK = 128
GM_GROUPS = BRANCH_WIDTH // HEAD_DIM
ATT_HEADS = BRANCH_WIDTH // HEAD_DIM
ATT_KV_HEADS = 2
ATT_REP = ATT_HEADS // ATT_KV_HEADS
KV_WIDTH = ATT_KV_HEADS * HEAD_DIM
ROPE_THETA = 10000.0
CONV_K = 31
ML_HEADS = BRANCH_WIDTH // HEAD_DIM
ML_HEAD_DIM = HEAD_DIM
ML_CHUNK = 128
ML_N_GATES = 4
N_EXPERTS = 16
EXPERT_FF = D_MODEL // 4
CAPACITY_FACTOR = 2
ALPHA = (2 * DEPTH) ** 0.25
EPS = 1e-6
N_MOD = 6

N_TOK = SEQ + CTX_LEN
SEG1 = 9 * BRANCH_WIDTH + 2 * KV_WIDTH
GATE_COLS = ML_N_GATES * ML_HEADS
MERGE_START = SEG1 + GATE_COLS

V7X_VMEM_LIMIT = 56 * 1024 * 1024
LANES = 128

GATE_TOK_TILE = next(t for t in (1408, 1280, 1024, 768, 512, 384, 256, 128) if N_TOK % t == 0)
ROW_BLK = 256
N_ROW_BLKS = N_TOK // ROW_BLK
LAT_ROW_BLKS = SEQ // ROW_BLK

P_TILE = 512
P_KV_TILE_IN = (3 * BRANCH_WIDTH) // P_TILE
P_N_TILES = SEG1 // P_TILE
COL_GM_U, COL_GM_V, COL_ATT_Q = 0, 1, 2
COL_CV_A, COL_CV_B, COL_ML_Q, COL_ML_K, COL_ML_V, COL_ML_O = 3, 4, 5, 6, 7, 8
P_K_COL = 9 * BRANCH_WIDTH


def _cparams(*sem):
    return pltpu.CompilerParams(dimension_semantics=sem, vmem_limit_bytes=V7X_VMEM_LIMIT)


def _sigmoid(x):
    return 1.0 / (1.0 + jnp.exp(-x))


def _gelu_tanh(x):
    return 0.5 * x * (1.0 + jnp.tanh(0.7978845608028654 * (x + 0.044715 * (x * x * x))))


def _ln_rows(z, g, b):
    mu = jnp.mean(z, axis=-1, keepdims=True)
    zc = z - mu
    var = jnp.mean(zc * zc, axis=-1, keepdims=True)
    return zc * lax.rsqrt(var + EPS) * g + b


def _pick(ref, is_ctx):
    return jnp.where(is_ctx, ref[1:2, :], ref[0:1, :])


def _mod_kernel(c_ref, w_ref, b_ref, o_ref):
    c = c_ref[...]
    a = (c * _sigmoid(c)).astype(BF16)
    o_ref[...] = jnp.dot(a, w_ref[...].astype(BF16), preferred_element_type=F32) + b_ref[...]


def modulation(cc, w_mod, b_mod, tn=1024):
    L, D, N = w_mod.shape
    return pl.pallas_call(
        _mod_kernel,
        grid=(L, N // tn),
        in_specs=[pl.BlockSpec((16, D), lambda l, j: (0, 0)),
                  pl.BlockSpec((None, D, tn), lambda l, j: (l, 0, j)),
                  pl.BlockSpec((None, 1, tn), lambda l, j: (l, 0, j))],
        out_specs=pl.BlockSpec((None, 16, tn), lambda l, j: (l, 0, j)),
        out_shape=jax.ShapeDtypeStruct((L, 16, N), F32),
        compiler_params=_cparams("parallel", "parallel"),
    )(cc, w_mod, b_mod.reshape(L, 1, N))


def _modulate_kernel(x_ref, sh_ref, sc_ref, o_ref):
    is_ctx = pl.program_id(0) >= LAT_ROW_BLKS
    o_ref[...] = (x_ref[...] * (1.0 + _pick(sc_ref, is_ctx)) + _pick(sh_ref, is_ctx)).astype(o_ref.dtype)


def modulate(x_all, mod_l):
    D = D_MODEL
    return pl.pallas_call(
        _modulate_kernel,
        grid=(N_ROW_BLKS,),
        in_specs=[pl.BlockSpec((ROW_BLK, D), lambda i: (i, 0)),
                  pl.BlockSpec((16, D), lambda i: (0, 0)),
                  pl.BlockSpec((16, D), lambda i: (0, 1))],
        out_specs=pl.BlockSpec((ROW_BLK, D), lambda i: (i, 0)),
        out_shape=jax.ShapeDtypeStruct((N_TOK, D), BF16),
        compiler_params=_cparams("parallel"),
    )(x_all, mod_l, mod_l)


def _proj_kernel(a_ref, w_ref, o_ref, wb_ref):
    @pl.when(pl.program_id(1) == 0)
    def _():
        wb_ref[...] = w_ref[...].astype(BF16)

    o_ref[...] = jnp.dot(a_ref[...], wb_ref[...], preferred_element_type=F32).astype(o_ref.dtype)


def _p_out_tile(j):
    return jnp.where(j < P_KV_TILE_IN, j, jnp.where(j == P_KV_TILE_IN, P_N_TILES - 1, j - 1))


def branch_projection(h_all, w_in, layer, tm=N_TOK // 8):
    M, K = h_all.shape
    return pl.pallas_call(
        _proj_kernel,
        grid=(P_N_TILES, M // tm),
        in_specs=[pl.BlockSpec((tm, K), lambda j, i: (i, 0)),
                  pl.BlockSpec((None, K, P_TILE), lambda j, i: (layer, 0, j))],
        out_specs=pl.BlockSpec((tm, P_TILE), lambda j, i: (i, _p_out_tile(j))),
        out_shape=jax.ShapeDtypeStruct((M, SEG1), BF16),
        scratch_shapes=[pltpu.VMEM((K, P_TILE), BF16)],
        compiler_params=_cparams("parallel", "arbitrary"),
    )(h_all, w_in)


def _gates_kernel(w_ref, h_ref, bias_ref, o_ref, *, n_chunks):
    z = lax.dot_general(w_ref[...], h_ref[...], (((1,), (1,)), ((), ())), preferred_element_type=F32)
    z = z + bias_ref[:, 0:1]
    lf = jnp.minimum(z, 0.0) - jnp.log1p(jnp.exp(-jnp.abs(z)))
    row = lax.broadcasted_iota(jnp.int32, (ML_CHUNK, ML_CHUNK), 0)
    col = lax.broadcasted_iota(jnp.int32, (ML_CHUNK, ML_CHUNK), 1)
    upper = (row <= col).astype(F32)
    lower = (row >= col).astype(F32)
    H = ML_HEADS
    for c in range(n_chunks):
        sl = slice(c * ML_CHUNK, (c + 1) * ML_CHUNK)
        b_f = jnp.dot(lf[H:2 * H, sl], upper, preferred_element_type=F32, precision=lax.Precision.HIGHEST)
        b_b = jnp.dot(lf[3 * H:4 * H, sl], lower, preferred_element_type=F32, precision=lax.Precision.HIGHEST)
        o_ref[:, sl] = jnp.concatenate([b_f, z[0:H, sl] - b_f, b_b, z[2 * H:3 * H, sl] - b_b], axis=0)


def mlstm_gate_rows(h_all, w_gates_t, bias, tt=GATE_TOK_TILE):
    M, K = h_all.shape
    kern = functools.partial(_gates_kernel, n_chunks=tt // ML_CHUNK)
    return pl.pallas_call(
        kern,
        grid=(M // tt,),
        in_specs=[pl.BlockSpec((GATE_COLS, K), lambda i: (0, 0)),
                  pl.BlockSpec((tt, K), lambda i: (i, 0)),
                  pl.BlockSpec((GATE_COLS, LANES), lambda i: (0, 0))],
        out_specs=pl.BlockSpec((GATE_COLS, tt), lambda i: (0, i)),
        out_shape=jax.ShapeDtypeStruct((GATE_COLS, M), F32),
        compiler_params=_cparams("parallel"),
    )(w_gates_t, h_all, bias)


def _rope_head(x, cos, sin, first_half):
    partner = jnp.where(first_half, pltpu.roll(x, 96, 1), pltpu.roll(x, 32, 1))
    return x * cos + partner * sin


def _qk_kernel(q_ref, kv_ref, cos_ref, sin_ref, qg_ref, kg_ref, qo_ref, ko_ref):
    cos = cos_ref[...]
    sin = sin_ref[...]
    lane = lax.broadcasted_iota(jnp.int32, (ROW_BLK, HEAD_DIM), 1)
    first_half = jnp.bitwise_and(lane, 63) < 32

    def prep(x, gain, scale):
        x = x.astype(F32)
        y = x * lax.rsqrt(jnp.mean(x * x, axis=-1, keepdims=True) + EPS) * gain
        return (_rope_head(y, cos, sin, first_half) * scale).astype(BF16)

    for h in range(ATT_HEADS):
        sl = slice(h * HEAD_DIM, (h + 1) * HEAD_DIM)
        qo_ref[:, sl] = prep(q_ref[:, sl], qg_ref[...], HEAD_DIM ** -0.5)
    for h in range(ATT_KV_HEADS):
        sl = slice(h * HEAD_DIM, (h + 1) * HEAD_DIM)
        ko_ref[:, sl] = prep(kv_ref[:, sl], kg_ref[...], 1.0)


def qk_prepare(P, cos_t, sin_t, q_gain, k_gain):
    return pl.pallas_call(
        _qk_kernel,
        grid=(N_ROW_BLKS,),
        in_specs=[pl.BlockSpec((ROW_BLK, BRANCH_WIDTH), lambda i: (i, COL_ATT_Q)),
                  pl.BlockSpec((ROW_BLK, P_TILE), lambda i: (i, P_N_TILES - 1)),
                  pl.BlockSpec((ROW_BLK, HEAD_DIM), lambda i: (i, 0)),
                  pl.BlockSpec((ROW_BLK, HEAD_DIM), lambda i: (i, 0)),
                  pl.BlockSpec((1, HEAD_DIM), lambda i: (0, 0)),
                  pl.BlockSpec((1, HEAD_DIM), lambda i: (0, 0))],
        out_specs=[pl.BlockSpec((ROW_BLK, BRANCH_WIDTH), lambda i: (i, 0)),
                   pl.BlockSpec((ROW_BLK, KV_WIDTH), lambda i: (i, 0))],
        out_shape=[jax.ShapeDtypeStruct((N_TOK, BRANCH_WIDTH), BF16),
                   jax.ShapeDtypeStruct((N_TOK, KV_WIDTH), BF16)],
        compiler_params=_cparams("parallel"),
    )(P, P, cos_t, sin_t, q_gain.reshape(1, HEAD_DIM), k_gain.reshape(1, HEAD_DIM))


def _attn_kernel(q_ref, k1_ref, v1_ref, k2_ref, v2_ref, o_ref, *, tq, ck, n_chunks, n_lat_blocks):
    rows = ATT_REP * tq
    q = jnp.concatenate([q_ref[:, h * HEAD_DIM:(h + 1) * HEAD_DIM] for h in range(ATT_REP)], axis=0)

    def step(k, v, carry):
        m, l, acc = carry
        s = lax.dot_general(q, k, (((1,), (1,)), ((), ())), preferred_element_type=F32)
        m_new = jnp.maximum(m, jnp.max(s, axis=-1, keepdims=True))
        p = jnp.exp(s - m_new)
        a = jnp.exp(m - m_new)
        l = a * l + jnp.sum(p, axis=-1, keepdims=True)
        acc = a * acc + jnp.dot(p.astype(BF16), v, preferred_element_type=F32)
        return m_new, l, acc

    def body(c, carry):
        off = pl.multiple_of(c * ck, ck)
        return step(k1_ref[pl.ds(off, ck), :], v1_ref[pl.ds(off, ck), :], carry)

    carry = (jnp.full((rows, 1), -jnp.inf, F32), jnp.zeros((rows, 1), F32),
             jnp.zeros((rows, HEAD_DIM), F32))
    trips = jnp.where(pl.program_id(1) < n_lat_blocks, n_chunks, 0)
    carry = lax.fori_loop(0, trips, body, carry)
    _, l, acc = step(k2_ref[...], v2_ref[...], carry)
    o = acc / l
    for h in range(ATT_REP):
        o_ref[:, h * HEAD_DIM:(h + 1) * HEAD_DIM] = o[h * tq:(h + 1) * tq].astype(o_ref.dtype)


def attention(q, k, P, n_rows, tq=128, ck=512):
    gw = ATT_REP * HEAD_DIM
    v_col = (P_K_COL + KV_WIDTH) // HEAD_DIM
    ctx_blk = SEQ // CTX_LEN
    kern = functools.partial(_attn_kernel, tq=tq, ck=ck, n_chunks=SEQ // ck, n_lat_blocks=SEQ // tq)
    return pl.pallas_call(
        kern,
        grid=(ATT_KV_HEADS, n_rows // tq),
        in_specs=[pl.BlockSpec((tq, gw), lambda g, i: (i, g)),
                  pl.BlockSpec((SEQ, HEAD_DIM), lambda g, i: (0, g)),
                  pl.BlockSpec((SEQ, HEAD_DIM), lambda g, i: (0, v_col + g)),
                  pl.BlockSpec((CTX_LEN, HEAD_DIM), lambda g, i: (ctx_blk, g)),
                  pl.BlockSpec((CTX_LEN, HEAD_DIM), lambda g, i: (ctx_blk, v_col + g))],
        out_specs=pl.BlockSpec((tq, gw), lambda g, i: (i, g)),
        out_shape=jax.ShapeDtypeStruct((n_rows, ATT_HEADS * HEAD_DIM), BF16),
        compiler_params=_cparams("parallel", "parallel"),
    )(q, k, P, k, P)


def _gmlp_kernel(u_ref, v_ref, g_ref, b_ref, ws_ref, bs_ref, o_ref):
    u = _gelu_tanh(u_ref[...].astype(F32))
    v = _ln_rows(_gelu_tanh(v_ref[...].astype(F32)), g_ref[...], b_ref[...]).astype(BF16)
    for c in range(ROW_BLK // GM_CHUNK):
        rs = slice(c * GM_CHUNK, (c + 1) * GM_CHUNK)
        for g in range(GM_GROUPS):
            cs = slice(g * HEAD_DIM, (g + 1) * HEAD_DIM)
            s = jnp.dot(ws_ref[g], v[rs, cs], preferred_element_type=F32) + bs_ref[g]
            o_ref[rs, cs] = (u[rs, cs] * s).astype(o_ref.dtype)


def chunk_gmlp(P, ln_g, ln_b, w_s, b_s, n_rows):
    bs_rep = jnp.broadcast_to(b_s[:, :, None], (GM_GROUPS, GM_CHUNK, HEAD_DIM))
    BW = BRANCH_WIDTH
    return pl.pallas_call(
        _gmlp_kernel,
        grid=(n_rows // ROW_BLK,),
        in_specs=[pl.BlockSpec((ROW_BLK, BW), lambda i: (i, COL_GM_U)),
                  pl.BlockSpec((ROW_BLK, BW), lambda i: (i, COL_GM_V)),
                  pl.BlockSpec((1, BW), lambda i: (0, 0)),
                  pl.BlockSpec((1, BW), lambda i: (0, 0)),
                  pl.BlockSpec((GM_GROUPS, GM_CHUNK, GM_CHUNK), lambda i: (0, 0, 0)),
                  pl.BlockSpec((GM_GROUPS, GM_CHUNK, HEAD_DIM), lambda i: (0, 0, 0))],
        out_specs=pl.BlockSpec((ROW_BLK, BW), lambda i: (i, 0)),
        out_shape=jax.ShapeDtypeStruct((n_rows, BW), BF16),
        compiler_params=_cparams("parallel"),
    )(P, P, ln_g.reshape(1, BW), ln_b.reshape(1, BW), w_s.astype(BF16), bs_rep)


CONV_HALO = 16
CONV_SUB = 32


def _conv_kernel(a_ref, b_ref, ap_ref, bp_ref, an_ref, bn_ref, w_ref, cb_ref, g_ref, beta_ref, o_ref, gbuf):
    i = pl.program_id(0)

    def glu(a, b):
        return a.astype(F32) * _sigmoid(b.astype(F32))

    prev_ok = jnp.logical_and(i != 0, i != LAT_ROW_BLKS)
    next_ok = jnp.logical_and(i != LAT_ROW_BLKS - 1, i != N_ROW_BLKS - 1)
    gbuf[0:CONV_HALO, :] = jnp.where(prev_ok, glu(ap_ref[...], bp_ref[...]), 0.0)
    gbuf[CONV_HALO:CONV_HALO + ROW_BLK, :] = glu(a_ref[...], b_ref[...])
    gbuf[CONV_HALO + ROW_BLK:, :] = jnp.where(next_ok, glu(an_ref[...], bn_ref[...]), 0.0)
    first = CONV_HALO - CONV_K // 2
    for r in range(ROW_BLK // CONV_SUB):
        base = r * CONV_SUB
        acc = jnp.zeros((CONV_SUB, BRANCH_WIDTH), F32)
        for k in range(CONV_K):
            acc = acc + gbuf[base + first + k:base + first + k + CONV_SUB, :] * w_ref[k:k + 1, :]
        y = _ln_rows(acc + cb_ref[...], g_ref[...], beta_ref[...])
        o_ref[base:base + CONV_SUB, :] = (y * _sigmoid(y)).astype(o_ref.dtype)


def conformer_conv(P, conv_w, conv_b, ln_g, ln_b, n_rows):
    BW = BRANCH_WIDTH
    hpb = ROW_BLK // CONV_HALO
    last_halo = N_TOK // CONV_HALO - 1

    def prev_map(col):
        return lambda i: (jnp.maximum(i * hpb - 1, 0), col)

    def next_map(col):
        return lambda i: (jnp.minimum((i + 1) * hpb, last_halo), col)

    vec = pl.BlockSpec((1, BW), lambda i: (0, 0))
    return pl.pallas_call(
        _conv_kernel,
        grid=(n_rows // ROW_BLK,),
        in_specs=[pl.BlockSpec((ROW_BLK, BW), lambda i: (i, COL_CV_A)),
                  pl.BlockSpec((ROW_BLK, BW), lambda i: (i, COL_CV_B)),
                  pl.BlockSpec((CONV_HALO, BW), prev_map(COL_CV_A)),
                  pl.BlockSpec((CONV_HALO, BW), prev_map(COL_CV_B)),
                  pl.BlockSpec((CONV_HALO, BW), next_map(COL_CV_A)),
                  pl.BlockSpec((CONV_HALO, BW), next_map(COL_CV_B)),
                  pl.BlockSpec((CONV_K + 1, BW), lambda i: (0, 0)),
                  vec, vec, vec],
        out_specs=pl.BlockSpec((ROW_BLK, BW), lambda i: (i, 0)),
        out_shape=jax.ShapeDtypeStruct((n_rows, BW), BF16),
        scratch_shapes=[pltpu.VMEM((ROW_BLK + 2 * CONV_HALO, BW), F32)],
        compiler_params=_cparams("parallel"),
    )(P, P, P, P, P, P, jnp.pad(conv_w, ((0, 1), (0, 0))), conv_b.reshape(1, BW),
      ln_g.reshape(1, BW), ln_b.reshape(1, BW))


def _merge_kernel(h_ref, b0, b1, b2, b3, m0, m1, m2, m3, w0, w1, w2, w3, o_ref):
    h = h_ref[...]
    acc = None
    for br, wm, wb in ((b0, m0, w0), (b1, m1, w1), (b2, m2, w2), (b3, m3, w3)):
        gate = _sigmoid(jnp.dot(h, wm[...], preferred_element_type=F32))
        t = gate * jnp.dot(br[...], wb[...], preferred_element_type=F32)
        acc = t if acc is None else acc + t
    o_ref[...] = acc.astype(o_ref.dtype)


def merge_branches(h_all, branches, w_merge, w_branch, n_rows, tm, tn=256):
    D = D_MODEL
    BW = BRANCH_WIDTH
    nj = D // tn
    one = pl.Buffered(1)
    in_specs = [pl.BlockSpec((tm, D), lambda i, j: (i, 0), pipeline_mode=one)]
    in_specs += [pl.BlockSpec((tm, BW), lambda i, j: (i, 0), pipeline_mode=one) for _ in range(N_BRANCH)]
    in_specs += [pl.BlockSpec((D, tn), functools.partial(lambda i, j, b: (0, b * nj + j), b=b))
                 for b in range(N_BRANCH)]
    in_specs += [pl.BlockSpec((None, BW, tn), functools.partial(lambda i, j, b: (b, 0, j), b=b))
                 for b in range(N_BRANCH)]
    return pl.pallas_call(
        _merge_kernel,
        grid=(n_rows // tm, nj),
        in_specs=in_specs,
        out_specs=pl.BlockSpec((tm, tn), lambda i, j: (i, j)),
        out_shape=jax.ShapeDtypeStruct((n_rows, D), BF16),
        compiler_params=_cparams("parallel", "arbitrary"),
    )(h_all, *branches, *([w_merge] * N_BRANCH), *([w_branch] * N_BRANCH))


def _out_ln_kernel(y_ref, w_ref, x_ref, g1_ref, lg_ref, lb_ref, sh_ref, sc_ref, x1_ref, hm_ref, *, nk, tm):
    k = pl.program_id(1)
    part = jnp.dot(y_ref[...], w_ref[...], preferred_element_type=F32)

    @pl.when(k == 0)
    def _():
        x1_ref[...] = part

    @pl.when(k > 0)
    def _():
        x1_ref[...] += part

    @pl.when(k == nk - 1)
    def _():
        sub = next(s for s in (64, 48, 32, 16) if tm % s == 0)

        def chunk(r, carry):
            r0 = pl.multiple_of(r * sub, 16)
            rows = pl.ds(r0, sub)
            is_ctx = pl.program_id(0) * tm + r0 + lax.broadcasted_iota(jnp.int32, (sub, 1), 0) >= SEQ
            z = ALPHA * x_ref[rows, :] + _pick(g1_ref, is_ctx) * x1_ref[rows, :]
            x1 = _ln_rows(z, lg_ref[...], lb_ref[...])
            x1_ref[rows, :] = x1
            hm_ref[rows, :] = (x1 * (1.0 + _pick(sc_ref, is_ctx)) + _pick(sh_ref, is_ctx)).astype(hm_ref.dtype)
            return carry

        lax.fori_loop(0, tm // sub, chunk, 0)


def out_proj_ln(y, w_out, x_all, mod_l, ln_g, ln_b, n_rows, tm, tk=512):
    D = D_MODEL
    nk = D // tk
    kern = functools.partial(_out_ln_kernel, nk=nk, tm=tm)
    vec = pl.BlockSpec((1, D), lambda i, k: (0, 0))

    def modspec(col):
        return pl.BlockSpec((16, D), lambda i, k: (0, col))

    return pl.pallas_call(
        kern,
        grid=(n_rows // tm, nk),
        in_specs=[pl.BlockSpec((tm, tk), lambda i, k: (i, k)),
                  pl.BlockSpec((tk, D), lambda i, k: (k, 0)),
                  pl.BlockSpec((tm, D), lambda i, k: (i, 0), pipeline_mode=pl.Buffered(1)),
                  modspec(2), vec, vec, modspec(3), modspec(4)],
        out_specs=[pl.BlockSpec((tm, D), lambda i, k: (i, 0)),
                   pl.BlockSpec((tm, D), lambda i, k: (i, 0))],
        out_shape=[jax.ShapeDtypeStruct((n_rows, D), F32),
                   jax.ShapeDtypeStruct((n_rows, D), BF16)],
        compiler_params=_cparams("parallel", "arbitrary"),
    )(y, w_out, x_all, mod_l, ln_g.reshape(1, D), ln_b.reshape(1, D), mod_l, mod_l)


def _ln2_kernel(x_ref, y_ref, g2_ref, lg_ref, lb_ref, sh_ref, sc_ref, x2_ref, *h_ref, blk_ctx_from):
    is_ctx = pl.program_id(0) >= blk_ctx_from
    z = ALPHA * x_ref[...] + _pick(g2_ref, is_ctx) * y_ref[...]
    x2 = _ln_rows(z, lg_ref[...], lb_ref[...])
    x2_ref[...] = x2
    if h_ref:
        h_ref[0][...] = (x2 * (1.0 + _pick(sc_ref, is_ctx)) + _pick(sh_ref, is_ctx)).astype(BF16)


def ffn_residual_ln(x1, y_moe, mod_l, ln_g, ln_b, mod_next, n_rows):
    D = D_MODEL
    want_h = mod_next is not None
    nxt = mod_next if want_h else mod_l
    kern = functools.partial(_ln2_kernel, blk_ctx_from=LAT_ROW_BLKS)
    row = pl.BlockSpec((ROW_BLK, D), lambda i: (i, 0))
    vec = pl.BlockSpec((1, D), lambda i: (0, 0))
    out_specs = [row, row] if want_h else [row]
    out_shape = [jax.ShapeDtypeStruct((n_rows, D), F32)]
    if want_h:
        out_shape.append(jax.ShapeDtypeStruct((n_rows, D), BF16))
    return pl.pallas_call(
        kern,
        grid=(n_rows // ROW_BLK,),
        in_specs=[row, row, pl.BlockSpec((16, D), lambda i: (0, 5)), vec, vec,
                  pl.BlockSpec((16, D), lambda i: (0, 0)), pl.BlockSpec((16, D), lambda i: (0, 1))],
        out_specs=out_specs,
        out_shape=out_shape,
        compiler_params=_cparams("parallel"),
    )(x1, y_moe, mod_l, ln_g.reshape(1, D), ln_b.reshape(1, D), nxt, nxt)


def _swiglu_kernel(x_ref, wg_ref, wu_ref, o_ref):
    x = x_ref[...]
    g = jnp.dot(x, wg_ref[...].astype(BF16), preferred_element_type=F32)
    u = jnp.dot(x, wu_ref[...].astype(BF16), preferred_element_type=F32)
    o_ref[...] = (g * _sigmoid(g) * u).astype(o_ref.dtype)


def expert_swiglu(xe, w_gate, w_up, layer, tn=256):
    E, M, K = xe.shape
    N = w_gate.shape[-1]
    wspec = pl.BlockSpec((None, None, K, tn), lambda e, j: (layer, e, 0, j))
    return pl.pallas_call(
        _swiglu_kernel,
        grid=(E, N // tn),
        in_specs=[pl.BlockSpec((None, M, K), lambda e, j: (e, 0, 0)), wspec, wspec],
        out_specs=pl.BlockSpec((None, M, tn), lambda e, j: (e, 0, j)),
        out_shape=jax.ShapeDtypeStruct((E, M, N), BF16),
        compiler_params=_cparams("parallel", "parallel"),
    )(xe, w_gate, w_up)


def _down_kernel(h_ref, w_ref, g_ref, o_ref):
    y = jnp.dot(h_ref[...], w_ref[...].astype(BF16), preferred_element_type=F32)
    o_ref[...] = y * g_ref[...]


def expert_down(hid, w_down, gains, layer, tn=1024):
    E, M, K = hid.shape
    N = w_down.shape[-1]
    return pl.pallas_call(
        _down_kernel,
        grid=(E, N // tn),
        in_specs=[pl.BlockSpec((None, M, K), lambda e, j: (e, 0, 0)),
                  pl.BlockSpec((None, None, K, tn), lambda e, j: (layer, e, 0, j)),
                  pl.BlockSpec((None, M, 1), lambda e, j: (e, 0, 0))],
        out_specs=pl.BlockSpec((None, M, tn), lambda e, j: (e, 0, j)),
        out_shape=jax.ShapeDtypeStruct((E, M, N), F32),
        compiler_params=_cparams("parallel", "parallel"),
    )(hid, w_down, gains)


def _norm_stats(x):
    xc = x - jnp.mean(x, axis=-1, keepdims=True)
    return xc * lax.rsqrt(jnp.mean(xc * xc, axis=-1, keepdims=True) + EPS)


def _rope_tables():
    rows = SEQ // GRID_W
    row = jnp.repeat(jnp.arange(rows), GRID_W).astype(F32)
    col = (jnp.arange(SEQ) % GRID_W).astype(F32)
    n_freq = HEAD_DIM // 4
    inv = ROPE_THETA ** (-jnp.arange(n_freq, dtype=F32) / n_freq)
    ang_r = row[:, None] * inv[None, :]
    ang_c = col[:, None] * inv[None, :]
    cos = jnp.concatenate([jnp.cos(ang_r), jnp.cos(ang_r), jnp.cos(ang_c), jnp.cos(ang_c)], axis=1)
    sin = jnp.concatenate([-jnp.sin(ang_r), jnp.sin(ang_r), -jnp.sin(ang_c), jnp.sin(ang_c)], axis=1)
    cos = jnp.concatenate([cos, jnp.ones((CTX_LEN, HEAD_DIM), F32)], axis=0)
    sin = jnp.concatenate([sin, jnp.zeros((CTX_LEN, HEAD_DIM), F32)], axis=0)
    return cos, sin


def _mlstm_chunk_scan(q, k, v, a_row, b_row, state, reverse):
    H, T, d = q.shape
    nc = T // ML_CHUNK

    def chunks(x):
        return jnp.moveaxis(x.reshape(H, nc, ML_CHUNK, *x.shape[2:]), 1, 0)

    tri = jnp.tril(jnp.ones((ML_CHUNK, ML_CHUNK), dtype=bool))
    mask = tri.T if reverse else tri
    last = 0 if reverse else ML_CHUNK - 1

    def step(carry, inp):
        C, n, m = carry
        qc, kc, vc, ac, bc = inp
        inter = bc + m[..., None]
        dlog = jnp.where(mask, bc[..., :, None] + ac[..., None, :], -jnp.inf)
        mj = jnp.maximum(inter, jnp.max(dlog, axis=-1))
        w_inter = jnp.exp(inter - mj)
        s = jnp.einsum('hjd,hsd->hjs', qc, kc) * jnp.exp(dlog - mj[..., None])
        num = (w_inter[..., None] * jnp.einsum('hjd,hde->hje', qc, C)
               + jnp.einsum('hjs,hse->hje', s, vc))
        den = w_inter * jnp.einsum('hjd,hd->hj', qc, n) + jnp.sum(s, axis=-1)
        h = num / jnp.maximum(jnp.abs(den), jnp.exp(-mj))[..., None]
        m_new = mj[..., last]
        b_end = bc[..., last]
        w_c = jnp.exp(b_end + m - m_new)
        w_s = jnp.exp(b_end[..., None] + ac - m_new[..., None])
        C_new = w_c[..., None, None] * C + jnp.einsum('hs,hsd,hse->hde', w_s, kc, vc)
        n_new = w_c[..., None] * n + jnp.einsum('hs,hsd->hd', w_s, kc)
        return (C_new, n_new, m_new), h

    final, h = lax.scan(step, state, (chunks(q), chunks(k), chunks(v), chunks(a_row), chunks(b_row)),
                        reverse=reverse)
    return jnp.moveaxis(h, 0, 1).reshape(H, T, d), final


def _mlstm(P, G, norm_g, rows_out):
    BW = BRANCH_WIDTH

    def heads(col, rows):
        a = P[rows, col * BW:(col + 1) * BW].astype(F32)
        return jnp.swapaxes(a.reshape(-1, ML_HEADS, ML_HEAD_DIM), 0, 1)

    H = ML_HEADS
    zero = (jnp.zeros((H, ML_HEAD_DIM, ML_HEAD_DIM), F32), jnp.zeros((H, ML_HEAD_DIM), F32), jnp.zeros((H,), F32))
    outs = {}
    st_f, st_b = zero, zero
    for name, rows in (('ctx', slice(SEQ, N_TOK)), ('lat', slice(0, SEQ))):
        q = heads(COL_ML_Q, rows)
        k = heads(COL_ML_K, rows) * (ML_HEAD_DIM ** -0.5)
        v = heads(COL_ML_V, rows)
        h_f, st_f = _mlstm_chunk_scan(q, k, v, G[H:2 * H, rows], G[0:H, rows], st_f, False)
        h_b, st_b = _mlstm_chunk_scan(q, k, v, G[3 * H:4 * H, rows], G[2 * H:3 * H, rows], st_b, True)
        outs[name] = h_f + h_b
    h = jnp.concatenate([outs['lat'], outs['ctx']], axis=1)[:, :rows_out]
    hn = _norm_stats(jnp.swapaxes(h, 0, 1)) * norm_g.reshape(H, ML_HEAD_DIM)
    o_pre = P[:rows_out, COL_ML_O * BW:(COL_ML_O + 1) * BW].astype(F32)
    return (jax.nn.sigmoid(o_pre) * hn.reshape(rows_out, BW)).astype(BF16)


def _expert_choice_route(hm, w_router, cap):
    logits = jnp.dot(hm, w_router.astype(BF16), preferred_element_type=F32)
    aff = jax.nn.softmax(logits, axis=-1)
    return lax.top_k(aff.T, cap)


def kernel(x, c, ctx, c_ctx, w_mod, b_mod, w_in, att_q_gain, att_k_gain, gm_ln_g, gm_ln_b,
           gm_w_s, gm_b_s, conv_w, conv_b, conv_ln_g, conv_ln_b, ml_gate_bias, ml_norm_g,
           w_branch, w_out, ln1_g, ln1_b, w_router, w_gate, w_up, w_down, ln2_g, ln2_b):
    cos_t, sin_t = _rope_tables()
    cc = jnp.zeros((16, D_MODEL), F32).at[0].set(c[0]).at[1].set(c_ctx)
    mods = modulation(cc, w_mod, b_mod)
    x_all = jnp.concatenate([x[0], ctx[0]], axis=0)
    h_all = modulate(x_all, mods[0])
    for l in range(DEPTH):
        last = l == DEPTH - 1
        n_rows = SEQ if last else N_TOK
        tm = (SEQ if last else N_TOK) // 8
        mod_l = mods[l]

        P = branch_projection(h_all, w_in, l)
        wg_t = w_in[l][:, SEG1:MERGE_START].T.astype(BF16)
        gbias = jnp.broadcast_to(ml_gate_bias[l].reshape(GATE_COLS, 1), (GATE_COLS, LANES))
        G = mlstm_gate_rows(h_all, wg_t, gbias)
        q, k = qk_prepare(P, cos_t, sin_t, att_q_gain[l], att_k_gain[l])

        br_a = chunk_gmlp(P, gm_ln_g[l], gm_ln_b[l], gm_w_s[l], gm_b_s[l], n_rows)
        br_b = attention(q, k, P, n_rows)
        br_c = conformer_conv(P, conv_w[l], conv_b[l], conv_ln_g[l], conv_ln_b[l], n_rows)
        br_d = _mlstm(P, G, ml_norm_g[l], n_rows)

        w_merge = w_in[l][:, MERGE_START:].astype(BF16)
        y = merge_branches(h_all, [br_a, br_b, br_c, br_d], w_merge, w_branch[l].astype(BF16), n_rows, tm)
        x1, hm = out_proj_ln(y, w_out[l].astype(BF16), x_all, mod_l, ln1_g[l], ln1_b[l], n_rows, tm // 2)

        cap_l = CAPACITY_FACTOR * SEQ // N_EXPERTS
        g_l, idx_l = _expert_choice_route(hm[:SEQ], w_router[l], cap_l)
        gains, idx = g_l, idx_l
        if not last:
            g_c, idx_c = _expert_choice_route(hm[SEQ:], w_router[l], CAPACITY_FACTOR * CTX_LEN // N_EXPERTS)
            gains = jnp.concatenate([g_l, g_c], axis=1)
            idx = jnp.concatenate([idx_l, idx_c + SEQ], axis=1)
        xe = hm[idx]
        hid = expert_swiglu(xe, w_gate, w_up, l)
        ye = expert_down(hid, w_down, gains[..., None], l)
        y_moe = jnp.zeros((n_rows, D_MODEL), F32).at[idx.reshape(-1)].add(ye.reshape(-1, D_MODEL))
        res = ffn_residual_ln(x1, y_moe, mod_l, ln2_g[l], ln2_b[l], None if last else mods[l + 1], n_rows)
        if not last:
            x_all, h_all = res
        else:
            x_all = res[0]
    return x_all[None]
```

```python
import functools

import jax
import jax.numpy as jnp
from jax import lax
from jax.experimental import pallas as pl
from jax.experimental.pallas import tpu as pltpu

F32 = jnp.float32
BF16 = jnp.bfloat16

D_MODEL = 4096
SEQ = 8192
DEPTH = 2
GRID_W = 64
CTX_LEN = 256
N_BRANCH = 4
BRANCH_WIDTH = D_MODEL // N_BRANCH
HEAD_DIM = 128
GM_CHUNK = 128
GM_GROUPS = BRANCH_WIDTH // HEAD_DIM
ATT_HEADS = BRANCH_WIDTH // HEAD_DIM
ATT_KV_HEADS = 2
ATT_REP = ATT_HEADS // ATT_KV_HEADS
KV_WIDTH = ATT_KV_HEADS * HEAD_DIM
ROPE_THETA = 10000.0
CONV_K = 31
ML_HEADS = BRANCH_WIDTH // HEAD_DIM
ML_HEAD_DIM = HEAD_DIM
ML_CHUNK = 128
ML_N_GATES = 4
N_EXPERTS = 16
EXPERT_FF = D_MODEL // 4
CAPACITY_FACTOR = 2
ALPHA = (2 * DEPTH) ** 0.25
EPS = 1e-6
N_MOD = 6

N_TOK = SEQ + CTX_LEN
SEG1 = 9 * BRANCH_WIDTH + 2 * KV_WIDTH
GATE_COLS = ML_N_GATES * ML_HEADS
MERGE_START = SEG1 + GATE_COLS

V7X_VMEM_LIMIT = 56 * 1024 * 1024
LANES = 128

GATE_TOK_TILE = next(t for t in (1408, 1280, 1024, 768, 512, 384, 256, 128) if N_TOK % t == 0)
ROW_BLK = 256
N_ROW_BLKS = N_TOK // ROW_BLK
LAT_ROW_BLKS = SEQ // ROW_BLK

P_TILE = 512
P_KV_TILE_IN = (3 * BRANCH_WIDTH) // P_TILE
P_N_TILES = SEG1 // P_TILE
COL_GM_U, COL_GM_V, COL_ATT_Q = 0, 1, 2
COL_CV_A, COL_CV_B, COL_ML_Q, COL_ML_K, COL_ML_V, COL_ML_O = 3, 4, 5, 6, 7, 8
P_K_COL = 9 * BRANCH_WIDTH


def _cparams(*sem):
    return pltpu.CompilerParams(dimension_semantics=sem, vmem_limit_bytes=V7X_VMEM_LIMIT)


def _sigmoid(x):
    return 1.0 / (1.0 + jnp.exp(-x))


def _gelu_tanh(x):
    return 0.5 * x * (1.0 + jnp.tanh(0.7978845608028654 * (x + 0.044715 * (x * x * x))))


def _ln_rows(z, g, b):
    mu = jnp.mean(z, axis=-1, keepdims=True)
    zc = z - mu
    var = jnp.mean(zc * zc, axis=-1, keepdims=True)
    return zc * lax.rsqrt(var + EPS) * g + b


def _pick(ref, is_ctx):
    return jnp.where(is_ctx, ref[1:2, :], ref[0:1, :])


def _mod_kernel(c_ref, w_ref, b_ref, o_ref):
    c = c_ref[...]
    a = (c * _sigmoid(c)).astype(BF16)
    o_ref[...] = jnp.dot(a, w_ref[...].astype(BF16), preferred_element_type=F32) + b_ref[...]


def modulation(cc, w_mod, b_mod, tn=1024):
    L, D, N = w_mod.shape
    return pl.pallas_call(
        _mod_kernel,
        grid=(L, N // tn),
        in_specs=[pl.BlockSpec((16, D), lambda l, j: (0, 0)),
                  pl.BlockSpec((None, D, tn), lambda l, j: (l, 0, j)),
                  pl.BlockSpec((None, 1, tn), lambda l, j: (l, 0, j))],
        out_specs=pl.BlockSpec((None, 16, tn), lambda l, j: (l, 0, j)),
        out_shape=jax.ShapeDtypeStruct((L, 16, N), F32),
        compiler_params=_cparams("parallel", "parallel"),
    )(cc, w_mod, b_mod.reshape(L, 1, N))


def _modulate_kernel(x_ref, sh_ref, sc_ref, o_ref):
    is_ctx = pl.program_id(0) >= LAT_ROW_BLKS
    o_ref[...] = (x_ref[...] * (1.0 + _pick(sc_ref, is_ctx)) + _pick(sh_ref, is_ctx)).astype(o_ref.dtype)


def modulate(x_all, mod_l):
    D = D_MODEL
    return pl.pallas_call(
        _modulate_kernel,
        grid=(N_ROW_BLKS,),
        in_specs=[pl.BlockSpec((ROW_BLK, D), lambda i: (i, 0)),
                  pl.BlockSpec((16, D), lambda i: (0, 0)),
                  pl.BlockSpec((16, D), lambda i: (0, 1))],
        out_specs=pl.BlockSpec((ROW_BLK, D), lambda i: (i, 0)),
        out_shape=jax.ShapeDtypeStruct((N_TOK, D), BF16),
        compiler_params=_cparams("parallel"),
    )(x_all, mod_l, mod_l)


def _proj_kernel(a_ref, w_ref, o_ref, wb_ref):
    @pl.when(pl.program_id(1) == 0)
    def _():
        wb_ref[...] = w_ref[...].astype(BF16)

    o_ref[...] = jnp.dot(a_ref[...], wb_ref[...], preferred_element_type=F32).astype(o_ref.dtype)


def _p_out_tile(j):
    return jnp.where(j < P_KV_TILE_IN, j, jnp.where(j == P_KV_TILE_IN, P_N_TILES - 1, j - 1))


def branch_projection(h_all, w_in, layer, tm=N_TOK // 8):
    M, K = h_all.shape
    return pl.pallas_call(
        _proj_kernel,
        grid=(P_N_TILES, M // tm),
        in_specs=[pl.BlockSpec((tm, K), lambda j, i: (i, 0)),
                  pl.BlockSpec((None, K, P_TILE), lambda j, i: (layer, 0, j))],
        out_specs=pl.BlockSpec((tm, P_TILE), lambda j, i: (i, _p_out_tile(j))),
        out_shape=jax.ShapeDtypeStruct((M, SEG1), BF16),
        scratch_shapes=[pltpu.VMEM((K, P_TILE), BF16)],
        compiler_params=_cparams("parallel", "arbitrary"),
        name="branch_projection",
    )(h_all, w_in)


def _gates_kernel(w_ref, h_ref, bias_ref, o_ref, *, n_chunks):
    z = lax.dot_general(w_ref[...], h_ref[...], (((1,), (1,)), ((), ())), preferred_element_type=F32)
    z = z + bias_ref[:, 0:1]
    lf = jnp.minimum(z, 0.0) - jnp.log1p(jnp.exp(-jnp.abs(z)))
    row = lax.broadcasted_iota(jnp.int32, (ML_CHUNK, ML_CHUNK), 0)
    col = lax.broadcasted_iota(jnp.int32, (ML_CHUNK, ML_CHUNK), 1)
    upper = (row <= col).astype(F32)
    lower = (row >= col).astype(F32)
    H = ML_HEADS
    for c in range(n_chunks):
        sl = slice(c * ML_CHUNK, (c + 1) * ML_CHUNK)
        b_f = jnp.dot(lf[H:2 * H, sl], upper, preferred_element_type=F32, precision=lax.Precision.HIGHEST)
        b_b = jnp.dot(lf[3 * H:4 * H, sl], lower, preferred_element_type=F32, precision=lax.Precision.HIGHEST)
        o_ref[:, sl] = jnp.concatenate([b_f, z[0:H, sl] - b_f, b_b, z[2 * H:3 * H, sl] - b_b], axis=0)


def mlstm_gate_rows(h_all, w_gates_t, bias, tt=GATE_TOK_TILE):
    M, K = h_all.shape
    kern = functools.partial(_gates_kernel, n_chunks=tt // ML_CHUNK)
    return pl.pallas_call(
        kern,
        grid=(M // tt,),
        in_specs=[pl.BlockSpec((GATE_COLS, K), lambda i: (0, 0)),
                  pl.BlockSpec((tt, K), lambda i: (i, 0)),
                  pl.BlockSpec((GATE_COLS, LANES), lambda i: (0, 0))],
        out_specs=pl.BlockSpec((GATE_COLS, tt), lambda i: (0, i)),
        out_shape=jax.ShapeDtypeStruct((GATE_COLS, M), F32),
        compiler_params=_cparams("parallel"),
    )(w_gates_t, h_all, bias)


def _rope_head(x, cos, sin, first_half):
    partner = jnp.where(first_half, pltpu.roll(x, 96, 1), pltpu.roll(x, 32, 1))
    return x * cos + partner * sin


def _qk_kernel(q_ref, kv_ref, cos_ref, sin_ref, qg_ref, kg_ref, qo_ref, ko_ref):
    cos = cos_ref[...]
    sin = sin_ref[...]
    lane = lax.broadcasted_iota(jnp.int32, (ROW_BLK, HEAD_DIM), 1)
    first_half = jnp.bitwise_and(lane, 63) < 32

    def prep(x, gain, scale):
        x = x.astype(F32)
        y = x * lax.rsqrt(jnp.mean(x * x, axis=-1, keepdims=True) + EPS) * gain
        return (_rope_head(y, cos, sin, first_half) * scale).astype(BF16)

    for h in range(ATT_HEADS):
        sl = slice(h * HEAD_DIM, (h + 1) * HEAD_DIM)
        qo_ref[:, sl] = prep(q_ref[:, sl], qg_ref[...], HEAD_DIM ** -0.5)
    for h in range(ATT_KV_HEADS):
        sl = slice(h * HEAD_DIM, (h + 1) * HEAD_DIM)
        ko_ref[:, sl] = prep(kv_ref[:, sl], kg_ref[...], 1.0)


def qk_prepare(P, cos_t, sin_t, q_gain, k_gain):
    return pl.pallas_call(
        _qk_kernel,
        grid=(N_ROW_BLKS,),
        in_specs=[pl.BlockSpec((ROW_BLK, BRANCH_WIDTH), lambda i: (i, COL_ATT_Q)),
                  pl.BlockSpec((ROW_BLK, P_TILE), lambda i: (i, P_N_TILES - 1)),
                  pl.BlockSpec((ROW_BLK, HEAD_DIM), lambda i: (i, 0)),
                  pl.BlockSpec((ROW_BLK, HEAD_DIM), lambda i: (i, 0)),
                  pl.BlockSpec((1, HEAD_DIM), lambda i: (0, 0)),
                  pl.BlockSpec((1, HEAD_DIM), lambda i: (0, 0))],
        out_specs=[pl.BlockSpec((ROW_BLK, BRANCH_WIDTH), lambda i: (i, 0)),
                   pl.BlockSpec((ROW_BLK, KV_WIDTH), lambda i: (i, 0))],
        out_shape=[jax.ShapeDtypeStruct((N_TOK, BRANCH_WIDTH), BF16),
                   jax.ShapeDtypeStruct((N_TOK, KV_WIDTH), BF16)],
        compiler_params=_cparams("parallel"),
    )(P, P, cos_t, sin_t, q_gain.reshape(1, HEAD_DIM), k_gain.reshape(1, HEAD_DIM))


def _attn_kernel(q_ref, k1_ref, v1_ref, k2_ref, v2_ref, o_ref, *, tq, ck, n_chunks):
    rows = ATT_REP * tq
    q = jnp.concatenate([q_ref[:, h * HEAD_DIM:(h + 1) * HEAD_DIM] for h in range(ATT_REP)], axis=0)

    def step(k, v, carry):
        m, l, acc = carry
        s = lax.dot_general(q, k, (((1,), (1,)), ((), ())), preferred_element_type=F32)
        m_new = jnp.maximum(m, jnp.max(s, axis=-1, keepdims=True))
        p = jnp.exp(s - m_new)
        a = jnp.exp(m - m_new)
        l = a * l + jnp.sum(p, axis=-1, keepdims=True)
        acc = a * acc + jnp.dot(p.astype(BF16), v, preferred_element_type=F32)
        return m_new, l, acc

    def body(c, carry):
        off = pl.multiple_of(c * ck, ck)
        return step(k1_ref[pl.ds(off, ck), :], v1_ref[pl.ds(off, ck), :], carry)

    carry = (jnp.full((rows, 1), -jnp.inf, F32), jnp.zeros((rows, 1), F32),
             jnp.zeros((rows, HEAD_DIM), F32))
    if n_chunks:
        carry = lax.fori_loop(0, n_chunks, body, carry, unroll=2)
    _, l, acc = step(k2_ref[...], v2_ref[...], carry)
    o = acc / l
    for h in range(ATT_REP):
        o_ref[:, h * HEAD_DIM:(h + 1) * HEAD_DIM] = o[h * tq:(h + 1) * tq].astype(o_ref.dtype)


def attention(q, k, P, latent, tq=128, ck=1024):
    gw = ATT_REP * HEAD_DIM
    v_col = (P_K_COL + KV_WIDTH) // HEAD_DIM
    ctx_blk = SEQ // CTX_LEN
    n_rows = SEQ if latent else CTX_LEN
    q_blk0 = 0 if latent else SEQ // tq
    kern = functools.partial(_attn_kernel, tq=tq, ck=ck, n_chunks=SEQ // ck if latent else 0)
    return pl.pallas_call(
        kern,
        grid=(ATT_KV_HEADS, n_rows // tq),
        in_specs=[pl.BlockSpec((tq, gw), lambda g, i: (q_blk0 + i, g)),
                  pl.BlockSpec((SEQ, HEAD_DIM), lambda g, i: (0, g)),
                  pl.BlockSpec((SEQ, HEAD_DIM), lambda g, i: (0, v_col + g)),
                  pl.BlockSpec((CTX_LEN, HEAD_DIM), lambda g, i: (ctx_blk, g)),
                  pl.BlockSpec((CTX_LEN, HEAD_DIM), lambda g, i: (ctx_blk, v_col + g))],
        out_specs=pl.BlockSpec((tq, gw), lambda g, i: (i, g)),
        out_shape=jax.ShapeDtypeStruct((n_rows, ATT_HEADS * HEAD_DIM), BF16),
        compiler_params=_cparams("parallel", "parallel"),
        name="gqa_attention",
    )(q, k, P, k, P)


def _gmlp_kernel(u_ref, v_ref, g_ref, b_ref, ws_ref, bs_ref, o_ref):
    u = _gelu_tanh(u_ref[...].astype(F32))
    v = _ln_rows(_gelu_tanh(v_ref[...].astype(F32)), g_ref[...], b_ref[...]).astype(BF16)
    for c in range(ROW_BLK // GM_CHUNK):
        rs = slice(c * GM_CHUNK, (c + 1) * GM_CHUNK)
        for g in range(GM_GROUPS):
            cs = slice(g * HEAD_DIM, (g + 1) * HEAD_DIM)
            s = jnp.dot(ws_ref[g], v[rs, cs], preferred_element_type=F32) + bs_ref[g]
            o_ref[rs, cs] = (u[rs, cs] * s).astype(o_ref.dtype)


def chunk_gmlp(P, ln_g, ln_b, w_s, b_s, n_rows):
    bs_rep = jnp.broadcast_to(b_s[:, :, None], (GM_GROUPS, GM_CHUNK, HEAD_DIM))
    BW = BRANCH_WIDTH
    return pl.pallas_call(
        _gmlp_kernel,
        grid=(n_rows // ROW_BLK,),
        in_specs=[pl.BlockSpec((ROW_BLK, BW), lambda i: (i, COL_GM_U)),
                  pl.BlockSpec((ROW_BLK, BW), lambda i: (i, COL_GM_V)),
                  pl.BlockSpec((1, BW), lambda i: (0, 0)),
                  pl.BlockSpec((1, BW), lambda i: (0, 0)),
                  pl.BlockSpec((GM_GROUPS, GM_CHUNK, GM_CHUNK), lambda i: (0, 0, 0)),
                  pl.BlockSpec((GM_GROUPS, GM_CHUNK, HEAD_DIM), lambda i: (0, 0, 0))],
        out_specs=pl.BlockSpec((ROW_BLK, BW), lambda i: (i, 0)),
        out_shape=jax.ShapeDtypeStruct((n_rows, BW), BF16),
        compiler_params=_cparams("parallel"),
    )(P, P, ln_g.reshape(1, BW), ln_b.reshape(1, BW), w_s.astype(BF16), bs_rep)


CONV_HALO = 16
CONV_SUB = 32


def _conv_kernel(a_ref, b_ref, ap_ref, bp_ref, an_ref, bn_ref, w_ref, cb_ref, g_ref, beta_ref, o_ref, gbuf):
    i = pl.program_id(0)

    def glu(a, b):
        return a.astype(F32) * _sigmoid(b.astype(F32))

    prev_ok = jnp.logical_and(i != 0, i != LAT_ROW_BLKS)
    next_ok = jnp.logical_and(i != LAT_ROW_BLKS - 1, i != N_ROW_BLKS - 1)
    gbuf[0:CONV_HALO, :] = jnp.where(prev_ok, glu(ap_ref[...], bp_ref[...]), 0.0)
    gbuf[CONV_HALO:CONV_HALO + ROW_BLK, :] = glu(a_ref[...], b_ref[...])
    gbuf[CONV_HALO + ROW_BLK:, :] = jnp.where(next_ok, glu(an_ref[...], bn_ref[...]), 0.0)
    first = CONV_HALO - CONV_K // 2
    for r in range(ROW_BLK // CONV_SUB):
        base = r * CONV_SUB
        acc = jnp.zeros((CONV_SUB, BRANCH_WIDTH), F32)
        for k in range(CONV_K):
            acc = acc + gbuf[base + first + k:base + first + k + CONV_SUB, :] * w_ref[k:k + 1, :]
        y = _ln_rows(acc + cb_ref[...], g_ref[...], beta_ref[...])
        o_ref[base:base + CONV_SUB, :] = (y * _sigmoid(y)).astype(o_ref.dtype)


def conformer_conv(P, conv_w, conv_b, ln_g, ln_b, n_rows):
    BW = BRANCH_WIDTH
    hpb = ROW_BLK // CONV_HALO
    last_halo = N_TOK // CONV_HALO - 1

    def prev_map(col):
        return lambda i: (jnp.maximum(i * hpb - 1, 0), col)

    def next_map(col):
        return lambda i: (jnp.minimum((i + 1) * hpb, last_halo), col)

    vec = pl.BlockSpec((1, BW), lambda i: (0, 0))
    return pl.pallas_call(
        _conv_kernel,
        grid=(n_rows // ROW_BLK,),
        in_specs=[pl.BlockSpec((ROW_BLK, BW), lambda i: (i, COL_CV_A)),
                  pl.BlockSpec((ROW_BLK, BW), lambda i: (i, COL_CV_B)),
                  pl.BlockSpec((CONV_HALO, BW), prev_map(COL_CV_A)),
                  pl.BlockSpec((CONV_HALO, BW), prev_map(COL_CV_B)),
                  pl.BlockSpec((CONV_HALO, BW), next_map(COL_CV_A)),
                  pl.BlockSpec((CONV_HALO, BW), next_map(COL_CV_B)),
                  pl.BlockSpec((CONV_K + 1, BW), lambda i: (0, 0)),
                  vec, vec, vec],
        out_specs=pl.BlockSpec((ROW_BLK, BW), lambda i: (i, 0)),
        out_shape=jax.ShapeDtypeStruct((n_rows, BW), BF16),
        scratch_shapes=[pltpu.VMEM((ROW_BLK + 2 * CONV_HALO, BW), F32)],
        compiler_params=_cparams("parallel"),
    )(P, P, P, P, P, P, jnp.pad(conv_w, ((0, 1), (0, 0))), conv_b.reshape(1, BW),
      ln_g.reshape(1, BW), ln_b.reshape(1, BW))


def _merge_kernel(h_ref, b0, b1, b2, b3, m0, m1, m2, m3, w0, w1, w2, w3, o_ref):
    h = h_ref[...]
    acc = None
    for br, wm, wb in ((b0, m0, w0), (b1, m1, w1), (b2, m2, w2), (b3, m3, w3)):
        gate = _sigmoid(jnp.dot(h, wm[...], preferred_element_type=F32))
        t = gate * jnp.dot(br[...], wb[...], preferred_element_type=F32)
        acc = t if acc is None else acc + t
    o_ref[...] = acc.astype(o_ref.dtype)


def merge_branches(h_all, branches, w_merge, w_branch, n_rows, tm, tn=256):
    D = D_MODEL
    BW = BRANCH_WIDTH
    nj = D // tn
    one = pl.Buffered(1)
    in_specs = [pl.BlockSpec((tm, D), lambda i, j: (i, 0), pipeline_mode=one)]
    in_specs += [pl.BlockSpec((tm, BW), lambda i, j: (i, 0), pipeline_mode=one) for _ in range(N_BRANCH)]
    in_specs += [pl.BlockSpec((D, tn), functools.partial(lambda i, j, b: (0, b * nj + j), b=b))
                 for b in range(N_BRANCH)]
    in_specs += [pl.BlockSpec((None, BW, tn), functools.partial(lambda i, j, b: (b, 0, j), b=b))
                 for b in range(N_BRANCH)]
    return pl.pallas_call(
        _merge_kernel,
        grid=(n_rows // tm, nj),
        in_specs=in_specs,
        out_specs=pl.BlockSpec((tm, tn), lambda i, j: (i, j)),
        out_shape=jax.ShapeDtypeStruct((n_rows, D), BF16),
        compiler_params=_cparams("parallel", "arbitrary"),
        name="gated_merge",
    )(h_all, *branches, *([w_merge] * N_BRANCH), *([w_branch] * N_BRANCH))


def _out_ln_kernel(y_ref, w_ref, x_ref, g1_ref, lg_ref, lb_ref, sh_ref, sc_ref, x1_ref, hm_ref, *, nk, tm):
    k = pl.program_id(1)
    part = jnp.dot(y_ref[...], w_ref[...], preferred_element_type=F32)

    @pl.when(k == 0)
    def _():
        x1_ref[...] = part

    @pl.when(k > 0)
    def _():
        x1_ref[...] += part

    @pl.when(k == nk - 1)
    def _():
        sub = next(s for s in (64, 48, 32, 16) if tm % s == 0)

        def chunk(r, carry):
            r0 = pl.multiple_of(r * sub, 16)
            rows = pl.ds(r0, sub)
            is_ctx = pl.program_id(0) * tm + r0 + lax.broadcasted_iota(jnp.int32, (sub, 1), 0) >= SEQ
            z = ALPHA * x_ref[rows, :] + _pick(g1_ref, is_ctx) * x1_ref[rows, :]
            x1 = _ln_rows(z, lg_ref[...], lb_ref[...])
            x1_ref[rows, :] = x1
            hm_ref[rows, :] = (x1 * (1.0 + _pick(sc_ref, is_ctx)) + _pick(sh_ref, is_ctx)).astype(hm_ref.dtype)
            return carry

        lax.fori_loop(0, tm // sub, chunk, 0)


def out_proj_ln(y, w_out, x_all, mod_l, ln_g, ln_b, n_rows, tm, tk=512):
    D = D_MODEL
    nk = D // tk
    kern = functools.partial(_out_ln_kernel, nk=nk, tm=tm)
    vec = pl.BlockSpec((1, D), lambda i, k: (0, 0))

    def modspec(col):
        return pl.BlockSpec((16, D), lambda i, k: (0, col))

    return pl.pallas_call(
        kern,
        grid=(n_rows // tm, nk),
        in_specs=[pl.BlockSpec((tm, tk), lambda i, k: (i, k)),
                  pl.BlockSpec((tk, D), lambda i, k: (k, 0)),
                  pl.BlockSpec((tm, D), lambda i, k: (i, 0), pipeline_mode=pl.Buffered(1)),
                  modspec(2), vec, vec, modspec(3), modspec(4)],
        out_specs=[pl.BlockSpec((tm, D), lambda i, k: (i, 0)),
                   pl.BlockSpec((tm, D), lambda i, k: (i, 0))],
        out_shape=[jax.ShapeDtypeStruct((n_rows, D), F32),
                   jax.ShapeDtypeStruct((n_rows, D), BF16)],
        compiler_params=_cparams("parallel", "arbitrary"),
        name="out_proj_ln1",
    )(y, w_out, x_all, mod_l, ln_g.reshape(1, D), ln_b.reshape(1, D), mod_l, mod_l)


def _ln2_kernel(x_ref, y_ref, g2_ref, lg_ref, lb_ref, sh_ref, sc_ref, x2_ref, *h_ref, blk_ctx_from):
    is_ctx = pl.program_id(0) >= blk_ctx_from
    z = ALPHA * x_ref[...] + _pick(g2_ref, is_ctx) * y_ref[...]
    x2 = _ln_rows(z, lg_ref[...], lb_ref[...])
    x2_ref[...] = x2
    if h_ref:
        h_ref[0][...] = (x2 * (1.0 + _pick(sc_ref, is_ctx)) + _pick(sh_ref, is_ctx)).astype(BF16)


def ffn_residual_ln(x1, y_moe, mod_l, ln_g, ln_b, mod_next, n_rows):
    D = D_MODEL
    want_h = mod_next is not None
    nxt = mod_next if want_h else mod_l
    kern = functools.partial(_ln2_kernel, blk_ctx_from=LAT_ROW_BLKS)
    row = pl.BlockSpec((ROW_BLK, D), lambda i: (i, 0))
    vec = pl.BlockSpec((1, D), lambda i: (0, 0))
    out_specs = [row, row] if want_h else [row]
    out_shape = [jax.ShapeDtypeStruct((n_rows, D), F32)]
    if want_h:
        out_shape.append(jax.ShapeDtypeStruct((n_rows, D), BF16))
    return pl.pallas_call(
        kern,
        grid=(n_rows // ROW_BLK,),
        in_specs=[row, row, pl.BlockSpec((16, D), lambda i: (0, 5)), vec, vec,
                  pl.BlockSpec((16, D), lambda i: (0, 0)), pl.BlockSpec((16, D), lambda i: (0, 1))],
        out_specs=out_specs,
        out_shape=out_shape,
        compiler_params=_cparams("parallel"),
    )(x1, y_moe, mod_l, ln_g.reshape(1, D), ln_b.reshape(1, D), nxt, nxt)


def _swiglu_kernel(x_ref, wg_ref, wu_ref, o_ref):
    x = x_ref[...]
    g = jnp.dot(x, wg_ref[...].astype(BF16), preferred_element_type=F32)
    u = jnp.dot(x, wu_ref[...].astype(BF16), preferred_element_type=F32)
    o_ref[...] = (g * _sigmoid(g) * u).astype(o_ref.dtype)


def expert_swiglu(xe, w_gate, w_up, layer, tn=256):
    E, M, K = xe.shape
    N = w_gate.shape[-1]
    wspec = pl.BlockSpec((None, None, K, tn), lambda e, j: (layer, e, 0, j))
    return pl.pallas_call(
        _swiglu_kernel,
        grid=(E, N // tn),
        in_specs=[pl.BlockSpec((None, M, K), lambda e, j: (e, 0, 0)), wspec, wspec],
        out_specs=pl.BlockSpec((None, M, tn), lambda e, j: (e, 0, j)),
        out_shape=jax.ShapeDtypeStruct((E, M, N), BF16),
        compiler_params=_cparams("parallel", "parallel"),
    )(xe, w_gate, w_up)


def _down_kernel(h_ref, w_ref, g_ref, o_ref):
    y = jnp.dot(h_ref[...], w_ref[...].astype(BF16), preferred_element_type=F32)
    o_ref[...] = y * g_ref[...]


def expert_down(hid, w_down, gains, layer, tn=1024):
    E, M, K = hid.shape
    N = w_down.shape[-1]
    return pl.pallas_call(
        _down_kernel,
        grid=(E, N // tn),
        in_specs=[pl.BlockSpec((None, M, K), lambda e, j: (e, 0, 0)),
                  pl.BlockSpec((None, None, K, tn), lambda e, j: (layer, e, 0, j)),
                  pl.BlockSpec((None, M, 1), lambda e, j: (e, 0, 0))],
        out_specs=pl.BlockSpec((None, M, tn), lambda e, j: (e, 0, j)),
        out_shape=jax.ShapeDtypeStruct((E, M, N), F32),
        compiler_params=_cparams("parallel", "parallel"),
    )(hid, w_down, gains)


N_CHUNKS = N_TOK // ML_CHUNK
LAT_CHUNKS = SEQ // ML_CHUNK
CTX_CHUNKS = CTX_LEN // ML_CHUNK


def _mlstm_kernel(q_ref, k_ref, v_ref, og_ref, g_ref, ng_ref, out_ref, acc_ref, c_ref, n_ref, m_ref, *, hb, out_chunks):
    L, d = ML_CHUNK, ML_HEAD_DIM
    row = lax.broadcasted_iota(jnp.int32, (L, L), 0)
    col = lax.broadcasted_iota(jnp.int32, (L, L), 1)
    eye = row == col
    masks = (col <= row, col >= row)
    lasts = (L - 1, 0)
    k_scale = ML_HEAD_DIM ** -0.5

    acc_ref[...] = jnp.zeros(acc_ref.shape, F32)
    c_ref[...] = jnp.zeros(c_ref.shape, F32)
    n_ref[...] = jnp.zeros(n_ref.shape, F32)
    m_ref[...] = jnp.zeros(m_ref.shape, F32)

    def col_of(r):
        return jnp.sum(jnp.where(eye, r, 0.0), axis=1, keepdims=True)

    def step(t, carry):
        chunk = (jnp.where(t < CTX_CHUNKS, t + LAT_CHUNKS, t - CTX_CHUNKS), N_CHUNKS - 1 - t)
        for j in range(hb):
            hs = slice(j * d, (j + 1) * d)
            for dr in range(2):
                si = 2 * j + dr
                off = pl.multiple_of(chunk[dr] * L, L)
                rows = pl.ds(off, L)
                q = q_ref[rows, hs]
                kf = k_ref[rows, hs].astype(F32) * k_scale
                v = v_ref[rows, hs]
                b_row = g_ref[j, 2 * dr:2 * dr + 1, rows]
                a_row = g_ref[j, 2 * dr + 1:2 * dr + 2, rows]
                last = lasts[dr]
                m = m_ref[si][0:1, 0:1]
                n = n_ref[si][0:1, :]
                C = c_ref[si]
                b_col = col_of(b_row)
                a_col = col_of(a_row)
                dlog = jnp.where(masks[dr], b_col + a_row, -jnp.inf)
                inter = b_col + m
                mj = jnp.maximum(inter, jnp.max(dlog, axis=1, keepdims=True))
                w_inter = jnp.exp(inter - mj)
                qk = lax.dot_general(q, kf.astype(BF16), (((1,), (1,)), ((), ())), preferred_element_type=F32)
                s = qk * jnp.exp(dlog - mj)
                num = (w_inter * jnp.dot(q, C.astype(BF16), preferred_element_type=F32)
                       + jnp.dot(s.astype(BF16), v, preferred_element_type=F32))
                qn = jnp.sum(q.astype(F32) * n, axis=1, keepdims=True)
                den = w_inter * qn + jnp.sum(s, axis=1, keepdims=True)
                h = num / jnp.maximum(jnp.abs(den), jnp.exp(-mj))
                acc_ref[rows, hs] += h
                m_new = mj[last:last + 1, :]
                b_end = b_row[:, last:last + 1]
                w_c = jnp.exp(b_end + m - m_new)
                kw = kf * jnp.exp(b_end + a_col - m_new)
                c_ref[si] = w_c * C + lax.dot_general(kw.astype(BF16), v, (((0,), (0,)), ((), ())),
                                                      preferred_element_type=F32)
                n_ref[si] = jnp.broadcast_to(w_c * n + jnp.sum(kw, axis=0, keepdims=True), (8, d))
                m_ref[si] = jnp.broadcast_to(m_new, (8, LANES))
        return carry

    lax.fori_loop(0, N_CHUNKS, step, 0)

    def finish(c, carry):
        rows = pl.ds(pl.multiple_of(c * L, L), L)
        for j in range(hb):
            hs = slice(j * d, (j + 1) * d)
            hh = acc_ref[rows, hs]
            hc = hh - jnp.mean(hh, axis=1, keepdims=True)
            hn = hc * lax.rsqrt(jnp.mean(hc * hc, axis=1, keepdims=True) + EPS) * ng_ref[j]
            out_ref[rows, hs] = (_sigmoid(og_ref[rows, hs].astype(F32)) * hn).astype(out_ref.dtype)
        return carry

    lax.fori_loop(0, out_chunks, finish, 0)


def mlstm(P, G, norm_g, n_rows, hb=2):
    w = hb * ML_HEAD_DIM
    per = BRANCH_WIDTH // w

    def colspec(col):
        return pl.BlockSpec((N_TOK, w), lambda i: (0, col * per + i))

    kern = functools.partial(_mlstm_kernel, hb=hb, out_chunks=n_rows // ML_CHUNK)
    return pl.pallas_call(
        kern,
        grid=(ML_HEADS // hb,),
        in_specs=[colspec(COL_ML_Q), colspec(COL_ML_K), colspec(COL_ML_V),
                  pl.BlockSpec((n_rows, w), lambda i: (0, COL_ML_O * per + i)),
                  pl.BlockSpec((hb, 4, N_TOK), lambda i: (i, 0, 0)),
                  pl.BlockSpec((hb, 1, ML_HEAD_DIM), lambda i: (i, 0, 0))],
        out_specs=pl.BlockSpec((n_rows, w), lambda i: (0, i)),
        out_shape=jax.ShapeDtypeStruct((n_rows, BRANCH_WIDTH), BF16),
        scratch_shapes=[pltpu.VMEM((N_TOK, w), F32),
                        pltpu.VMEM((2 * hb, ML_HEAD_DIM, ML_HEAD_DIM), F32),
                        pltpu.VMEM((2 * hb, 8, ML_HEAD_DIM), F32),
                        pltpu.VMEM((2 * hb, 8, LANES), F32)],
        compiler_params=_cparams("parallel"),
        name="mlstm_scan",
    )(P, P, P, P, G, norm_g.reshape(ML_HEADS, 1, ML_HEAD_DIM))


def _norm_stats(x):
    xc = x - jnp.mean(x, axis=-1, keepdims=True)
    return xc * lax.rsqrt(jnp.mean(xc * xc, axis=-1, keepdims=True) + EPS)


def _rope_tables():
    rows = SEQ // GRID_W
    row = jnp.repeat(jnp.arange(rows), GRID_W).astype(F32)
    col = (jnp.arange(SEQ) % GRID_W).astype(F32)
    n_freq = HEAD_DIM // 4
    inv = ROPE_THETA ** (-jnp.arange(n_freq, dtype=F32) / n_freq)
    ang_r = row[:, None] * inv[None, :]
    ang_c = col[:, None] * inv[None, :]
    cos = jnp.concatenate([jnp.cos(ang_r), jnp.cos(ang_r), jnp.cos(ang_c), jnp.cos(ang_c)], axis=1)
    sin = jnp.concatenate([-jnp.sin(ang_r), jnp.sin(ang_r), -jnp.sin(ang_c), jnp.sin(ang_c)], axis=1)
    cos = jnp.concatenate([cos, jnp.ones((CTX_LEN, HEAD_DIM), F32)], axis=0)
    sin = jnp.concatenate([sin, jnp.zeros((CTX_LEN, HEAD_DIM), F32)], axis=0)
    return cos, sin


def _mlstm_chunk_scan(q, k, v, a_row, b_row, state, reverse):
    H, T, d = q.shape
    nc = T // ML_CHUNK

    def chunks(x):
        return jnp.moveaxis(x.reshape(H, nc, ML_CHUNK, *x.shape[2:]), 1, 0)

    tri = jnp.tril(jnp.ones((ML_CHUNK, ML_CHUNK), dtype=bool))
    mask = tri.T if reverse else tri
    last = 0 if reverse else ML_CHUNK - 1

    def step(carry, inp):
        C, n, m = carry
        qc, kc, vc, ac, bc = inp
        inter = bc + m[..., None]
        dlog = jnp.where(mask, bc[..., :, None] + ac[..., None, :], -jnp.inf)
        mj = jnp.maximum(inter, jnp.max(dlog, axis=-1))
        w_inter = jnp.exp(inter - mj)
        s = jnp.einsum('hjd,hsd->hjs', qc, kc) * jnp.exp(dlog - mj[..., None])
        num = (w_inter[..., None] * jnp.einsum('hjd,hde->hje', qc, C)
               + jnp.einsum('hjs,hse->hje', s, vc))
        den = w_inter * jnp.einsum('hjd,hd->hj', qc, n) + jnp.sum(s, axis=-1)
        h = num / jnp.maximum(jnp.abs(den), jnp.exp(-mj))[..., None]
        m_new = mj[..., last]
        b_end = bc[..., last]
        w_c = jnp.exp(b_end + m - m_new)
        w_s = jnp.exp(b_end[..., None] + ac - m_new[..., None])
        C_new = w_c[..., None, None] * C + jnp.einsum('hs,hsd,hse->hde', w_s, kc, vc)
        n_new = w_c[..., None] * n + jnp.einsum('hs,hsd->hd', w_s, kc)
        return (C_new, n_new, m_new), h

    final, h = lax.scan(step, state, (chunks(q), chunks(k), chunks(v), chunks(a_row), chunks(b_row)),
                        reverse=reverse)
    return jnp.moveaxis(h, 0, 1).reshape(H, T, d), final


def _mlstm(P, G, norm_g, rows_out):
    BW = BRANCH_WIDTH

    def heads(col, rows):
        a = P[rows, col * BW:(col + 1) * BW].astype(F32)
        return jnp.swapaxes(a.reshape(-1, ML_HEADS, ML_HEAD_DIM), 0, 1)

    H = ML_HEADS
    zero = (jnp.zeros((H, ML_HEAD_DIM, ML_HEAD_DIM), F32), jnp.zeros((H, ML_HEAD_DIM), F32), jnp.zeros((H,), F32))
    outs = {}
    st_f, st_b = zero, zero
    for name, rows in (('ctx', slice(SEQ, N_TOK)), ('lat', slice(0, SEQ))):
        q = heads(COL_ML_Q, rows)
        k = heads(COL_ML_K, rows) * (ML_HEAD_DIM ** -0.5)
        v = heads(COL_ML_V, rows)
        h_f, st_f = _mlstm_chunk_scan(q, k, v, G[H:2 * H, rows], G[0:H, rows], st_f, False)
        h_b, st_b = _mlstm_chunk_scan(q, k, v, G[3 * H:4 * H, rows], G[2 * H:3 * H, rows], st_b, True)
        outs[name] = h_f + h_b
    h = jnp.concatenate([outs['lat'], outs['ctx']], axis=1)[:, :rows_out]
    hn = _norm_stats(jnp.swapaxes(h, 0, 1)) * norm_g.reshape(H, ML_HEAD_DIM)
    o_pre = P[:rows_out, COL_ML_O * BW:(COL_ML_O + 1) * BW].astype(F32)
    return (jax.nn.sigmoid(o_pre) * hn.reshape(rows_out, BW)).astype(BF16)


def _expert_choice_route(hm, w_router, cap):
    logits = jnp.dot(hm, w_router.astype(BF16), preferred_element_type=F32)
    aff = jax.nn.softmax(logits, axis=-1)
    return lax.top_k(aff.T, cap)


def kernel(x, c, ctx, c_ctx, w_mod, b_mod, w_in, att_q_gain, att_k_gain, gm_ln_g, gm_ln_b,
           gm_w_s, gm_b_s, conv_w, conv_b, conv_ln_g, conv_ln_b, ml_gate_bias, ml_norm_g,
           w_branch, w_out, ln1_g, ln1_b, w_router, w_gate, w_up, w_down, ln2_g, ln2_b):
    cos_t, sin_t = _rope_tables()
    cc = jnp.zeros((16, D_MODEL), F32).at[0].set(c[0]).at[1].set(c_ctx)
    mods = modulation(cc, w_mod, b_mod)
    x_all = jnp.concatenate([x[0], ctx[0]], axis=0)
    h_all = modulate(x_all, mods[0])
    for l in range(DEPTH):
        last = l == DEPTH - 1
        n_rows = SEQ if last else N_TOK
        tm = (SEQ if last else N_TOK) // 8
        mod_l = mods[l]

        P = branch_projection(h_all, w_in, l)
        wg_t = w_in[l][:, SEG1:MERGE_START].T.astype(BF16)
        gbias = jnp.broadcast_to(ml_gate_bias[l].reshape(GATE_COLS, 1), (GATE_COLS, LANES))
        G = mlstm_gate_rows(h_all, wg_t, gbias)
        q, k = qk_prepare(P, cos_t, sin_t, att_q_gain[l], att_k_gain[l])

        br_a = chunk_gmlp(P, gm_ln_g[l], gm_ln_b[l], gm_w_s[l], gm_b_s[l], n_rows)
        br_b = attention(q, k, P, True)
        if not last:
            br_b = jnp.concatenate([br_b, attention(q, k, P, False)], axis=0)
        br_c = conformer_conv(P, conv_w[l], conv_b[l], conv_ln_g[l], conv_ln_b[l], n_rows)
        G_heads = G.reshape(ML_N_GATES, ML_HEADS, N_TOK).transpose(1, 0, 2)
        br_d = mlstm(P, G_heads, ml_norm_g[l], n_rows)

        w_merge = w_in[l][:, MERGE_START:].astype(BF16)
        y = merge_branches(h_all, [br_a, br_b, br_c, br_d], w_merge, w_branch[l].astype(BF16), n_rows, tm)
        x1, hm = out_proj_ln(y, w_out[l].astype(BF16), x_all, mod_l, ln1_g[l], ln1_b[l], n_rows, tm // 2)

        cap_l = CAPACITY_FACTOR * SEQ // N_EXPERTS
        g_l, idx_l = _expert_choice_route(hm[:SEQ], w_router[l], cap_l)
        gains, idx = g_l, idx_l
        if not last:
            g_c, idx_c = _expert_choice_route(hm[SEQ:], w_router[l], CAPACITY_FACTOR * CTX_LEN // N_EXPERTS)
            gains = jnp.concatenate([g_l, g_c], axis=1)
            idx = jnp.concatenate([idx_l, idx_c + SEQ], axis=1)
        xe = hm[idx]
        hid = expert_swiglu(xe, w_gate, w_up, l)
        ye = expert_down(hid, w_down, gains[..., None], l)
        y_moe = jnp.zeros((n_rows, D_MODEL), F32).at[idx.reshape(-1)].add(ye.reshape(-1, D_MODEL))
        res = ffn_residual_ln(x1, y_moe, mod_l, ln2_g[l], ln2_b[l], None if last else mods[l + 1], n_rows)
        if not last:
            x_all, h_all = res
        else:
            x_all = res[0]
    return x_all[None]
```

```python
import functools

import jax
import jax.numpy as jnp
from jax import lax
from jax.experimental import pallas as pl
from jax.experimental.pallas import tpu as pltpu

F32 = jnp.float32
BF16 = jnp.bfloat16

D_MODEL = 4096
SEQ = 8192
DEPTH = 2
GRID_W = 64
CTX_LEN = 256
N_BRANCH = 4
BRANCH_WIDTH = D_MODEL // N_BRANCH
HEAD_DIM = 128
GM_CHUNK = 128
GM_GROUPS = BRANCH_WIDTH // HEAD_DIM
ATT_HEADS = BRANCH_WIDTH // HEAD_DIM
ATT_KV_HEADS = 2
ATT_REP = ATT_HEADS // ATT_KV_HEADS
KV_WIDTH = ATT_KV_HEADS * HEAD_DIM
ROPE_THETA = 10000.0
CONV_K = 31
ML_HEADS = BRANCH_WIDTH // HEAD_DIM
ML_HEAD_DIM = HEAD_DIM
ML_CHUNK = 128
ML_N_GATES = 4
N_EXPERTS = 16
EXPERT_FF = D_MODEL // 4
CAPACITY_FACTOR = 2
ALPHA = (2 * DEPTH) ** 0.25
EPS = 1e-6
N_MOD = 6

N_TOK = SEQ + CTX_LEN
SEG1 = 9 * BRANCH_WIDTH + 2 * KV_WIDTH
GATE_COLS = ML_N_GATES * ML_HEADS
MERGE_START = SEG1 + GATE_COLS

V7X_VMEM_LIMIT = 56 * 1024 * 1024
LANES = 128

GATE_TOK_TILE = next(t for t in (1408, 1280, 1024, 768, 512, 384, 256, 128) if N_TOK % t == 0)
ROW_BLK = 256
N_ROW_BLKS = N_TOK // ROW_BLK
LAT_ROW_BLKS = SEQ // ROW_BLK

P_TILE = 512
P_KV_TILE_IN = (3 * BRANCH_WIDTH) // P_TILE
P_N_TILES = SEG1 // P_TILE
COL_GM_U, COL_GM_V, COL_ATT_Q = 0, 1, 2
COL_CV_A, COL_CV_B, COL_ML_Q, COL_ML_K, COL_ML_V, COL_ML_O = 3, 4, 5, 6, 7, 8
P_K_COL = 9 * BRANCH_WIDTH


def _cparams(*sem):
    return pltpu.CompilerParams(dimension_semantics=sem, vmem_limit_bytes=V7X_VMEM_LIMIT)


def _sigmoid(x):
    return 1.0 / (1.0 + jnp.exp(-x))


def _gelu_tanh(x):
    return 0.5 * x * (1.0 + jnp.tanh(0.7978845608028654 * (x + 0.044715 * (x * x * x))))


def _ln_rows(z, g, b):
    mu = jnp.mean(z, axis=-1, keepdims=True)
    zc = z - mu
    var = jnp.mean(zc * zc, axis=-1, keepdims=True)
    return zc * lax.rsqrt(var + EPS) * g + b


def _pick(ref, is_ctx):
    return jnp.where(is_ctx, ref[1:2, :], ref[0:1, :])


def _mod_kernel(c_ref, w_ref, b_ref, o_ref):
    c = c_ref[...]
    a = (c * _sigmoid(c)).astype(BF16)
    o_ref[...] = jnp.dot(a, w_ref[...].astype(BF16), preferred_element_type=F32) + b_ref[...]


def modulation(cc, w_mod, b_mod, tn=1024):
    L, D, N = w_mod.shape
    return pl.pallas_call(
        _mod_kernel,
        grid=(L, N // tn),
        in_specs=[pl.BlockSpec((16, D), lambda l, j: (0, 0)),
                  pl.BlockSpec((None, D, tn), lambda l, j: (l, 0, j)),
                  pl.BlockSpec((None, 1, tn), lambda l, j: (l, 0, j))],
        out_specs=pl.BlockSpec((None, 16, tn), lambda l, j: (l, 0, j)),
        out_shape=jax.ShapeDtypeStruct((L, 16, N), F32),
        compiler_params=_cparams("parallel", "parallel"),
    )(cc, w_mod, b_mod.reshape(L, 1, N))


def _modulate_kernel(x_ref, sh_ref, sc_ref, o_ref):
    is_ctx = pl.program_id(0) >= LAT_ROW_BLKS
    o_ref[...] = (x_ref[...] * (1.0 + _pick(sc_ref, is_ctx)) + _pick(sh_ref, is_ctx)).astype(o_ref.dtype)


def modulate(x_all, mod_l):
    D = D_MODEL
    return pl.pallas_call(
        _modulate_kernel,
        grid=(N_ROW_BLKS,),
        in_specs=[pl.BlockSpec((ROW_BLK, D), lambda i: (i, 0)),
                  pl.BlockSpec((16, D), lambda i: (0, 0)),
                  pl.BlockSpec((16, D), lambda i: (0, 1))],
        out_specs=pl.BlockSpec((ROW_BLK, D), lambda i: (i, 0)),
        out_shape=jax.ShapeDtypeStruct((N_TOK, D), BF16),
        compiler_params=_cparams("parallel"),
    )(x_all, mod_l, mod_l)


def _proj_kernel(a_ref, w_ref, o_ref, wb_ref):
    @pl.when(pl.program_id(1) == 0)
    def _():
        wb_ref[...] = w_ref[...].astype(BF16)

    o_ref[...] = jnp.dot(a_ref[...], wb_ref[...], preferred_element_type=F32).astype(o_ref.dtype)


def _p_out_tile(j):
    return jnp.where(j < P_KV_TILE_IN, j, jnp.where(j == P_KV_TILE_IN, P_N_TILES - 1, j - 1))


def branch_projection(h_all, w_in, layer, tm=N_TOK // 8):
    M, K = h_all.shape
    return pl.pallas_call(
        _proj_kernel,
        grid=(P_N_TILES, M // tm),
        in_specs=[pl.BlockSpec((tm, K), lambda j, i: (i, 0)),
                  pl.BlockSpec((None, K, P_TILE), lambda j, i: (layer, 0, j))],
        out_specs=pl.BlockSpec((tm, P_TILE), lambda j, i: (i, _p_out_tile(j))),
        out_shape=jax.ShapeDtypeStruct((M, SEG1), BF16),
        scratch_shapes=[pltpu.VMEM((K, P_TILE), BF16)],
        compiler_params=_cparams("parallel", "arbitrary"),
        name="branch_projection",
    )(h_all, w_in)


def _gates_kernel(w_ref, h_ref, bias_ref, o_ref, *, n_chunks):
    w_t = jnp.transpose(w_ref[...])[0:GATE_COLS, :].astype(BF16)
    z = lax.dot_general(w_t, h_ref[...], (((1,), (1,)), ((), ())), preferred_element_type=F32)
    z = z + bias_ref[:, 0:1]
    lf = jnp.minimum(z, 0.0) - jnp.log1p(jnp.exp(-jnp.abs(z)))
    row = lax.broadcasted_iota(jnp.int32, (ML_CHUNK, ML_CHUNK), 0)
    col = lax.broadcasted_iota(jnp.int32, (ML_CHUNK, ML_CHUNK), 1)
    upper = (row <= col).astype(F32)
    lower = (row >= col).astype(F32)
    H = ML_HEADS
    for c in range(n_chunks):
        sl = slice(c * ML_CHUNK, (c + 1) * ML_CHUNK)
        b_f = jnp.dot(lf[H:2 * H, sl], upper, preferred_element_type=F32, precision=lax.Precision.HIGHEST)
        b_b = jnp.dot(lf[3 * H:4 * H, sl], lower, preferred_element_type=F32, precision=lax.Precision.HIGHEST)
        o_ref[:, sl] = jnp.concatenate([b_f, z[0:H, sl] - b_f, b_b, z[2 * H:3 * H, sl] - b_b], axis=0)


def mlstm_gate_rows(h_all, w_in, layer, bias, tt=GATE_TOK_TILE):
    M, K = h_all.shape
    kern = functools.partial(_gates_kernel, n_chunks=tt // ML_CHUNK)
    return pl.pallas_call(
        kern,
        grid=(M // tt,),
        in_specs=[pl.BlockSpec((None, K, LANES), lambda i: (layer, 0, SEG1 // LANES)),
                  pl.BlockSpec((tt, K), lambda i: (i, 0)),
                  pl.BlockSpec((GATE_COLS, LANES), lambda i: (0, 0))],
        out_specs=pl.BlockSpec((GATE_COLS, tt), lambda i: (0, i)),
        out_shape=jax.ShapeDtypeStruct((GATE_COLS, M), F32),
        compiler_params=_cparams("parallel"),
    )(w_in, h_all, bias)


def _rope_head(x, cos, sin, first_half):
    partner = jnp.where(first_half, pltpu.roll(x, 96, 1), pltpu.roll(x, 32, 1))
    return x * cos + partner * sin


def _qk_kernel(q_ref, kv_ref, cos_ref, sin_ref, qg_ref, kg_ref, qo_ref, ko_ref):
    cos = cos_ref[...]
    sin = sin_ref[...]
    lane = lax.broadcasted_iota(jnp.int32, (ROW_BLK, HEAD_DIM), 1)
    first_half = jnp.bitwise_and(lane, 63) < 32

    def prep(x, gain, scale):
        x = x.astype(F32)
        y = x * lax.rsqrt(jnp.mean(x * x, axis=-1, keepdims=True) + EPS) * gain
        return (_rope_head(y, cos, sin, first_half) * scale).astype(BF16)

    for h in range(ATT_HEADS):
        sl = slice(h * HEAD_DIM, (h + 1) * HEAD_DIM)
        qo_ref[:, sl] = prep(q_ref[:, sl], qg_ref[...], HEAD_DIM ** -0.5)
    for h in range(ATT_KV_HEADS):
        sl = slice(h * HEAD_DIM, (h + 1) * HEAD_DIM)
        ko_ref[:, sl] = prep(kv_ref[:, sl], kg_ref[...], 1.0)


def qk_prepare(P, cos_t, sin_t, q_gain, k_gain):
    return pl.pallas_call(
        _qk_kernel,
        grid=(N_ROW_BLKS,),
        in_specs=[pl.BlockSpec((ROW_BLK, BRANCH_WIDTH), lambda i: (i, COL_ATT_Q)),
                  pl.BlockSpec((ROW_BLK, P_TILE), lambda i: (i, P_N_TILES - 1)),
                  pl.BlockSpec((ROW_BLK, HEAD_DIM), lambda i: (i, 0)),
                  pl.BlockSpec((ROW_BLK, HEAD_DIM), lambda i: (i, 0)),
                  pl.BlockSpec((1, HEAD_DIM), lambda i: (0, 0)),
                  pl.BlockSpec((1, HEAD_DIM), lambda i: (0, 0))],
        out_specs=[pl.BlockSpec((ROW_BLK, BRANCH_WIDTH), lambda i: (i, 0)),
                   pl.BlockSpec((ROW_BLK, KV_WIDTH), lambda i: (i, 0))],
        out_shape=[jax.ShapeDtypeStruct((N_TOK, BRANCH_WIDTH), BF16),
                   jax.ShapeDtypeStruct((N_TOK, KV_WIDTH), BF16)],
        compiler_params=_cparams("parallel"),
    )(P, P, cos_t, sin_t, q_gain.reshape(1, HEAD_DIM), k_gain.reshape(1, HEAD_DIM))


def _attn_kernel(q_ref, k1_ref, v1_ref, k2_ref, v2_ref, o_ref, *, tq, ck, n_chunks):
    rows = ATT_REP * tq
    q = jnp.concatenate([q_ref[:, h * HEAD_DIM:(h + 1) * HEAD_DIM] for h in range(ATT_REP)], axis=0)

    def step(k, v, carry):
        m, l, acc = carry
        s = lax.dot_general(q, k, (((1,), (1,)), ((), ())), preferred_element_type=F32)
        m_new = jnp.maximum(m, jnp.max(s, axis=-1, keepdims=True))
        p = jnp.exp(s - m_new)
        a = jnp.exp(m - m_new)
        l = a * l + jnp.sum(p, axis=-1, keepdims=True)
        acc = a * acc + jnp.dot(p.astype(BF16), v, preferred_element_type=F32)
        return m_new, l, acc

    def body(c, carry):
        off = pl.multiple_of(c * ck, ck)
        return step(k1_ref[pl.ds(off, ck), :], v1_ref[pl.ds(off, ck), :], carry)

    carry = (jnp.full((rows, 1), -jnp.inf, F32), jnp.zeros((rows, 1), F32),
             jnp.zeros((rows, HEAD_DIM), F32))
    if n_chunks:
        carry = lax.fori_loop(0, n_chunks, body, carry, unroll=2)
    _, l, acc = step(k2_ref[...], v2_ref[...], carry)
    o = acc / l
    for h in range(ATT_REP):
        o_ref[:, h * HEAD_DIM:(h + 1) * HEAD_DIM] = o[h * tq:(h + 1) * tq].astype(o_ref.dtype)


def attention(q, k, P, latent, tq=128, ck=1024):
    gw = ATT_REP * HEAD_DIM
    v_col = (P_K_COL + KV_WIDTH) // HEAD_DIM
    ctx_blk = SEQ // CTX_LEN
    n_rows = SEQ if latent else CTX_LEN
    q_blk0 = 0 if latent else SEQ // tq
    kern = functools.partial(_attn_kernel, tq=tq, ck=ck, n_chunks=SEQ // ck if latent else 0)
    return pl.pallas_call(
        kern,
        grid=(ATT_KV_HEADS, n_rows // tq),
        in_specs=[pl.BlockSpec((tq, gw), lambda g, i: (q_blk0 + i, g)),
                  pl.BlockSpec((SEQ, HEAD_DIM), lambda g, i: (0, g)),
                  pl.BlockSpec((SEQ, HEAD_DIM), lambda g, i: (0, v_col + g)),
                  pl.BlockSpec((CTX_LEN, HEAD_DIM), lambda g, i: (ctx_blk, g)),
                  pl.BlockSpec((CTX_LEN, HEAD_DIM), lambda g, i: (ctx_blk, v_col + g))],
        out_specs=pl.BlockSpec((tq, gw), lambda g, i: (i, g)),
        out_shape=jax.ShapeDtypeStruct((n_rows, ATT_HEADS * HEAD_DIM), BF16),
        compiler_params=_cparams("parallel", "parallel"),
        name="gqa_attention",
    )(q, k, P, k, P)


def _gmlp_kernel(u_ref, v_ref, g_ref, b_ref, ws_ref, bs_ref, o_ref):
    u = _gelu_tanh(u_ref[...].astype(F32))
    v = _ln_rows(_gelu_tanh(v_ref[...].astype(F32)), g_ref[...], b_ref[...]).astype(BF16)
    for c in range(ROW_BLK // GM_CHUNK):
        rs = slice(c * GM_CHUNK, (c + 1) * GM_CHUNK)
        for g in range(GM_GROUPS):
            cs = slice(g * HEAD_DIM, (g + 1) * HEAD_DIM)
            s = jnp.dot(ws_ref[g], v[rs, cs], preferred_element_type=F32) + bs_ref[g]
            o_ref[rs, cs] = (u[rs, cs] * s).astype(o_ref.dtype)


def chunk_gmlp(P, ln_g, ln_b, w_s, b_s, n_rows):
    bs_rep = jnp.broadcast_to(b_s[:, :, None], (GM_GROUPS, GM_CHUNK, HEAD_DIM))
    BW = BRANCH_WIDTH
    return pl.pallas_call(
        _gmlp_kernel,
        grid=(n_rows // ROW_BLK,),
        in_specs=[pl.BlockSpec((ROW_BLK, BW), lambda i: (i, COL_GM_U)),
                  pl.BlockSpec((ROW_BLK, BW), lambda i: (i, COL_GM_V)),
                  pl.BlockSpec((1, BW), lambda i: (0, 0)),
                  pl.BlockSpec((1, BW), lambda i: (0, 0)),
                  pl.BlockSpec((GM_GROUPS, GM_CHUNK, GM_CHUNK), lambda i: (0, 0, 0)),
                  pl.BlockSpec((GM_GROUPS, GM_CHUNK, HEAD_DIM), lambda i: (0, 0, 0))],
        out_specs=pl.BlockSpec((ROW_BLK, BW), lambda i: (i, 0)),
        out_shape=jax.ShapeDtypeStruct((n_rows, BW), BF16),
        compiler_params=_cparams("parallel"),
    )(P, P, ln_g.reshape(1, BW), ln_b.reshape(1, BW), w_s.astype(BF16), bs_rep)


CONV_HALO = 16
CONV_SUB = 32


def _conv_kernel(a_ref, b_ref, ap_ref, bp_ref, an_ref, bn_ref, w_ref, cb_ref, g_ref, beta_ref, o_ref, gbuf):
    i = pl.program_id(0)

    def glu(a, b):
        return a.astype(F32) * _sigmoid(b.astype(F32))

    prev_ok = jnp.logical_and(i != 0, i != LAT_ROW_BLKS)
    next_ok = jnp.logical_and(i != LAT_ROW_BLKS - 1, i != N_ROW_BLKS - 1)
    gbuf[0:CONV_HALO, :] = jnp.where(prev_ok, glu(ap_ref[...], bp_ref[...]), 0.0)
    gbuf[CONV_HALO:CONV_HALO + ROW_BLK, :] = glu(a_ref[...], b_ref[...])
    gbuf[CONV_HALO + ROW_BLK:, :] = jnp.where(next_ok, glu(an_ref[...], bn_ref[...]), 0.0)
    first = CONV_HALO - CONV_K // 2
    for r in range(ROW_BLK // CONV_SUB):
        base = r * CONV_SUB
        acc = jnp.zeros((CONV_SUB, BRANCH_WIDTH), F32)
        for k in range(CONV_K):
            acc = acc + gbuf[base + first + k:base + first + k + CONV_SUB, :] * w_ref[k:k + 1, :]
        y = _ln_rows(acc + cb_ref[...], g_ref[...], beta_ref[...])
        o_ref[base:base + CONV_SUB, :] = (y * _sigmoid(y)).astype(o_ref.dtype)


def conformer_conv(P, conv_w, conv_b, ln_g, ln_b, n_rows):
    BW = BRANCH_WIDTH
    hpb = ROW_BLK // CONV_HALO
    last_halo = N_TOK // CONV_HALO - 1

    def prev_map(col):
        return lambda i: (jnp.maximum(i * hpb - 1, 0), col)

    def next_map(col):
        return lambda i: (jnp.minimum((i + 1) * hpb, last_halo), col)

    vec = pl.BlockSpec((1, BW), lambda i: (0, 0))
    return pl.pallas_call(
        _conv_kernel,
        grid=(n_rows // ROW_BLK,),
        in_specs=[pl.BlockSpec((ROW_BLK, BW), lambda i: (i, COL_CV_A)),
                  pl.BlockSpec((ROW_BLK, BW), lambda i: (i, COL_CV_B)),
                  pl.BlockSpec((CONV_HALO, BW), prev_map(COL_CV_A)),
                  pl.BlockSpec((CONV_HALO, BW), prev_map(COL_CV_B)),
                  pl.BlockSpec((CONV_HALO, BW), next_map(COL_CV_A)),
                  pl.BlockSpec((CONV_HALO, BW), next_map(COL_CV_B)),
                  pl.BlockSpec((CONV_K + 1, BW), lambda i: (0, 0)),
                  vec, vec, vec],
        out_specs=pl.BlockSpec((ROW_BLK, BW), lambda i: (i, 0)),
        out_shape=jax.ShapeDtypeStruct((n_rows, BW), BF16),
        scratch_shapes=[pltpu.VMEM((ROW_BLK + 2 * CONV_HALO, BW), F32)],
        compiler_params=_cparams("parallel"),
    )(P, P, P, P, P, P, jnp.pad(conv_w, ((0, 1), (0, 0))), conv_b.reshape(1, BW),
      ln_g.reshape(1, BW), ln_b.reshape(1, BW))


MW_TILE = 512
MW_SHIFT = MERGE_START % LANES


def _merge_weights_kernel(a_ref, b_ref, o_ref):
    x = jnp.concatenate([a_ref[...], b_ref[...]], axis=1)
    o_ref[...] = x[:, MW_SHIFT:MW_SHIFT + MW_TILE].astype(o_ref.dtype)


def merge_weights(w_in, layer):
    D = D_MODEL
    a0 = (MERGE_START - MW_SHIFT) // MW_TILE
    b0 = (MERGE_START - MW_SHIFT) // LANES
    return pl.pallas_call(
        _merge_weights_kernel,
        grid=(D // MW_TILE, N_BRANCH * D // MW_TILE),
        in_specs=[pl.BlockSpec((None, MW_TILE, MW_TILE), lambda i, j: (layer, i, a0 + j)),
                  pl.BlockSpec((None, MW_TILE, LANES), lambda i, j: (layer, i, b0 + (j + 1) * (MW_TILE // LANES)))],
        out_specs=pl.BlockSpec((MW_TILE, MW_TILE), lambda i, j: (i, j)),
        out_shape=jax.ShapeDtypeStruct((D, N_BRANCH * D), BF16),
        compiler_params=_cparams("parallel", "parallel"),
        name="merge_weights",
    )(w_in, w_in)


def _merge_kernel(h_ref, b0, b1, b2, b3, m0, m1, m2, m3, w0, w1, w2, w3, o_ref):
    h = h_ref[...]
    acc = None
    for br, wm, wb in ((b0, m0, w0), (b1, m1, w1), (b2, m2, w2), (b3, m3, w3)):
        gate = _sigmoid(jnp.dot(h, wm[...], preferred_element_type=F32))
        t = gate * jnp.dot(br[...], wb[...], preferred_element_type=F32)
        acc = t if acc is None else acc + t
    o_ref[...] = acc.astype(o_ref.dtype)


def merge_branches(h_all, branches, w_merge, w_branch, n_rows, tm, tn=256):
    D = D_MODEL
    BW = BRANCH_WIDTH
    nj = D // tn
    one = pl.Buffered(1)
    in_specs = [pl.BlockSpec((tm, D), lambda i, j: (i, 0), pipeline_mode=one)]
    in_specs += [pl.BlockSpec((tm, BW), lambda i, j: (i, 0), pipeline_mode=one) for _ in range(N_BRANCH)]
    in_specs += [pl.BlockSpec((D, tn), functools.partial(lambda i, j, b: (0, b * nj + j), b=b))
                 for b in range(N_BRANCH)]
    in_specs += [pl.BlockSpec((None, BW, tn), functools.partial(lambda i, j, b: (b, 0, j), b=b))
                 for b in range(N_BRANCH)]
    return pl.pallas_call(
        _merge_kernel,
        grid=(n_rows // tm, nj),
        in_specs=in_specs,
        out_specs=pl.BlockSpec((tm, tn), lambda i, j: (i, j)),
        out_shape=jax.ShapeDtypeStruct((n_rows, D), BF16),
        compiler_params=_cparams("parallel", "arbitrary"),
        name="gated_merge",
    )(h_all, *branches, *([w_merge] * N_BRANCH), *([w_branch] * N_BRANCH))


def _out_ln_kernel(y_ref, w_ref, x_ref, g1_ref, lg_ref, lb_ref, sh_ref, sc_ref, x1_ref, hm_ref, *, nk, tm):
    k = pl.program_id(1)
    part = jnp.dot(y_ref[...], w_ref[...], preferred_element_type=F32)

    @pl.when(k == 0)
    def _():
        x1_ref[...] = part

    @pl.when(k > 0)
    def _():
        x1_ref[...] += part

    @pl.when(k == nk - 1)
    def _():
        sub = next(s for s in (64, 48, 32, 16) if tm % s == 0)

        def chunk(r, carry):
            r0 = pl.multiple_of(r * sub, 16)
            rows = pl.ds(r0, sub)
            is_ctx = pl.program_id(0) * tm + r0 + lax.broadcasted_iota(jnp.int32, (sub, 1), 0) >= SEQ
            z = ALPHA * x_ref[rows, :] + _pick(g1_ref, is_ctx) * x1_ref[rows, :]
            x1 = _ln_rows(z, lg_ref[...], lb_ref[...])
            x1_ref[rows, :] = x1
            hm_ref[rows, :] = (x1 * (1.0 + _pick(sc_ref, is_ctx)) + _pick(sh_ref, is_ctx)).astype(hm_ref.dtype)
            return carry

        lax.fori_loop(0, tm // sub, chunk, 0)


def out_proj_ln(y, w_out, x_all, mod_l, ln_g, ln_b, n_rows, tm, tk=512):
    D = D_MODEL
    nk = D // tk
    kern = functools.partial(_out_ln_kernel, nk=nk, tm=tm)
    vec = pl.BlockSpec((1, D), lambda i, k: (0, 0))

    def modspec(col):
        return pl.BlockSpec((16, D), lambda i, k: (0, col))

    return pl.pallas_call(
        kern,
        grid=(n_rows // tm, nk),
        in_specs=[pl.BlockSpec((tm, tk), lambda i, k: (i, k)),
                  pl.BlockSpec((tk, D), lambda i, k: (k, 0)),
                  pl.BlockSpec((tm, D), lambda i, k: (i, 0), pipeline_mode=pl.Buffered(1)),
                  modspec(2), vec, vec, modspec(3), modspec(4)],
        out_specs=[pl.BlockSpec((tm, D), lambda i, k: (i, 0)),
                   pl.BlockSpec((tm, D), lambda i, k: (i, 0))],
        out_shape=[jax.ShapeDtypeStruct((n_rows, D), F32),
                   jax.ShapeDtypeStruct((n_rows, D), BF16)],
        compiler_params=_cparams("parallel", "arbitrary"),
        name="out_proj_ln1",
    )(y, w_out, x_all, mod_l, ln_g.reshape(1, D), ln_b.reshape(1, D), mod_l, mod_l)


def _ln2_kernel(x_ref, y_ref, g2_ref, lg_ref, lb_ref, sh_ref, sc_ref, x2_ref, *h_ref, blk_ctx_from):
    is_ctx = pl.program_id(0) >= blk_ctx_from
    z = ALPHA * x_ref[...] + _pick(g2_ref, is_ctx) * y_ref[...]
    x2 = _ln_rows(z, lg_ref[...], lb_ref[...])
    x2_ref[...] = x2
    if h_ref:
        h_ref[0][...] = (x2 * (1.0 + _pick(sc_ref, is_ctx)) + _pick(sh_ref, is_ctx)).astype(BF16)


def ffn_residual_ln(x1, y_moe, mod_l, ln_g, ln_b, mod_next, n_rows):
    D = D_MODEL
    want_h = mod_next is not None
    nxt = mod_next if want_h else mod_l
    kern = functools.partial(_ln2_kernel, blk_ctx_from=LAT_ROW_BLKS)
    row = pl.BlockSpec((ROW_BLK, D), lambda i: (i, 0))
    vec = pl.BlockSpec((1, D), lambda i: (0, 0))
    out_specs = [row, row] if want_h else [row]
    out_shape = [jax.ShapeDtypeStruct((n_rows, D), F32)]
    if want_h:
        out_shape.append(jax.ShapeDtypeStruct((n_rows, D), BF16))
    return pl.pallas_call(
        kern,
        grid=(n_rows // ROW_BLK,),
        in_specs=[row, row, pl.BlockSpec((16, D), lambda i: (0, 5)), vec, vec,
                  pl.BlockSpec((16, D), lambda i: (0, 0)), pl.BlockSpec((16, D), lambda i: (0, 1))],
        out_specs=out_specs,
        out_shape=out_shape,
        compiler_params=_cparams("parallel"),
    )(x1, y_moe, mod_l, ln_g.reshape(1, D), ln_b.reshape(1, D), nxt, nxt)


def _swiglu_kernel(x_ref, wg_ref, wu_ref, o_ref):
    x = x_ref[...]
    g = jnp.dot(x, wg_ref[...].astype(BF16), preferred_element_type=F32)
    u = jnp.dot(x, wu_ref[...].astype(BF16), preferred_element_type=F32)
    o_ref[...] = (g * _sigmoid(g) * u).astype(o_ref.dtype)


def expert_swiglu(xe, w_gate, w_up, layer, tn=256):
    E, M, K = xe.shape
    N = w_gate.shape[-1]
    wspec = pl.BlockSpec((None, None, K, tn), lambda e, j: (layer, e, 0, j))
    return pl.pallas_call(
        _swiglu_kernel,
        grid=(E, N // tn),
        in_specs=[pl.BlockSpec((None, M, K), lambda e, j: (e, 0, 0)), wspec, wspec],
        out_specs=pl.BlockSpec((None, M, tn), lambda e, j: (e, 0, j)),
        out_shape=jax.ShapeDtypeStruct((E, M, N), BF16),
        compiler_params=_cparams("parallel", "parallel"),
    )(xe, w_gate, w_up)


def _down_kernel(h_ref, w_ref, g_ref, o_ref):
    y = jnp.dot(h_ref[...], w_ref[...].astype(BF16), preferred_element_type=F32)
    o_ref[...] = y * g_ref[...]


def expert_down(hid, w_down, gains, layer, tn=1024):
    E, M, K = hid.shape
    N = w_down.shape[-1]
    return pl.pallas_call(
        _down_kernel,
        grid=(E, N // tn),
        in_specs=[pl.BlockSpec((None, M, K), lambda e, j: (e, 0, 0)),
                  pl.BlockSpec((None, None, K, tn), lambda e, j: (layer, e, 0, j)),
                  pl.BlockSpec((None, M, 1), lambda e, j: (e, 0, 0))],
        out_specs=pl.BlockSpec((None, M, tn), lambda e, j: (e, 0, j)),
        out_shape=jax.ShapeDtypeStruct((E, M, N), F32),
        compiler_params=_cparams("parallel", "parallel"),
    )(hid, w_down, gains)


def _scatter_kernel(idx_ref, ye_ref, acc_in, acc_ref, buf, gsem, ssem, *, R, n_chunks):
    del acc_in
    c = pl.program_id(1)

    def gather(chunk, slot, r):
        t = idx_ref[0, chunk * R + r]
        return pltpu.make_async_copy(acc_ref.at[pl.ds(t, 1), :], buf.at[slot, pl.ds(r, 1), :], gsem.at[slot])

    def scatter(chunk, slot, r):
        t = idx_ref[0, chunk * R + r]
        return pltpu.make_async_copy(buf.at[slot, pl.ds(r, 1), :], acc_ref.at[pl.ds(t, 1), :], ssem.at[slot])

    def each_row(fn):
        def body(r, carry):
            fn(r)
            return carry
        lax.fori_loop(0, R, body, 0, unroll=8)

    slot = c % 2

    @pl.when(c == 0)
    def _():
        each_row(lambda r: gather(0, 0, r).start())

    each_row(lambda r: gather(c, slot, r).wait())

    @pl.when(c >= 1)
    def _():
        each_row(lambda r: scatter(c - 1, 1 - slot, r).wait())

    @pl.when(c + 1 < n_chunks)
    def _():
        each_row(lambda r: gather(c + 1, 1 - slot, r).start())

    buf[slot] = buf[slot] + ye_ref[...]
    each_row(lambda r: scatter(c, slot, r).start())

    @pl.when(c == n_chunks - 1)
    def _():
        each_row(lambda r: scatter(c, slot, r).wait())


def scatter_add_rows(idx, ye, n_rows):
    E, M, D = ye.shape
    R = next(r for r in (264, 256, 160, 128, 64, 32) if M % r == 0)
    n_chunks = M // R
    kern = functools.partial(_scatter_kernel, R=R, n_chunks=n_chunks)
    return pl.pallas_call(
        kern,
        grid=(E, n_chunks),
        in_specs=[pl.BlockSpec((None, 1, M), lambda e, c: (e, 0, 0), memory_space=pltpu.SMEM),
                  pl.BlockSpec((None, R, D), lambda e, c: (e, c, 0)),
                  pl.BlockSpec(memory_space=pl.ANY)],
        out_specs=pl.BlockSpec(memory_space=pl.ANY),
        out_shape=jax.ShapeDtypeStruct((n_rows, D), F32),
        scratch_shapes=[pltpu.VMEM((2, R, D), F32), pltpu.SemaphoreType.DMA((2,)), pltpu.SemaphoreType.DMA((2,))],
        input_output_aliases={2: 0},
        compiler_params=_cparams("arbitrary", "arbitrary"),
        name="scatter_add_rows",
    )(idx.reshape(E, 1, M), ye, jnp.zeros((n_rows, D), F32))


N_CHUNKS = N_TOK // ML_CHUNK
LAT_CHUNKS = SEQ // ML_CHUNK
CTX_CHUNKS = CTX_LEN // ML_CHUNK


def _mlstm_kernel(q_ref, k_ref, v_ref, og_ref, g_ref, ng_ref, out_ref, acc_ref, c_ref, n_ref, m_ref, *, hb, out_chunks):
    L, d = ML_CHUNK, ML_HEAD_DIM
    row = lax.broadcasted_iota(jnp.int32, (L, L), 0)
    col = lax.broadcasted_iota(jnp.int32, (L, L), 1)
    eye = row == col
    masks = (col <= row, col >= row)
    lasts = (L - 1, 0)
    k_scale = ML_HEAD_DIM ** -0.5

    acc_ref[...] = jnp.zeros(acc_ref.shape, F32)
    c_ref[...] = jnp.zeros(c_ref.shape, F32)
    n_ref[...] = jnp.zeros(n_ref.shape, F32)
    m_ref[...] = jnp.zeros(m_ref.shape, F32)

    def col_of(r):
        return jnp.sum(jnp.where(eye, r, 0.0), axis=1, keepdims=True)

    def step(t, carry):
        chunk = (jnp.where(t < CTX_CHUNKS, t + LAT_CHUNKS, t - CTX_CHUNKS), N_CHUNKS - 1 - t)
        for j in range(hb):
            hs = slice(j * d, (j + 1) * d)
            for dr in range(2):
                si = 2 * j + dr
                off = pl.multiple_of(chunk[dr] * L, L)
                rows = pl.ds(off, L)
                q = q_ref[rows, hs]
                kf = k_ref[rows, hs].astype(F32) * k_scale
                v = v_ref[rows, hs]
                b_row = g_ref[j, 2 * dr:2 * dr + 1, rows]
                a_row = g_ref[j, 2 * dr + 1:2 * dr + 2, rows]
                last = lasts[dr]
                m = m_ref[si][0:1, 0:1]
                n = n_ref[si][0:1, :]
                C = c_ref[si]
                b_col = col_of(b_row)
                a_col = col_of(a_row)
                dlog = jnp.where(masks[dr], b_col + a_row, -jnp.inf)
                inter = b_col + m
                mj = jnp.maximum(inter, jnp.max(dlog, axis=1, keepdims=True))
                w_inter = jnp.exp(inter - mj)
                qk = lax.dot_general(q, kf.astype(BF16), (((1,), (1,)), ((), ())), preferred_element_type=F32)
                s = qk * jnp.exp(dlog - mj)
                num = (w_inter * jnp.dot(q, C.astype(BF16), preferred_element_type=F32)
                       + jnp.dot(s.astype(BF16), v, preferred_element_type=F32))
                qn = jnp.sum(q.astype(F32) * n, axis=1, keepdims=True)
                den = w_inter * qn + jnp.sum(s, axis=1, keepdims=True)
                h = num / jnp.maximum(jnp.abs(den), jnp.exp(-mj))
                acc_ref[rows, hs] += h
                m_new = mj[last:last + 1, :]
                b_end = b_row[:, last:last + 1]
                w_c = jnp.exp(b_end + m - m_new)
                kw = kf * jnp.exp(b_end + a_col - m_new)
                c_ref[si] = w_c * C + lax.dot_general(kw.astype(BF16), v, (((0,), (0,)), ((), ())),
                                                      preferred_element_type=F32)
                n_ref[si] = jnp.broadcast_to(w_c * n + jnp.sum(kw, axis=0, keepdims=True), (8, d))
                m_ref[si] = jnp.broadcast_to(m_new, (8, LANES))
        return carry

    lax.fori_loop(0, N_CHUNKS, step, 0)

    def finish(c, carry):
        rows = pl.ds(pl.multiple_of(c * L, L), L)
        for j in range(hb):
            hs = slice(j * d, (j + 1) * d)
            hh = acc_ref[rows, hs]
            hc = hh - jnp.mean(hh, axis=1, keepdims=True)
            hn = hc * lax.rsqrt(jnp.mean(hc * hc, axis=1, keepdims=True) + EPS) * ng_ref[j]
            out_ref[rows, hs] = (_sigmoid(og_ref[rows, hs].astype(F32)) * hn).astype(out_ref.dtype)
        return carry

    lax.fori_loop(0, out_chunks, finish, 0)


def mlstm(P, G, norm_g, n_rows, hb=2):
    w = hb * ML_HEAD_DIM
    per = BRANCH_WIDTH // w

    def colspec(col):
        return pl.BlockSpec((N_TOK, w), lambda i: (0, col * per + i))

    kern = functools.partial(_mlstm_kernel, hb=hb, out_chunks=n_rows // ML_CHUNK)
    return pl.pallas_call(
        kern,
        grid=(ML_HEADS // hb,),
        in_specs=[colspec(COL_ML_Q), colspec(COL_ML_K), colspec(COL_ML_V),
                  pl.BlockSpec((n_rows, w), lambda i: (0, COL_ML_O * per + i)),
                  pl.BlockSpec((hb, 4, N_TOK), lambda i: (i, 0, 0)),
                  pl.BlockSpec((hb, 1, ML_HEAD_DIM), lambda i: (i, 0, 0))],
        out_specs=pl.BlockSpec((n_rows, w), lambda i: (0, i)),
        out_shape=jax.ShapeDtypeStruct((n_rows, BRANCH_WIDTH), BF16),
        scratch_shapes=[pltpu.VMEM((N_TOK, w), F32),
                        pltpu.VMEM((2 * hb, ML_HEAD_DIM, ML_HEAD_DIM), F32),
                        pltpu.VMEM((2 * hb, 8, ML_HEAD_DIM), F32),
                        pltpu.VMEM((2 * hb, 8, LANES), F32)],
        compiler_params=_cparams("parallel"),
        name="mlstm_scan",
    )(P, P, P, P, G, norm_g.reshape(ML_HEADS, 1, ML_HEAD_DIM))


def _norm_stats(x):
    xc = x - jnp.mean(x, axis=-1, keepdims=True)
    return xc * lax.rsqrt(jnp.mean(xc * xc, axis=-1, keepdims=True) + EPS)


def _rope_tables():
    rows = SEQ // GRID_W
    row = jnp.repeat(jnp.arange(rows), GRID_W).astype(F32)
    col = (jnp.arange(SEQ) % GRID_W).astype(F32)
    n_freq = HEAD_DIM // 4
    inv = ROPE_THETA ** (-jnp.arange(n_freq, dtype=F32) / n_freq)
    ang_r = row[:, None] * inv[None, :]
    ang_c = col[:, None] * inv[None, :]
    cos = jnp.concatenate([jnp.cos(ang_r), jnp.cos(ang_r), jnp.cos(ang_c), jnp.cos(ang_c)], axis=1)
    sin = jnp.concatenate([-jnp.sin(ang_r), jnp.sin(ang_r), -jnp.sin(ang_c), jnp.sin(ang_c)], axis=1)
    cos = jnp.concatenate([cos, jnp.ones((CTX_LEN, HEAD_DIM), F32)], axis=0)
    sin = jnp.concatenate([sin, jnp.zeros((CTX_LEN, HEAD_DIM), F32)], axis=0)
    return cos, sin


def _mlstm_chunk_scan(q, k, v, a_row, b_row, state, reverse):
    H, T, d = q.shape
    nc = T // ML_CHUNK

    def chunks(x):
        return jnp.moveaxis(x.reshape(H, nc, ML_CHUNK, *x.shape[2:]), 1, 0)

    tri = jnp.tril(jnp.ones((ML_CHUNK, ML_CHUNK), dtype=bool))
    mask = tri.T if reverse else tri
    last = 0 if reverse else ML_CHUNK - 1

    def step(carry, inp):
        C, n, m = carry
        qc, kc, vc, ac, bc = inp
        inter = bc + m[..., None]
        dlog = jnp.where(mask, bc[..., :, None] + ac[..., None, :], -jnp.inf)
        mj = jnp.maximum(inter, jnp.max(dlog, axis=-1))
        w_inter = jnp.exp(inter - mj)
        s = jnp.einsum('hjd,hsd->hjs', qc, kc) * jnp.exp(dlog - mj[..., None])
        num = (w_inter[..., None] * jnp.einsum('hjd,hde->hje', qc, C)
               + jnp.einsum('hjs,hse->hje', s, vc))
        den = w_inter * jnp.einsum('hjd,hd->hj', qc, n) + jnp.sum(s, axis=-1)
        h = num / jnp.maximum(jnp.abs(den), jnp.exp(-mj))[..., None]
        m_new = mj[..., last]
        b_end = bc[..., last]
        w_c = jnp.exp(b_end + m - m_new)
        w_s = jnp.exp(b_end[..., None] + ac - m_new[..., None])
        C_new = w_c[..., None, None] * C + jnp.einsum('hs,hsd,hse->hde', w_s, kc, vc)
        n_new = w_c[..., None] * n + jnp.einsum('hs,hsd->hd', w_s, kc)
        return (C_new, n_new, m_new), h

    final, h = lax.scan(step, state, (chunks(q), chunks(k), chunks(v), chunks(a_row), chunks(b_row)),
                        reverse=reverse)
    return jnp.moveaxis(h, 0, 1).reshape(H, T, d), final


def _mlstm(P, G, norm_g, rows_out):
    BW = BRANCH_WIDTH

    def heads(col, rows):
        a = P[rows, col * BW:(col + 1) * BW].astype(F32)
        return jnp.swapaxes(a.reshape(-1, ML_HEADS, ML_HEAD_DIM), 0, 1)

    H = ML_HEADS
    zero = (jnp.zeros((H, ML_HEAD_DIM, ML_HEAD_DIM), F32), jnp.zeros((H, ML_HEAD_DIM), F32), jnp.zeros((H,), F32))
    outs = {}
    st_f, st_b = zero, zero
    for name, rows in (('ctx', slice(SEQ, N_TOK)), ('lat', slice(0, SEQ))):
        q = heads(COL_ML_Q, rows)
        k = heads(COL_ML_K, rows) * (ML_HEAD_DIM ** -0.5)
        v = heads(COL_ML_V, rows)
        h_f, st_f = _mlstm_chunk_scan(q, k, v, G[H:2 * H, rows], G[0:H, rows], st_f, False)
        h_b, st_b = _mlstm_chunk_scan(q, k, v, G[3 * H:4 * H, rows], G[2 * H:3 * H, rows], st_b, True)
        outs[name] = h_f + h_b
    h = jnp.concatenate([outs['lat'], outs['ctx']], axis=1)[:, :rows_out]
    hn = _norm_stats(jnp.swapaxes(h, 0, 1)) * norm_g.reshape(H, ML_HEAD_DIM)
    o_pre = P[:rows_out, COL_ML_O * BW:(COL_ML_O + 1) * BW].astype(F32)
    return (jax.nn.sigmoid(o_pre) * hn.reshape(rows_out, BW)).astype(BF16)


def _expert_choice_route(hm, w_router, cap):
    logits = jnp.dot(hm, w_router.astype(BF16), preferred_element_type=F32)
    aff = jax.nn.softmax(logits, axis=-1)
    return lax.top_k(aff.T, cap)


def kernel(x, c, ctx, c_ctx, w_mod, b_mod, w_in, att_q_gain, att_k_gain, gm_ln_g, gm_ln_b,
           gm_w_s, gm_b_s, conv_w, conv_b, conv_ln_g, conv_ln_b, ml_gate_bias, ml_norm_g,
           w_branch, w_out, ln1_g, ln1_b, w_router, w_gate, w_up, w_down, ln2_g, ln2_b):
    cos_t, sin_t = _rope_tables()
    cc = jnp.zeros((16, D_MODEL), F32).at[0].set(c[0]).at[1].set(c_ctx)
    mods = modulation(cc, w_mod, b_mod)
    x_all = jnp.concatenate([x[0], ctx[0]], axis=0)
    h_all = modulate(x_all, mods[0])
    for l in range(DEPTH):
        last = l == DEPTH - 1
        n_rows = SEQ if last else N_TOK
        tm = (SEQ if last else N_TOK) // 8
        mod_l = mods[l]

        P = branch_projection(h_all, w_in, l)
        gbias = jnp.broadcast_to(ml_gate_bias[l].reshape(GATE_COLS, 1), (GATE_COLS, LANES))
        G = mlstm_gate_rows(h_all, w_in, l, gbias)
        q, k = qk_prepare(P, cos_t, sin_t, att_q_gain[l], att_k_gain[l])

        br_a = chunk_gmlp(P, gm_ln_g[l], gm_ln_b[l], gm_w_s[l], gm_b_s[l], n_rows)
        br_b = attention(q, k, P, True)
        if not last:
            br_b = jnp.concatenate([br_b, attention(q, k, P, False)], axis=0)
        br_c = conformer_conv(P, conv_w[l], conv_b[l], conv_ln_g[l], conv_ln_b[l], n_rows)
        G_heads = G.reshape(ML_N_GATES, ML_HEADS, N_TOK).transpose(1, 0, 2)
        br_d = mlstm(P, G_heads, ml_norm_g[l], n_rows)

        w_merge = merge_weights(w_in, l)
        y = merge_branches(h_all, [br_a, br_b, br_c, br_d], w_merge, w_branch[l].astype(BF16), n_rows, tm)
        x1, hm = out_proj_ln(y, w_out[l].astype(BF16), x_all, mod_l, ln1_g[l], ln1_b[l], n_rows, tm // 2)

        cap_l = CAPACITY_FACTOR * SEQ // N_EXPERTS
        g_l, idx_l = _expert_choice_route(hm[:SEQ], w_router[l], cap_l)
        gains, idx = g_l, idx_l
        if not last:
            g_c, idx_c = _expert_choice_route(hm[SEQ:], w_router[l], CAPACITY_FACTOR * CTX_LEN // N_EXPERTS)
            gains = jnp.concatenate([g_l, g_c], axis=1)
            idx = jnp.concatenate([idx_l, idx_c + SEQ], axis=1)
        xe = hm[idx]
        hid = expert_swiglu(xe, w_gate, w_up, l)
        ye = expert_down(hid, w_down, gains[..., None], l)
        y_moe = scatter_add_rows(idx, ye, n_rows)
        res = ffn_residual_ln(x1, y_moe, mod_l, ln2_g[l], ln2_b[l], None if last else mods[l + 1], n_rows)
        if not last:
            x_all, h_all = res
        else:
            x_all = res[0]
    return x_all[None]
```

```python
import functools

import jax
import jax.numpy as jnp
from jax import lax
from jax.experimental import pallas as pl
from jax.experimental.pallas import tpu as pltpu

F32 = jnp.float32
BF16 = jnp.bfloat16

D_MODEL = 4096
SEQ = 8192
DEPTH = 2
GRID_W = 64
CTX_LEN = 256
N_BRANCH = 4
BRANCH_WIDTH = D_MODEL // N_BRANCH
HEAD_DIM = 128
GM_CHUNK = 128
GM_GROUPS = BRANCH_WIDTH // HEAD_DIM
ATT_HEADS = BRANCH_WIDTH // HEAD_DIM
ATT_KV_HEADS = 2
ATT_REP = ATT_HEADS // ATT_KV_HEADS
KV_WIDTH = ATT_KV_HEADS * HEAD_DIM
ROPE_THETA = 10000.0
CONV_K = 31
ML_HEADS = BRANCH_WIDTH // HEAD_DIM
ML_HEAD_DIM = HEAD_DIM
ML_CHUNK = 128
ML_N_GATES = 4
N_EXPERTS = 16
EXPERT_FF = D_MODEL // 4
CAPACITY_FACTOR = 2
ALPHA = (2 * DEPTH) ** 0.25
EPS = 1e-6
N_MOD = 6

N_TOK = SEQ + CTX_LEN
SEG1 = 9 * BRANCH_WIDTH + 2 * KV_WIDTH
GATE_COLS = ML_N_GATES * ML_HEADS
MERGE_START = SEG1 + GATE_COLS

V7X_VMEM_LIMIT = 56 * 1024 * 1024
LANES = 128

GATE_TOK_TILE = next(t for t in (1408, 1280, 1024, 768, 512, 384, 256, 128) if N_TOK % t == 0)
ROW_BLK = 256
N_ROW_BLKS = N_TOK // ROW_BLK
LAT_ROW_BLKS = SEQ // ROW_BLK

P_TILE = 512
P_KV_TILE_IN = (3 * BRANCH_WIDTH) // P_TILE
P_N_TILES = SEG1 // P_TILE
COL_GM_U, COL_GM_V, COL_ATT_Q = 0, 1, 2
COL_CV_A, COL_CV_B, COL_ML_Q, COL_ML_K, COL_ML_V, COL_ML_O = 3, 4, 5, 6, 7, 8
P_K_COL = 9 * BRANCH_WIDTH


def _cparams(*sem):
    return pltpu.CompilerParams(dimension_semantics=sem, vmem_limit_bytes=V7X_VMEM_LIMIT)


def _sigmoid(x):
    return 1.0 / (1.0 + jnp.exp(-x))


def _gelu_tanh(x):
    return 0.5 * x * (1.0 + jnp.tanh(0.7978845608028654 * (x + 0.044715 * (x * x * x))))


def _ln_rows(z, g, b):
    mu = jnp.mean(z, axis=-1, keepdims=True)
    zc = z - mu
    var = jnp.mean(zc * zc, axis=-1, keepdims=True)
    return zc * lax.rsqrt(var + EPS) * g + b


def _dot_nt(a, b_t):
    return lax.dot_general(a, b_t, (((1,), (1,)), ((), ())), preferred_element_type=F32)


def _pick(ref, is_ctx):
    return jnp.where(is_ctx, ref[1:2, :], ref[0:1, :])


def _mod_kernel(c_ref, w_ref, b_ref, o_ref):
    c = c_ref[...]
    a = (c * _sigmoid(c)).astype(BF16)
    o_ref[...] = jnp.dot(a, w_ref[...].astype(BF16), preferred_element_type=F32) + b_ref[...]


def modulation(cc, w_mod, b_mod, tn=1024):
    L, D, N = w_mod.shape
    return pl.pallas_call(
        _mod_kernel,
        grid=(L, N // tn),
        in_specs=[pl.BlockSpec((16, D), lambda l, j: (0, 0)),
                  pl.BlockSpec((None, D, tn), lambda l, j: (l, 0, j)),
                  pl.BlockSpec((None, 1, tn), lambda l, j: (l, 0, j))],
        out_specs=pl.BlockSpec((None, 16, tn), lambda l, j: (l, 0, j)),
        out_shape=jax.ShapeDtypeStruct((L, 16, N), F32),
        compiler_params=_cparams("parallel", "parallel"),
    )(cc, w_mod, b_mod.reshape(L, 1, N))


def _modulate_kernel(x_ref, sh_ref, sc_ref, o_ref):
    is_ctx = pl.program_id(0) >= LAT_ROW_BLKS
    o_ref[...] = (x_ref[...] * (1.0 + _pick(sc_ref, is_ctx)) + _pick(sh_ref, is_ctx)).astype(o_ref.dtype)


def modulate(x_all, mod_l):
    D = D_MODEL
    return pl.pallas_call(
        _modulate_kernel,
        grid=(N_ROW_BLKS,),
        in_specs=[pl.BlockSpec((ROW_BLK, D), lambda i: (i, 0)),
                  pl.BlockSpec((16, D), lambda i: (0, 0)),
                  pl.BlockSpec((16, D), lambda i: (0, 1))],
        out_specs=pl.BlockSpec((ROW_BLK, D), lambda i: (i, 0)),
        out_shape=jax.ShapeDtypeStruct((N_TOK, D), BF16),
        compiler_params=_cparams("parallel"),
    )(x_all, mod_l, mod_l)


def _proj_kernel(a_ref, w_ref, o_ref, wb_ref):
    @pl.when(pl.program_id(1) == 0)
    def _():
        wb_ref[...] = w_ref[...].astype(BF16)

    o_ref[...] = _dot_nt(a_ref[...], wb_ref[...]).astype(o_ref.dtype)


def _p_out_tile(j):
    return jnp.where(j < P_KV_TILE_IN, j, jnp.where(j == P_KV_TILE_IN, P_N_TILES - 1, j - 1))


def branch_projection(h_all, w_in_t, layer, tm=N_TOK // 8):
    M, K = h_all.shape
    return pl.pallas_call(
        _proj_kernel,
        grid=(P_N_TILES, M // tm),
        in_specs=[pl.BlockSpec((tm, K), lambda j, i: (i, 0)),
                  pl.BlockSpec((None, P_TILE, K), lambda j, i: (layer, j, 0))],
        out_specs=pl.BlockSpec((tm, P_TILE), lambda j, i: (i, _p_out_tile(j))),
        out_shape=jax.ShapeDtypeStruct((M, SEG1), BF16),
        scratch_shapes=[pltpu.VMEM((P_TILE, K), BF16)],
        compiler_params=_cparams("parallel", "arbitrary"),
        name="branch_projection",
    )(h_all, w_in_t)


def _gates_kernel(w_ref, h_ref, bias_ref, o_ref, *, n_chunks):
    z = _dot_nt(w_ref[...].astype(BF16), h_ref[...])
    z = z + bias_ref[:, 0:1]
    lf = jnp.minimum(z, 0.0) - jnp.log1p(jnp.exp(-jnp.abs(z)))
    row = lax.broadcasted_iota(jnp.int32, (ML_CHUNK, ML_CHUNK), 0)
    col = lax.broadcasted_iota(jnp.int32, (ML_CHUNK, ML_CHUNK), 1)
    upper = (row <= col).astype(F32)
    lower = (row >= col).astype(F32)
    H = ML_HEADS
    for c in range(n_chunks):
        sl = slice(c * ML_CHUNK, (c + 1) * ML_CHUNK)
        b_f = jnp.dot(lf[H:2 * H, sl], upper, preferred_element_type=F32, precision=lax.Precision.HIGHEST)
        b_b = jnp.dot(lf[3 * H:4 * H, sl], lower, preferred_element_type=F32, precision=lax.Precision.HIGHEST)
        o_ref[:, sl] = jnp.concatenate([b_f, z[0:H, sl] - b_f, b_b, z[2 * H:3 * H, sl] - b_b], axis=0)


def mlstm_gate_rows(h_all, w_in_t, layer, bias, tt=GATE_TOK_TILE):
    M, K = h_all.shape
    kern = functools.partial(_gates_kernel, n_chunks=tt // ML_CHUNK)
    return pl.pallas_call(
        kern,
        grid=(M // tt,),
        in_specs=[pl.BlockSpec((None, GATE_COLS, K), lambda i: (layer, SEG1 // GATE_COLS, 0)),
                  pl.BlockSpec((tt, K), lambda i: (i, 0)),
                  pl.BlockSpec((GATE_COLS, LANES), lambda i: (0, 0))],
        out_specs=pl.BlockSpec((GATE_COLS, tt), lambda i: (0, i)),
        out_shape=jax.ShapeDtypeStruct((GATE_COLS, M), F32),
        compiler_params=_cparams("parallel"),
    )(w_in_t, h_all, bias)


def _rope_head(x, cos, sin, first_half):
    partner = jnp.where(first_half, pltpu.roll(x, 96, 1), pltpu.roll(x, 32, 1))
    return x * cos + partner * sin


def _qk_kernel(q_ref, kv_ref, cos_ref, sin_ref, qg_ref, kg_ref, qo_ref, ko_ref):
    cos = cos_ref[...]
    sin = sin_ref[...]
    lane = lax.broadcasted_iota(jnp.int32, (ROW_BLK, HEAD_DIM), 1)
    first_half = jnp.bitwise_and(lane, 63) < 32

    def prep(x, gain, scale):
        x = x.astype(F32)
        y = x * lax.rsqrt(jnp.mean(x * x, axis=-1, keepdims=True) + EPS) * gain
        return (_rope_head(y, cos, sin, first_half) * scale).astype(BF16)

    for h in range(ATT_HEADS):
        sl = slice(h * HEAD_DIM, (h + 1) * HEAD_DIM)
        qo_ref[:, sl] = prep(q_ref[:, sl], qg_ref[...], HEAD_DIM ** -0.5)
    for h in range(ATT_KV_HEADS):
        sl = slice(h * HEAD_DIM, (h + 1) * HEAD_DIM)
        ko_ref[:, sl] = prep(kv_ref[:, sl], kg_ref[...], 1.0)


def qk_prepare(P, cos_t, sin_t, q_gain, k_gain):
    return pl.pallas_call(
        _qk_kernel,
        grid=(N_ROW_BLKS,),
        in_specs=[pl.BlockSpec((ROW_BLK, BRANCH_WIDTH), lambda i: (i, COL_ATT_Q)),
                  pl.BlockSpec((ROW_BLK, P_TILE), lambda i: (i, P_N_TILES - 1)),
                  pl.BlockSpec((ROW_BLK, HEAD_DIM), lambda i: (i, 0)),
                  pl.BlockSpec((ROW_BLK, HEAD_DIM), lambda i: (i, 0)),
                  pl.BlockSpec((1, HEAD_DIM), lambda i: (0, 0)),
                  pl.BlockSpec((1, HEAD_DIM), lambda i: (0, 0))],
        out_specs=[pl.BlockSpec((ROW_BLK, BRANCH_WIDTH), lambda i: (i, 0)),
                   pl.BlockSpec((ROW_BLK, KV_WIDTH), lambda i: (i, 0))],
        out_shape=[jax.ShapeDtypeStruct((N_TOK, BRANCH_WIDTH), BF16),
                   jax.ShapeDtypeStruct((N_TOK, KV_WIDTH), BF16)],
        compiler_params=_cparams("parallel"),
    )(P, P, cos_t, sin_t, q_gain.reshape(1, HEAD_DIM), k_gain.reshape(1, HEAD_DIM))


def _attn_kernel(q_ref, k1_ref, v1_ref, k2_ref, v2_ref, o_ref, *, tq, ck, n_chunks):
    rows = ATT_REP * tq
    q = jnp.concatenate([q_ref[:, h * HEAD_DIM:(h + 1) * HEAD_DIM] for h in range(ATT_REP)], axis=0)

    def step(k, v, carry):
        m, l, acc = carry
        s = lax.dot_general(q, k, (((1,), (1,)), ((), ())), preferred_element_type=F32)
        m_new = jnp.maximum(m, jnp.max(s, axis=-1, keepdims=True))
        p = jnp.exp(s - m_new)
        a = jnp.exp(m - m_new)
        l = a * l + jnp.sum(p, axis=-1, keepdims=True)
        acc = a * acc + jnp.dot(p.astype(BF16), v, preferred_element_type=F32)
        return m_new, l, acc

    def body(c, carry):
        off = pl.multiple_of(c * ck, ck)
        return step(k1_ref[pl.ds(off, ck), :], v1_ref[pl.ds(off, ck), :], carry)

    carry = (jnp.full((rows, 1), -jnp.inf, F32), jnp.zeros((rows, 1), F32),
             jnp.zeros((rows, HEAD_DIM), F32))
    if n_chunks:
        carry = lax.fori_loop(0, n_chunks, body, carry, unroll=min(4, n_chunks))
    _, l, acc = step(k2_ref[...], v2_ref[...], carry)
    o = acc / l
    for h in range(ATT_REP):
        o_ref[:, h * HEAD_DIM:(h + 1) * HEAD_DIM] = o[h * tq:(h + 1) * tq].astype(o_ref.dtype)


def attention(q, k, P, latent, tq=128, ck=1024):
    gw = ATT_REP * HEAD_DIM
    v_col = (P_K_COL + KV_WIDTH) // HEAD_DIM
    ctx_blk = SEQ // CTX_LEN
    n_rows = SEQ if latent else CTX_LEN
    q_blk0 = 0 if latent else SEQ // tq
    kern = functools.partial(_attn_kernel, tq=tq, ck=ck, n_chunks=SEQ // ck if latent else 0)
    return pl.pallas_call(
        kern,
        grid=(ATT_KV_HEADS, n_rows // tq),
        in_specs=[pl.BlockSpec((tq, gw), lambda g, i: (q_blk0 + i, g)),
                  pl.BlockSpec((SEQ, HEAD_DIM), lambda g, i: (0, g)),
                  pl.BlockSpec((SEQ, HEAD_DIM), lambda g, i: (0, v_col + g)),
                  pl.BlockSpec((CTX_LEN, HEAD_DIM), lambda g, i: (ctx_blk, g)),
                  pl.BlockSpec((CTX_LEN, HEAD_DIM), lambda g, i: (ctx_blk, v_col + g))],
        out_specs=pl.BlockSpec((tq, gw), lambda g, i: (i, g)),
        out_shape=jax.ShapeDtypeStruct((n_rows, ATT_HEADS * HEAD_DIM), BF16),
        compiler_params=_cparams("parallel", "parallel"),
        name="gqa_attention",
    )(q, k, P, k, P)


def _gmlp_kernel(u_ref, v_ref, g_ref, b_ref, ws_ref, bs_ref, o_ref):
    u = _gelu_tanh(u_ref[...].astype(F32))
    v = _ln_rows(_gelu_tanh(v_ref[...].astype(F32)), g_ref[...], b_ref[...]).astype(BF16)
    for c in range(ROW_BLK // GM_CHUNK):
        rs = slice(c * GM_CHUNK, (c + 1) * GM_CHUNK)
        for g in range(GM_GROUPS):
            cs = slice(g * HEAD_DIM, (g + 1) * HEAD_DIM)
            s = jnp.dot(ws_ref[g], v[rs, cs], preferred_element_type=F32) + bs_ref[g]
            o_ref[rs, cs] = (u[rs, cs] * s).astype(o_ref.dtype)


def chunk_gmlp(P, ln_g, ln_b, w_s, b_s, n_rows):
    bs_rep = jnp.broadcast_to(b_s[:, :, None], (GM_GROUPS, GM_CHUNK, HEAD_DIM))
    BW = BRANCH_WIDTH
    return pl.pallas_call(
        _gmlp_kernel,
        grid=(n_rows // ROW_BLK,),
        in_specs=[pl.BlockSpec((ROW_BLK, BW), lambda i: (i, COL_GM_U)),
                  pl.BlockSpec((ROW_BLK, BW), lambda i: (i, COL_GM_V)),
                  pl.BlockSpec((1, BW), lambda i: (0, 0)),
                  pl.BlockSpec((1, BW), lambda i: (0, 0)),
                  pl.BlockSpec((GM_GROUPS, GM_CHUNK, GM_CHUNK), lambda i: (0, 0, 0)),
                  pl.BlockSpec((GM_GROUPS, GM_CHUNK, HEAD_DIM), lambda i: (0, 0, 0))],
        out_specs=pl.BlockSpec((ROW_BLK, BW), lambda i: (i, 0)),
        out_shape=jax.ShapeDtypeStruct((n_rows, BW), BF16),
        compiler_params=_cparams("parallel"),
    )(P, P, ln_g.reshape(1, BW), ln_b.reshape(1, BW), w_s.astype(BF16), bs_rep)


CONV_HALO = 16
CONV_SUB = 32


def _conv_kernel(a_ref, b_ref, ap_ref, bp_ref, an_ref, bn_ref, w_ref, cb_ref, g_ref, beta_ref, o_ref, gbuf):
    i = pl.program_id(0)

    def glu(a, b):
        return a.astype(F32) * _sigmoid(b.astype(F32))

    prev_ok = jnp.logical_and(i != 0, i != LAT_ROW_BLKS)
    next_ok = jnp.logical_and(i != LAT_ROW_BLKS - 1, i != N_ROW_BLKS - 1)
    gbuf[0:CONV_HALO, :] = jnp.where(prev_ok, glu(ap_ref[...], bp_ref[...]), 0.0)
    gbuf[CONV_HALO:CONV_HALO + ROW_BLK, :] = glu(a_ref[...], b_ref[...])
    gbuf[CONV_HALO + ROW_BLK:, :] = jnp.where(next_ok, glu(an_ref[...], bn_ref[...]), 0.0)
    first = CONV_HALO - CONV_K // 2
    for r in range(ROW_BLK // CONV_SUB):
        base = r * CONV_SUB
        acc = jnp.zeros((CONV_SUB, BRANCH_WIDTH), F32)
        for k in range(CONV_K):
            acc = acc + gbuf[base + first + k:base + first + k + CONV_SUB, :] * w_ref[k:k + 1, :]
        y = _ln_rows(acc + cb_ref[...], g_ref[...], beta_ref[...])
        o_ref[base:base + CONV_SUB, :] = (y * _sigmoid(y)).astype(o_ref.dtype)


def conformer_conv(P, conv_w, conv_b, ln_g, ln_b, n_rows):
    BW = BRANCH_WIDTH
    hpb = ROW_BLK // CONV_HALO
    last_halo = N_TOK // CONV_HALO - 1

    def prev_map(col):
        return lambda i: (jnp.maximum(i * hpb - 1, 0), col)

    def next_map(col):
        return lambda i: (jnp.minimum((i + 1) * hpb, last_halo), col)

    vec = pl.BlockSpec((1, BW), lambda i: (0, 0))
    return pl.pallas_call(
        _conv_kernel,
        grid=(n_rows // ROW_BLK,),
        in_specs=[pl.BlockSpec((ROW_BLK, BW), lambda i: (i, COL_CV_A)),
                  pl.BlockSpec((ROW_BLK, BW), lambda i: (i, COL_CV_B)),
                  pl.BlockSpec((CONV_HALO, BW), prev_map(COL_CV_A)),
                  pl.BlockSpec((CONV_HALO, BW), prev_map(COL_CV_B)),
                  pl.BlockSpec((CONV_HALO, BW), next_map(COL_CV_A)),
                  pl.BlockSpec((CONV_HALO, BW), next_map(COL_CV_B)),
                  pl.BlockSpec((CONV_K + 1, BW), lambda i: (0, 0)),
                  vec, vec, vec],
        out_specs=pl.BlockSpec((ROW_BLK, BW), lambda i: (i, 0)),
        out_shape=jax.ShapeDtypeStruct((n_rows, BW), BF16),
        scratch_shapes=[pltpu.VMEM((ROW_BLK + 2 * CONV_HALO, BW), F32)],
        compiler_params=_cparams("parallel"),
    )(P, P, P, P, P, P, jnp.pad(conv_w, ((0, 1), (0, 0))), conv_b.reshape(1, BW),
      ln_g.reshape(1, BW), ln_b.reshape(1, BW))


MW_TILE = 512
MW_SHIFT = MERGE_START % MW_TILE


def _merge_weights_kernel(a_ref, b_ref, o_ref):
    o_ref[0:MW_TILE - MW_SHIFT, :] = a_ref[MW_SHIFT:, :].astype(o_ref.dtype)
    o_ref[MW_TILE - MW_SHIFT:, :] = b_ref[...].astype(o_ref.dtype)


def merge_weights(w_in_t, layer):
    D = D_MODEL
    a0 = (MERGE_START - MW_SHIFT) // MW_TILE
    b0 = (MERGE_START - MW_SHIFT) // MW_SHIFT
    return pl.pallas_call(
        _merge_weights_kernel,
        grid=(N_BRANCH * D // MW_TILE,),
        in_specs=[pl.BlockSpec((None, MW_TILE, D), lambda j: (layer, a0 + j, 0)),
                  pl.BlockSpec((None, MW_SHIFT, D), lambda j: (layer, b0 + (j + 1) * (MW_TILE // MW_SHIFT), 0))],
        out_specs=pl.BlockSpec((MW_TILE, D), lambda j: (j, 0)),
        out_shape=jax.ShapeDtypeStruct((N_BRANCH * D, D), BF16),
        compiler_params=_cparams("parallel"),
        name="merge_weights",
    )(w_in_t, w_in_t)


def _merge_kernel(h_ref, b0, b1, b2, b3, m0, m1, m2, m3, w0, w1, w2, w3, o_ref):
    h = h_ref[...]
    acc = None
    for br, wm, wb in ((b0, m0, w0), (b1, m1, w1), (b2, m2, w2), (b3, m3, w3)):
        gate = _sigmoid(_dot_nt(h, wm[...]))
        t = gate * jnp.dot(br[...], wb[...], preferred_element_type=F32)
        acc = t if acc is None else acc + t
    o_ref[...] = acc.astype(o_ref.dtype)


def merge_branches(h_all, branches, w_merge, w_branch, n_rows, tm, tn=256):
    D = D_MODEL
    BW = BRANCH_WIDTH
    nj = D // tn
    one = pl.Buffered(1)
    in_specs = [pl.BlockSpec((tm, D), lambda i, j: (i, 0), pipeline_mode=one)]
    in_specs += [pl.BlockSpec((tm, BW), lambda i, j: (i, 0), pipeline_mode=one) for _ in range(N_BRANCH)]
    in_specs += [pl.BlockSpec((tn, D), functools.partial(lambda i, j, b: (b * nj + j, 0), b=b))
                 for b in range(N_BRANCH)]
    in_specs += [pl.BlockSpec((None, BW, tn), functools.partial(lambda i, j, b: (b, 0, j), b=b))
                 for b in range(N_BRANCH)]
    return pl.pallas_call(
        _merge_kernel,
        grid=(n_rows // tm, nj),
        in_specs=in_specs,
        out_specs=pl.BlockSpec((tm, tn), lambda i, j: (i, j)),
        out_shape=jax.ShapeDtypeStruct((n_rows, D), BF16),
        compiler_params=_cparams("parallel", "arbitrary"),
        name="gated_merge",
    )(h_all, *branches, *([w_merge] * N_BRANCH), *([w_branch] * N_BRANCH))


def _out_ln_kernel(y_ref, w_ref, x_ref, g1_ref, lg_ref, lb_ref, sh_ref, sc_ref, x1_ref, hm_ref, *, nk, tm):
    k = pl.program_id(1)
    part = jnp.dot(y_ref[...], w_ref[...], preferred_element_type=F32)

    @pl.when(k == 0)
    def _():
        x1_ref[...] = part

    @pl.when(k > 0)
    def _():
        x1_ref[...] += part

    @pl.when(k == nk - 1)
    def _():
        sub = next(s for s in (64, 48, 32, 16) if tm % s == 0)

        def chunk(r, carry):
            r0 = pl.multiple_of(r * sub, 16)
            rows = pl.ds(r0, sub)
            is_ctx = pl.program_id(0) * tm + r0 + lax.broadcasted_iota(jnp.int32, (sub, 1), 0) >= SEQ
            z = ALPHA * x_ref[rows, :] + _pick(g1_ref, is_ctx) * x1_ref[rows, :]
            x1 = _ln_rows(z, lg_ref[...], lb_ref[...])
            x1_ref[rows, :] = x1
            hm_ref[rows, :] = (x1 * (1.0 + _pick(sc_ref, is_ctx)) + _pick(sh_ref, is_ctx)).astype(hm_ref.dtype)
            return carry

        lax.fori_loop(0, tm // sub, chunk, 0)


def out_proj_ln(y, w_out, x_all, mod_l, ln_g, ln_b, n_rows, tm, tk=512):
    D = D_MODEL
    nk = D // tk
    kern = functools.partial(_out_ln_kernel, nk=nk, tm=tm)
    vec = pl.BlockSpec((1, D), lambda i, k: (0, 0))

    def modspec(col):
        return pl.BlockSpec((16, D), lambda i, k: (0, col))

    return pl.pallas_call(
        kern,
        grid=(n_rows // tm, nk),
        in_specs=[pl.BlockSpec((tm, tk), lambda i, k: (i, k)),
                  pl.BlockSpec((tk, D), lambda i, k: (k, 0)),
                  pl.BlockSpec((tm, D), lambda i, k: (i, 0), pipeline_mode=pl.Buffered(1)),
                  modspec(2), vec, vec, modspec(3), modspec(4)],
        out_specs=[pl.BlockSpec((tm, D), lambda i, k: (i, 0)),
                   pl.BlockSpec((tm, D), lambda i, k: (i, 0))],
        out_shape=[jax.ShapeDtypeStruct((n_rows, D), F32),
                   jax.ShapeDtypeStruct((n_rows, D), BF16)],
        compiler_params=_cparams("parallel", "arbitrary"),
        name="out_proj_ln1",
    )(y, w_out, x_all, mod_l, ln_g.reshape(1, D), ln_b.reshape(1, D), mod_l, mod_l)


def _ln2_kernel(x_ref, y_ref, g2_ref, lg_ref, lb_ref, sh_ref, sc_ref, x2_ref, *h_ref, blk_ctx_from):
    is_ctx = pl.program_id(0) >= blk_ctx_from
    z = ALPHA * x_ref[...] + _pick(g2_ref, is_ctx) * y_ref[...]
    x2 = _ln_rows(z, lg_ref[...], lb_ref[...])
    x2_ref[...] = x2
    if h_ref:
        h_ref[0][...] = (x2 * (1.0 + _pick(sc_ref, is_ctx)) + _pick(sh_ref, is_ctx)).astype(BF16)


def ffn_residual_ln(x1, y_moe, mod_l, ln_g, ln_b, mod_next, n_rows):
    D = D_MODEL
    want_h = mod_next is not None
    nxt = mod_next if want_h else mod_l
    kern = functools.partial(_ln2_kernel, blk_ctx_from=LAT_ROW_BLKS)
    row = pl.BlockSpec((ROW_BLK, D), lambda i: (i, 0))
    vec = pl.BlockSpec((1, D), lambda i: (0, 0))
    out_specs = [row, row] if want_h else [row]
    out_shape = [jax.ShapeDtypeStruct((n_rows, D), F32)]
    if want_h:
        out_shape.append(jax.ShapeDtypeStruct((n_rows, D), BF16))
    return pl.pallas_call(
        kern,
        grid=(n_rows // ROW_BLK,),
        in_specs=[row, row, pl.BlockSpec((16, D), lambda i: (0, 5)), vec, vec,
                  pl.BlockSpec((16, D), lambda i: (0, 0)), pl.BlockSpec((16, D), lambda i: (0, 1))],
        out_specs=out_specs,
        out_shape=out_shape,
        compiler_params=_cparams("parallel"),
    )(x1, y_moe, mod_l, ln_g.reshape(1, D), ln_b.reshape(1, D), nxt, nxt)


def _swiglu_kernel(x_ref, wg_ref, wu_ref, o_ref):
    x = x_ref[...]
    g = jnp.dot(x, wg_ref[...].astype(BF16), preferred_element_type=F32)
    u = jnp.dot(x, wu_ref[...].astype(BF16), preferred_element_type=F32)
    o_ref[...] = (g * _sigmoid(g) * u).astype(o_ref.dtype)


def expert_swiglu(xe, w_gate, w_up, layer, tn=256):
    E, M, K = xe.shape
    N = w_gate.shape[-1]
    wspec = pl.BlockSpec((None, None, K, tn), lambda e, j: (layer, e, 0, j))
    return pl.pallas_call(
        _swiglu_kernel,
        grid=(E, N // tn),
        in_specs=[pl.BlockSpec((None, M, K), lambda e, j: (e, 0, 0)), wspec, wspec],
        out_specs=pl.BlockSpec((None, M, tn), lambda e, j: (e, 0, j)),
        out_shape=jax.ShapeDtypeStruct((E, M, N), BF16),
        compiler_params=_cparams("parallel", "parallel"),
    )(xe, w_gate, w_up)


def _down_kernel(h_ref, w_ref, g_ref, o_ref):
    y = jnp.dot(h_ref[...], w_ref[...].astype(BF16), preferred_element_type=F32)
    o_ref[...] = y * g_ref[...]


def expert_down(hid, w_down, gains, layer, tn=1024):
    E, M, K = hid.shape
    N = w_down.shape[-1]
    return pl.pallas_call(
        _down_kernel,
        grid=(E, N // tn),
        in_specs=[pl.BlockSpec((None, M, K), lambda e, j: (e, 0, 0)),
                  pl.BlockSpec((None, None, K, tn), lambda e, j: (layer, e, 0, j)),
                  pl.BlockSpec((None, M, 1), lambda e, j: (e, 0, 0))],
        out_specs=pl.BlockSpec((None, M, tn), lambda e, j: (e, 0, j)),
        out_shape=jax.ShapeDtypeStruct((E, M, N), F32),
        compiler_params=_cparams("parallel", "parallel"),
    )(hid, w_down, gains)


def _scatter_kernel(idx_ref, ye_ref, acc_in, acc_ref, buf, gsem, ssem, *, R, n_chunks):
    del acc_in
    c = pl.program_id(1)

    def gather(chunk, slot, r):
        t = idx_ref[0, chunk * R + r]
        return pltpu.make_async_copy(acc_ref.at[pl.ds(t, 1), :], buf.at[slot, pl.ds(r, 1), :], gsem.at[slot])

    def scatter(chunk, slot, r):
        t = idx_ref[0, chunk * R + r]
        return pltpu.make_async_copy(buf.at[slot, pl.ds(r, 1), :], acc_ref.at[pl.ds(t, 1), :], ssem.at[slot])

    def each_row(fn):
        def body(r, carry):
            fn(r)
            return carry
        lax.fori_loop(0, R, body, 0, unroll=8)

    slot = c % 2

    @pl.when(c == 0)
    def _():
        each_row(lambda r: gather(0, 0, r).start())

    each_row(lambda r: gather(c, slot, r).wait())

    @pl.when(c >= 1)
    def _():
        each_row(lambda r: scatter(c - 1, 1 - slot, r).wait())

    @pl.when(c + 1 < n_chunks)
    def _():
        each_row(lambda r: gather(c + 1, 1 - slot, r).start())

    buf[slot] = buf[slot] + ye_ref[...]
    each_row(lambda r: scatter(c, slot, r).start())

    @pl.when(c == n_chunks - 1)
    def _():
        each_row(lambda r: scatter(c, slot, r).wait())


def scatter_add_rows(idx, ye, n_rows):
    E, M, D = ye.shape
    R = next(r for r in (264, 256, 160, 128, 64, 32) if M % r == 0)
    n_chunks = M // R
    kern = functools.partial(_scatter_kernel, R=R, n_chunks=n_chunks)
    return pl.pallas_call(
        kern,
        grid=(E, n_chunks),
        in_specs=[pl.BlockSpec((None, 1, M), lambda e, c: (e, 0, 0), memory_space=pltpu.SMEM),
                  pl.BlockSpec((None, R, D), lambda e, c: (e, c, 0)),
                  pl.BlockSpec(memory_space=pl.ANY)],
        out_specs=pl.BlockSpec(memory_space=pl.ANY),
        out_shape=jax.ShapeDtypeStruct((n_rows, D), F32),
        scratch_shapes=[pltpu.VMEM((2, R, D), F32), pltpu.SemaphoreType.DMA((2,)), pltpu.SemaphoreType.DMA((2,))],
        input_output_aliases={2: 0},
        compiler_params=_cparams("arbitrary", "arbitrary"),
        name="scatter_add_rows",
    )(idx.reshape(E, 1, M), ye, jnp.zeros((n_rows, D), F32))


N_CHUNKS = N_TOK // ML_CHUNK
LAT_CHUNKS = SEQ // ML_CHUNK
CTX_CHUNKS = CTX_LEN // ML_CHUNK


def _mlstm_kernel(q_ref, k_ref, v_ref, og_ref, g_ref, ng_ref, out_ref, acc_ref, c_ref, n_ref, m_ref, *, hb, out_chunks):
    L, d = ML_CHUNK, ML_HEAD_DIM
    row = lax.broadcasted_iota(jnp.int32, (L, L), 0)
    col = lax.broadcasted_iota(jnp.int32, (L, L), 1)
    eye = row == col
    masks = (col <= row, col >= row)
    lasts = (L - 1, 0)
    k_scale = ML_HEAD_DIM ** -0.5

    acc_ref[...] = jnp.zeros(acc_ref.shape, F32)
    c_ref[...] = jnp.zeros(c_ref.shape, F32)
    n_ref[...] = jnp.zeros(n_ref.shape, F32)
    m_ref[...] = jnp.zeros(m_ref.shape, F32)

    def col_of(r):
        return jnp.sum(jnp.where(eye, r, 0.0), axis=1, keepdims=True)

    def step(t, carry):
        chunk = (jnp.where(t < CTX_CHUNKS, t + LAT_CHUNKS, t - CTX_CHUNKS), N_CHUNKS - 1 - t)
        for j in range(hb):
            hs = slice(j * d, (j + 1) * d)
            for dr in range(2):
                si = 2 * j + dr
                off = pl.multiple_of(chunk[dr] * L, L)
                rows = pl.ds(off, L)
                q = q_ref[rows, hs]
                kf = k_ref[rows, hs].astype(F32) * k_scale
                v = v_ref[rows, hs]
                b_row = g_ref[j, 2 * dr:2 * dr + 1, rows]
                a_row = g_ref[j, 2 * dr + 1:2 * dr + 2, rows]
                last = lasts[dr]
                m = m_ref[si][0:1, 0:1]
                n = n_ref[si][0:1, :]
                C = c_ref[si]
                b_col = col_of(b_row)
                a_col = col_of(a_row)
                dlog = jnp.where(masks[dr], b_col + a_row, -jnp.inf)
                inter = b_col + m
                mj = jnp.maximum(inter, jnp.max(dlog, axis=1, keepdims=True))
                w_inter = jnp.exp(inter - mj)
                qk = lax.dot_general(q, kf.astype(BF16), (((1,), (1,)), ((), ())), preferred_element_type=F32)
                s = qk * jnp.exp(dlog - mj)
                num = (w_inter * jnp.dot(q, C.astype(BF16), preferred_element_type=F32)
                       + jnp.dot(s.astype(BF16), v, preferred_element_type=F32))
                qn = jnp.sum(q.astype(F32) * n, axis=1, keepdims=True)
                den = w_inter * qn + jnp.sum(s, axis=1, keepdims=True)
                h = num / jnp.maximum(jnp.abs(den), jnp.exp(-mj))
                acc_ref[rows, hs] += h
                m_new = mj[last:last + 1, :]
                b_end = b_row[:, last:last + 1]
                w_c = jnp.exp(b_end + m - m_new)
                kw = kf * jnp.exp(b_end + a_col - m_new)
                c_ref[si] = w_c * C + lax.dot_general(kw.astype(BF16), v, (((0,), (0,)), ((), ())),
                                                      preferred_element_type=F32)
                n_ref[si] = jnp.broadcast_to(w_c * n + jnp.sum(kw, axis=0, keepdims=True), (8, d))
                m_ref[si] = jnp.broadcast_to(m_new, (8, LANES))
        return carry

    lax.fori_loop(0, N_CHUNKS, step, 0)

    def finish(c, carry):
        rows = pl.ds(pl.multiple_of(c * L, L), L)
        for j in range(hb):
            hs = slice(j * d, (j + 1) * d)
            hh = acc_ref[rows, hs]
            hc = hh - jnp.mean(hh, axis=1, keepdims=True)
            hn = hc * lax.rsqrt(jnp.mean(hc * hc, axis=1, keepdims=True) + EPS) * ng_ref[j]
            out_ref[rows, hs] = (_sigmoid(og_ref[rows, hs].astype(F32)) * hn).astype(out_ref.dtype)
        return carry

    lax.fori_loop(0, out_chunks, finish, 0)


def mlstm(P, G, norm_g, n_rows, hb=2):
    w = hb * ML_HEAD_DIM
    per = BRANCH_WIDTH // w

    def colspec(col):
        return pl.BlockSpec((N_TOK, w), lambda i: (0, col * per + i))

    kern = functools.partial(_mlstm_kernel, hb=hb, out_chunks=n_rows // ML_CHUNK)
    return pl.pallas_call(
        kern,
        grid=(ML_HEADS // hb,),
        in_specs=[colspec(COL_ML_Q), colspec(COL_ML_K), colspec(COL_ML_V),
                  pl.BlockSpec((n_rows, w), lambda i: (0, COL_ML_O * per + i)),
                  pl.BlockSpec((hb, 4, N_TOK), lambda i: (i, 0, 0)),
                  pl.BlockSpec((hb, 1, ML_HEAD_DIM), lambda i: (i, 0, 0))],
        out_specs=pl.BlockSpec((n_rows, w), lambda i: (0, i)),
        out_shape=jax.ShapeDtypeStruct((n_rows, BRANCH_WIDTH), BF16),
        scratch_shapes=[pltpu.VMEM((N_TOK, w), F32),
                        pltpu.VMEM((2 * hb, ML_HEAD_DIM, ML_HEAD_DIM), F32),
                        pltpu.VMEM((2 * hb, 8, ML_HEAD_DIM), F32),
                        pltpu.VMEM((2 * hb, 8, LANES), F32)],
        compiler_params=_cparams("parallel"),
        name="mlstm_scan",
    )(P, P, P, P, G, norm_g.reshape(ML_HEADS, 1, ML_HEAD_DIM))


def _norm_stats(x):
    xc = x - jnp.mean(x, axis=-1, keepdims=True)
    return xc * lax.rsqrt(jnp.mean(xc * xc, axis=-1, keepdims=True) + EPS)


def _rope_tables():
    rows = SEQ // GRID_W
    row = jnp.repeat(jnp.arange(rows), GRID_W).astype(F32)
    col = (jnp.arange(SEQ) % GRID_W).astype(F32)
    n_freq = HEAD_DIM // 4
    inv = ROPE_THETA ** (-jnp.arange(n_freq, dtype=F32) / n_freq)
    ang_r = row[:, None] * inv[None, :]
    ang_c = col[:, None] * inv[None, :]
    cos = jnp.concatenate([jnp.cos(ang_r), jnp.cos(ang_r), jnp.cos(ang_c), jnp.cos(ang_c)], axis=1)
    sin = jnp.concatenate([-jnp.sin(ang_r), jnp.sin(ang_r), -jnp.sin(ang_c), jnp.sin(ang_c)], axis=1)
    cos = jnp.concatenate([cos, jnp.ones((CTX_LEN, HEAD_DIM), F32)], axis=0)
    sin = jnp.concatenate([sin, jnp.zeros((CTX_LEN, HEAD_DIM), F32)], axis=0)
    return cos, sin


def _mlstm_chunk_scan(q, k, v, a_row, b_row, state, reverse):
    H, T, d = q.shape
    nc = T // ML_CHUNK

    def chunks(x):
        return jnp.moveaxis(x.reshape(H, nc, ML_CHUNK, *x.shape[2:]), 1, 0)

    tri = jnp.tril(jnp.ones((ML_CHUNK, ML_CHUNK), dtype=bool))
    mask = tri.T if reverse else tri
    last = 0 if reverse else ML_CHUNK - 1

    def step(carry, inp):
        C, n, m = carry
        qc, kc, vc, ac, bc = inp
        inter = bc + m[..., None]
        dlog = jnp.where(mask, bc[..., :, None] + ac[..., None, :], -jnp.inf)
        mj = jnp.maximum(inter, jnp.max(dlog, axis=-1))
        w_inter = jnp.exp(inter - mj)
        s = jnp.einsum('hjd,hsd->hjs', qc, kc) * jnp.exp(dlog - mj[..., None])
        num = (w_inter[..., None] * jnp.einsum('hjd,hde->hje', qc, C)
               + jnp.einsum('hjs,hse->hje', s, vc))
        den = w_inter * jnp.einsum('hjd,hd->hj', qc, n) + jnp.sum(s, axis=-1)
        h = num / jnp.maximum(jnp.abs(den), jnp.exp(-mj))[..., None]
        m_new = mj[..., last]
        b_end = bc[..., last]
        w_c = jnp.exp(b_end + m - m_new)
        w_s = jnp.exp(b_end[..., None] + ac - m_new[..., None])
        C_new = w_c[..., None, None] * C + jnp.einsum('hs,hsd,hse->hde', w_s, kc, vc)
        n_new = w_c[..., None] * n + jnp.einsum('hs,hsd->hd', w_s, kc)
        return (C_new, n_new, m_new), h

    final, h = lax.scan(step, state, (chunks(q), chunks(k), chunks(v), chunks(a_row), chunks(b_row)),
                        reverse=reverse)
    return jnp.moveaxis(h, 0, 1).reshape(H, T, d), final


def _mlstm(P, G, norm_g, rows_out):
    BW = BRANCH_WIDTH

    def heads(col, rows):
        a = P[rows, col * BW:(col + 1) * BW].astype(F32)
        return jnp.swapaxes(a.reshape(-1, ML_HEADS, ML_HEAD_DIM), 0, 1)

    H = ML_HEADS
    zero = (jnp.zeros((H, ML_HEAD_DIM, ML_HEAD_DIM), F32), jnp.zeros((H, ML_HEAD_DIM), F32), jnp.zeros((H,), F32))
    outs = {}
    st_f, st_b = zero, zero
    for name, rows in (('ctx', slice(SEQ, N_TOK)), ('lat', slice(0, SEQ))):
        q = heads(COL_ML_Q, rows)
        k = heads(COL_ML_K, rows) * (ML_HEAD_DIM ** -0.5)
        v = heads(COL_ML_V, rows)
        h_f, st_f = _mlstm_chunk_scan(q, k, v, G[H:2 * H, rows], G[0:H, rows], st_f, False)
        h_b, st_b = _mlstm_chunk_scan(q, k, v, G[3 * H:4 * H, rows], G[2 * H:3 * H, rows], st_b, True)
        outs[name] = h_f + h_b
    h = jnp.concatenate([outs['lat'], outs['ctx']], axis=1)[:, :rows_out]
    hn = _norm_stats(jnp.swapaxes(h, 0, 1)) * norm_g.reshape(H, ML_HEAD_DIM)
    o_pre = P[:rows_out, COL_ML_O * BW:(COL_ML_O + 1) * BW].astype(F32)
    return (jax.nn.sigmoid(o_pre) * hn.reshape(rows_out, BW)).astype(BF16)


def _expert_choice_route(hm, w_router, cap):
    logits = jnp.dot(hm, w_router.astype(BF16), preferred_element_type=F32)
    aff = jax.nn.softmax(logits, axis=-1)
    return lax.top_k(aff.T, cap)


def kernel(x, c, ctx, c_ctx, w_mod, b_mod, w_in, att_q_gain, att_k_gain, gm_ln_g, gm_ln_b,
           gm_w_s, gm_b_s, conv_w, conv_b, conv_ln_g, conv_ln_b, ml_gate_bias, ml_norm_g,
           w_branch, w_out, ln1_g, ln1_b, w_router, w_gate, w_up, w_down, ln2_g, ln2_b):
    cos_t, sin_t = _rope_tables()
    w_in_t = jnp.swapaxes(w_in, 1, 2)
    cc = jnp.zeros((16, D_MODEL), F32).at[0].set(c[0]).at[1].set(c_ctx)
    mods = modulation(cc, w_mod, b_mod)
    x_all = jnp.concatenate([x[0], ctx[0]], axis=0)
    h_all = modulate(x_all, mods[0])
    for l in range(DEPTH):
        last = l == DEPTH - 1
        n_rows = SEQ if last else N_TOK
        tm = (SEQ if last else N_TOK) // 8
        mod_l = mods[l]

        P = branch_projection(h_all, w_in_t, l)
        gbias = jnp.broadcast_to(ml_gate_bias[l].reshape(GATE_COLS, 1), (GATE_COLS, LANES))
        G = mlstm_gate_rows(h_all, w_in_t, l, gbias)
        q, k = qk_prepare(P, cos_t, sin_t, att_q_gain[l], att_k_gain[l])

        br_a = chunk_gmlp(P, gm_ln_g[l], gm_ln_b[l], gm_w_s[l], gm_b_s[l], n_rows)
        br_b = attention(q, k, P, True)
        if not last:
            br_b = jnp.concatenate([br_b, attention(q, k, P, False)], axis=0)
        br_c = conformer_conv(P, conv_w[l], conv_b[l], conv_ln_g[l], conv_ln_b[l], n_rows)
        G_heads = G.reshape(ML_N_GATES, ML_HEADS, N_TOK).transpose(1, 0, 2)
        br_d = mlstm(P, G_heads, ml_norm_g[l], n_rows)

        w_merge = merge_weights(w_in_t, l)
        y = merge_branches(h_all, [br_a, br_b, br_c, br_d], w_merge, w_branch[l].astype(BF16), n_rows, tm)
        x1, hm = out_proj_ln(y, w_out[l].astype(BF16), x_all, mod_l, ln1_g[l], ln1_b[l], n_rows, tm // 2)

        cap_l = CAPACITY_FACTOR * SEQ // N_EXPERTS
        g_l, idx_l = _expert_choice_route(hm[:SEQ], w_router[l], cap_l)
        gains, idx = g_l, idx_l
        if not last:
            g_c, idx_c = _expert_choice_route(hm[SEQ:], w_router[l], CAPACITY_FACTOR * CTX_LEN // N_EXPERTS)
            gains = jnp.concatenate([g_l, g_c], axis=1)
            idx = jnp.concatenate([idx_l, idx_c + SEQ], axis=1)
        xe = hm[idx]
        hid = expert_swiglu(xe, w_gate, w_up, l)
        ye = expert_down(hid, w_down, gains[..., None], l)
        y_moe = scatter_add_rows(idx, ye, n_rows)
        res = ffn_residual_ln(x1, y_moe, mod_l, ln2_g[l], ln2_b[l], None if last else mods[l + 1], n_rows)
        if not last:
            x_all, h_all = res
        else:
            x_all = res[0]
    return x_all[None]
```

```python
import functools

import jax
import jax.numpy as jnp
from jax import lax
from jax.experimental import pallas as pl
from jax.experimental.pallas import tpu as pltpu

F32 = jnp.float32
BF16 = jnp.bfloat16

D_MODEL = 4096
SEQ = 8192
DEPTH = 2
GRID_W = 64
CTX_LEN = 256
N_BRANCH = 4
BRANCH_WIDTH = D_MODEL // N_BRANCH
HEAD_DIM = 128
GM_CHUNK = 128
GM_GROUPS = BRANCH_WIDTH // HEAD_DIM
ATT_HEADS = BRANCH_WIDTH // HEAD_DIM
ATT_KV_HEADS = 2
ATT_REP = ATT_HEADS // ATT_KV_HEADS
KV_WIDTH = ATT_KV_HEADS * HEAD_DIM
ROPE_THETA = 10000.0
CONV_K = 31
ML_HEADS = BRANCH_WIDTH // HEAD_DIM
ML_HEAD_DIM = HEAD_DIM
ML_CHUNK = 128
ML_N_GATES = 4
N_EXPERTS = 16
EXPERT_FF = D_MODEL // 4
CAPACITY_FACTOR = 2
ALPHA = (2 * DEPTH) ** 0.25
EPS = 1e-6
N_MOD = 6

N_TOK = SEQ + CTX_LEN
SEG1 = 9 * BRANCH_WIDTH + 2 * KV_WIDTH
GATE_COLS = ML_N_GATES * ML_HEADS
MERGE_START = SEG1 + GATE_COLS

V7X_VMEM_LIMIT = 56 * 1024 * 1024
LANES = 128

GATE_TOK_TILE = next(t for t in (1408, 1280, 1024, 768, 512, 384, 256, 128) if N_TOK % t == 0)
ROW_BLK = 256
N_ROW_BLKS = N_TOK // ROW_BLK
LAT_ROW_BLKS = SEQ // ROW_BLK

P_TILE = 512
P_KV_TILE_IN = (3 * BRANCH_WIDTH) // P_TILE
P_N_TILES = SEG1 // P_TILE
COL_GM_U, COL_GM_V, COL_ATT_Q = 0, 1, 2
COL_CV_A, COL_CV_B, COL_ML_Q, COL_ML_K, COL_ML_V, COL_ML_O = 3, 4, 5, 6, 7, 8
P_K_COL = 9 * BRANCH_WIDTH


def _cparams(*sem):
    return pltpu.CompilerParams(dimension_semantics=sem, vmem_limit_bytes=V7X_VMEM_LIMIT)


def _sigmoid(x):
    return 1.0 / (1.0 + jnp.exp(-x))


def _gelu_tanh(x):
    return 0.5 * x * (1.0 + jnp.tanh(0.7978845608028654 * (x + 0.044715 * (x * x * x))))


def _ln_rows(z, g, b):
    mu = jnp.mean(z, axis=-1, keepdims=True)
    zc = z - mu
    var = jnp.mean(zc * zc, axis=-1, keepdims=True)
    return zc * lax.rsqrt(var + EPS) * g + b


def _dot_nt(a, b_t):
    return lax.dot_general(a, b_t, (((1,), (1,)), ((), ())), preferred_element_type=F32)


def _pick(ref, is_ctx):
    return jnp.where(is_ctx, ref[1:2, :], ref[0:1, :])


def _mod_kernel(c_ref, w_ref, b_ref, o_ref):
    c = c_ref[...]
    a = (c * _sigmoid(c)).astype(BF16)
    o_ref[...] = jnp.dot(a, w_ref[...].astype(BF16), preferred_element_type=F32) + b_ref[...]


def modulation(cc, w_mod, b_mod, tn=1024):
    L, D, N = w_mod.shape
    return pl.pallas_call(
        _mod_kernel,
        grid=(L, N // tn),
        in_specs=[pl.BlockSpec((16, D), lambda l, j: (0, 0)),
                  pl.BlockSpec((None, D, tn), lambda l, j: (l, 0, j)),
                  pl.BlockSpec((None, 1, tn), lambda l, j: (l, 0, j))],
        out_specs=pl.BlockSpec((None, 16, tn), lambda l, j: (l, 0, j)),
        out_shape=jax.ShapeDtypeStruct((L, 16, N), F32),
        compiler_params=_cparams("parallel", "parallel"),
    )(cc, w_mod, b_mod.reshape(L, 1, N))


def _modulate_kernel(x_ref, sh_ref, sc_ref, o_ref):
    is_ctx = pl.program_id(0) >= LAT_ROW_BLKS
    o_ref[...] = (x_ref[...] * (1.0 + _pick(sc_ref, is_ctx)) + _pick(sh_ref, is_ctx)).astype(o_ref.dtype)


def modulate(x_all, mod_l):
    D = D_MODEL
    return pl.pallas_call(
        _modulate_kernel,
        grid=(N_ROW_BLKS,),
        in_specs=[pl.BlockSpec((ROW_BLK, D), lambda i: (i, 0)),
                  pl.BlockSpec((16, D), lambda i: (0, 0)),
                  pl.BlockSpec((16, D), lambda i: (0, 1))],
        out_specs=pl.BlockSpec((ROW_BLK, D), lambda i: (i, 0)),
        out_shape=jax.ShapeDtypeStruct((N_TOK, D), BF16),
        compiler_params=_cparams("parallel"),
    )(x_all, mod_l, mod_l)


def _proj_kernel(a_ref, w_ref, o_ref, wb_ref):
    @pl.when(pl.program_id(1) == 0)
    def _():
        wb_ref[...] = w_ref[...].astype(BF16)

    o_ref[...] = _dot_nt(a_ref[...], wb_ref[...]).astype(o_ref.dtype)


def _p_out_tile(j):
    return jnp.where(j < P_KV_TILE_IN, j, jnp.where(j == P_KV_TILE_IN, P_N_TILES - 1, j - 1))


def branch_projection(h_all, w_in_t, layer, tm=N_TOK // 8):
    M, K = h_all.shape
    return pl.pallas_call(
        _proj_kernel,
        grid=(P_N_TILES, M // tm),
        in_specs=[pl.BlockSpec((tm, K), lambda j, i: (i, 0)),
                  pl.BlockSpec((None, P_TILE, K), lambda j, i: (layer, j, 0))],
        out_specs=pl.BlockSpec((tm, P_TILE), lambda j, i: (i, _p_out_tile(j))),
        out_shape=jax.ShapeDtypeStruct((M, SEG1), BF16),
        scratch_shapes=[pltpu.VMEM((P_TILE, K), BF16)],
        compiler_params=_cparams("parallel", "arbitrary"),
        name="branch_projection",
    )(h_all, w_in_t)


def _gates_kernel(w_ref, h_ref, bias_ref, o_ref, *, n_chunks):
    z = _dot_nt(w_ref[...].astype(BF16), h_ref[...])
    z = z + bias_ref[:, 0:1]
    lf = jnp.minimum(z, 0.0) - jnp.log1p(jnp.exp(-jnp.abs(z)))
    row = lax.broadcasted_iota(jnp.int32, (ML_CHUNK, ML_CHUNK), 0)
    col = lax.broadcasted_iota(jnp.int32, (ML_CHUNK, ML_CHUNK), 1)
    upper = (row <= col).astype(F32)
    lower = (row >= col).astype(F32)
    H = ML_HEADS
    for c in range(n_chunks):
        sl = slice(c * ML_CHUNK, (c + 1) * ML_CHUNK)
        b_f = jnp.dot(lf[H:2 * H, sl], upper, preferred_element_type=F32, precision=lax.Precision.HIGHEST)
        b_b = jnp.dot(lf[3 * H:4 * H, sl], lower, preferred_element_type=F32, precision=lax.Precision.HIGHEST)
        o_ref[:, sl] = jnp.concatenate([b_f, z[0:H, sl] - b_f, b_b, z[2 * H:3 * H, sl] - b_b], axis=0)


def mlstm_gate_rows(h_all, w_in_t, layer, bias, tt=GATE_TOK_TILE):
    M, K = h_all.shape
    kern = functools.partial(_gates_kernel, n_chunks=tt // ML_CHUNK)
    return pl.pallas_call(
        kern,
        grid=(M // tt,),
        in_specs=[pl.BlockSpec((None, GATE_COLS, K), lambda i: (layer, SEG1 // GATE_COLS, 0)),
                  pl.BlockSpec((tt, K), lambda i: (i, 0)),
                  pl.BlockSpec((GATE_COLS, LANES), lambda i: (0, 0))],
        out_specs=pl.BlockSpec((GATE_COLS, tt), lambda i: (0, i)),
        out_shape=jax.ShapeDtypeStruct((GATE_COLS, M), F32),
        compiler_params=_cparams("parallel"),
    )(w_in_t, h_all, bias)


def _rope_head(x, cos, sin, first_half):
    partner = jnp.where(first_half, pltpu.roll(x, 96, 1), pltpu.roll(x, 32, 1))
    return x * cos + partner * sin


def _qk_kernel(q_ref, kv_ref, cos_ref, sin_ref, qg_ref, kg_ref, qo_ref, ko_ref):
    cos = cos_ref[...]
    sin = sin_ref[...]
    lane = lax.broadcasted_iota(jnp.int32, (ROW_BLK, HEAD_DIM), 1)
    first_half = jnp.bitwise_and(lane, 63) < 32

    def prep(x, gain, scale):
        x = x.astype(F32)
        y = x * lax.rsqrt(jnp.mean(x * x, axis=-1, keepdims=True) + EPS) * gain
        return (_rope_head(y, cos, sin, first_half) * scale).astype(BF16)

    for h in range(ATT_HEADS):
        sl = slice(h * HEAD_DIM, (h + 1) * HEAD_DIM)
        qo_ref[:, sl] = prep(q_ref[:, sl], qg_ref[...], HEAD_DIM ** -0.5)
    for h in range(ATT_KV_HEADS):
        sl = slice(h * HEAD_DIM, (h + 1) * HEAD_DIM)
        ko_ref[:, sl] = prep(kv_ref[:, sl], kg_ref[...], 1.0)


def qk_prepare(P, cos_t, sin_t, q_gain, k_gain):
    return pl.pallas_call(
        _qk_kernel,
        grid=(N_ROW_BLKS,),
        in_specs=[pl.BlockSpec((ROW_BLK, BRANCH_WIDTH), lambda i: (i, COL_ATT_Q)),
                  pl.BlockSpec((ROW_BLK, P_TILE), lambda i: (i, P_N_TILES - 1)),
                  pl.BlockSpec((ROW_BLK, HEAD_DIM), lambda i: (i, 0)),
                  pl.BlockSpec((ROW_BLK, HEAD_DIM), lambda i: (i, 0)),
                  pl.BlockSpec((1, HEAD_DIM), lambda i: (0, 0)),
                  pl.BlockSpec((1, HEAD_DIM), lambda i: (0, 0))],
        out_specs=[pl.BlockSpec((ROW_BLK, BRANCH_WIDTH), lambda i: (i, 0)),
                   pl.BlockSpec((ROW_BLK, KV_WIDTH), lambda i: (i, 0))],
        out_shape=[jax.ShapeDtypeStruct((N_TOK, BRANCH_WIDTH), BF16),
                   jax.ShapeDtypeStruct((N_TOK, KV_WIDTH), BF16)],
        compiler_params=_cparams("parallel"),
    )(P, P, cos_t, sin_t, q_gain.reshape(1, HEAD_DIM), k_gain.reshape(1, HEAD_DIM))


def _attn_kernel(q_ref, k1_ref, v1_ref, k2_ref, v2_ref, o_ref, *, tq, ck, n_chunks):
    rows = ATT_REP * tq
    q = jnp.concatenate([q_ref[:, h * HEAD_DIM:(h + 1) * HEAD_DIM] for h in range(ATT_REP)], axis=0)

    def step(k, v, carry):
        m, l, acc = carry
        s = lax.dot_general(q, k, (((1,), (1,)), ((), ())), preferred_element_type=F32)
        m_new = jnp.maximum(m, jnp.max(s, axis=-1, keepdims=True))
        p = jnp.exp(s - m_new)
        a = jnp.exp(m - m_new)
        l = a * l + jnp.sum(p, axis=-1, keepdims=True)
        acc = a * acc + jnp.dot(p.astype(BF16), v, preferred_element_type=F32)
        return m_new, l, acc

    def body(c, carry):
        off = pl.multiple_of(c * ck, ck)
        return step(k1_ref[pl.ds(off, ck), :], v1_ref[pl.ds(off, ck), :], carry)

    carry = (jnp.full((rows, 1), -jnp.inf, F32), jnp.zeros((rows, 1), F32),
             jnp.zeros((rows, HEAD_DIM), F32))
    if n_chunks:
        carry = lax.fori_loop(0, n_chunks, body, carry, unroll=min(4, n_chunks))
    _, l, acc = step(k2_ref[...], v2_ref[...], carry)
    o = acc / l
    for h in range(ATT_REP):
        o_ref[:, h * HEAD_DIM:(h + 1) * HEAD_DIM] = o[h * tq:(h + 1) * tq].astype(o_ref.dtype)


def attention(q, k, P, latent, tq=128, ck=1024):
    gw = ATT_REP * HEAD_DIM
    v_col = (P_K_COL + KV_WIDTH) // HEAD_DIM
    ctx_blk = SEQ // CTX_LEN
    n_rows = SEQ if latent else CTX_LEN
    q_blk0 = 0 if latent else SEQ // tq
    kern = functools.partial(_attn_kernel, tq=tq, ck=ck, n_chunks=SEQ // ck if latent else 0)
    return pl.pallas_call(
        kern,
        grid=(ATT_KV_HEADS, n_rows // tq),
        in_specs=[pl.BlockSpec((tq, gw), lambda g, i: (q_blk0 + i, g)),
                  pl.BlockSpec((SEQ, HEAD_DIM), lambda g, i: (0, g)),
                  pl.BlockSpec((SEQ, HEAD_DIM), lambda g, i: (0, v_col + g)),
                  pl.BlockSpec((CTX_LEN, HEAD_DIM), lambda g, i: (ctx_blk, g)),
                  pl.BlockSpec((CTX_LEN, HEAD_DIM), lambda g, i: (ctx_blk, v_col + g))],
        out_specs=pl.BlockSpec((tq, gw), lambda g, i: (i, g)),
        out_shape=jax.ShapeDtypeStruct((n_rows, ATT_HEADS * HEAD_DIM), BF16),
        compiler_params=_cparams("parallel", "parallel"),
        name="gqa_attention",
    )(q, k, P, k, P)


def _gmlp_kernel(u_ref, v_ref, g_ref, b_ref, ws_ref, bs_ref, o_ref):
    u = _gelu_tanh(u_ref[...].astype(F32))
    v = _ln_rows(_gelu_tanh(v_ref[...].astype(F32)), g_ref[...], b_ref[...]).astype(BF16)
    for c in range(ROW_BLK // GM_CHUNK):
        rs = slice(c * GM_CHUNK, (c + 1) * GM_CHUNK)
        for g in range(GM_GROUPS):
            cs = slice(g * HEAD_DIM, (g + 1) * HEAD_DIM)
            s = jnp.dot(ws_ref[g], v[rs, cs], preferred_element_type=F32) + bs_ref[g]
            o_ref[rs, cs] = (u[rs, cs] * s).astype(o_ref.dtype)


def chunk_gmlp(P, ln_g, ln_b, w_s, b_s, n_rows):
    bs_rep = jnp.broadcast_to(b_s[:, :, None], (GM_GROUPS, GM_CHUNK, HEAD_DIM))
    BW = BRANCH_WIDTH
    return pl.pallas_call(
        _gmlp_kernel,
        grid=(n_rows // ROW_BLK,),
        in_specs=[pl.BlockSpec((ROW_BLK, BW), lambda i: (i, COL_GM_U)),
                  pl.BlockSpec((ROW_BLK, BW), lambda i: (i, COL_GM_V)),
                  pl.BlockSpec((1, BW), lambda i: (0, 0)),
                  pl.BlockSpec((1, BW), lambda i: (0, 0)),
                  pl.BlockSpec((GM_GROUPS, GM_CHUNK, GM_CHUNK), lambda i: (0, 0, 0)),
                  pl.BlockSpec((GM_GROUPS, GM_CHUNK, HEAD_DIM), lambda i: (0, 0, 0))],
        out_specs=pl.BlockSpec((ROW_BLK, BW), lambda i: (i, 0)),
        out_shape=jax.ShapeDtypeStruct((n_rows, BW), BF16),
        compiler_params=_cparams("parallel"),
    )(P, P, ln_g.reshape(1, BW), ln_b.reshape(1, BW), w_s.astype(BF16), bs_rep)


CONV_HALO = 16
CONV_SUB = 32


def _conv_kernel(a_ref, b_ref, ap_ref, bp_ref, an_ref, bn_ref, w_ref, cb_ref, g_ref, beta_ref, o_ref, gbuf):
    i = pl.program_id(0)

    def glu(a, b):
        return a.astype(F32) * _sigmoid(b.astype(F32))

    prev_ok = jnp.logical_and(i != 0, i != LAT_ROW_BLKS)
    next_ok = jnp.logical_and(i != LAT_ROW_BLKS - 1, i != N_ROW_BLKS - 1)
    gbuf[0:CONV_HALO, :] = jnp.where(prev_ok, glu(ap_ref[...], bp_ref[...]), 0.0)
    gbuf[CONV_HALO:CONV_HALO + ROW_BLK, :] = glu(a_ref[...], b_ref[...])
    gbuf[CONV_HALO + ROW_BLK:, :] = jnp.where(next_ok, glu(an_ref[...], bn_ref[...]), 0.0)
    first = CONV_HALO - CONV_K // 2
    for r in range(ROW_BLK // CONV_SUB):
        base = r * CONV_SUB
        acc = jnp.zeros((CONV_SUB, BRANCH_WIDTH), F32)
        for k in range(CONV_K):
            acc = acc + gbuf[base + first + k:base + first + k + CONV_SUB, :] * w_ref[k:k + 1, :]
        y = _ln_rows(acc + cb_ref[...], g_ref[...], beta_ref[...])
        o_ref[base:base + CONV_SUB, :] = (y * _sigmoid(y)).astype(o_ref.dtype)


def conformer_conv(P, conv_w, conv_b, ln_g, ln_b, n_rows):
    BW = BRANCH_WIDTH
    hpb = ROW_BLK // CONV_HALO
    last_halo = N_TOK // CONV_HALO - 1

    def prev_map(col):
        return lambda i: (jnp.maximum(i * hpb - 1, 0), col)

    def next_map(col):
        return lambda i: (jnp.minimum((i + 1) * hpb, last_halo), col)

    vec = pl.BlockSpec((1, BW), lambda i: (0, 0))
    return pl.pallas_call(
        _conv_kernel,
        grid=(n_rows // ROW_BLK,),
        in_specs=[pl.BlockSpec((ROW_BLK, BW), lambda i: (i, COL_CV_A)),
                  pl.BlockSpec((ROW_BLK, BW), lambda i: (i, COL_CV_B)),
                  pl.BlockSpec((CONV_HALO, BW), prev_map(COL_CV_A)),
                  pl.BlockSpec((CONV_HALO, BW), prev_map(COL_CV_B)),
                  pl.BlockSpec((CONV_HALO, BW), next_map(COL_CV_A)),
                  pl.BlockSpec((CONV_HALO, BW), next_map(COL_CV_B)),
                  pl.BlockSpec((CONV_K + 1, BW), lambda i: (0, 0)),
                  vec, vec, vec],
        out_specs=pl.BlockSpec((ROW_BLK, BW), lambda i: (i, 0)),
        out_shape=jax.ShapeDtypeStruct((n_rows, BW), BF16),
        scratch_shapes=[pltpu.VMEM((ROW_BLK + 2 * CONV_HALO, BW), F32)],
        compiler_params=_cparams("parallel"),
    )(P, P, P, P, P, P, jnp.pad(conv_w, ((0, 1), (0, 0))), conv_b.reshape(1, BW),
      ln_g.reshape(1, BW), ln_b.reshape(1, BW))


MW_TILE = 512
MW_SHIFT = MERGE_START % MW_TILE


def _merge_weights_kernel(a_ref, b_ref, o_ref):
    o_ref[0:MW_TILE - MW_SHIFT, :] = a_ref[MW_SHIFT:, :].astype(o_ref.dtype)
    o_ref[MW_TILE - MW_SHIFT:, :] = b_ref[...].astype(o_ref.dtype)


def merge_weights(w_in_t, layer):
    D = D_MODEL
    a0 = (MERGE_START - MW_SHIFT) // MW_TILE
    b0 = (MERGE_START - MW_SHIFT) // MW_SHIFT
    return pl.pallas_call(
        _merge_weights_kernel,
        grid=(N_BRANCH * D // MW_TILE,),
        in_specs=[pl.BlockSpec((None, MW_TILE, D), lambda j: (layer, a0 + j, 0)),
                  pl.BlockSpec((None, MW_SHIFT, D), lambda j: (layer, b0 + (j + 1) * (MW_TILE // MW_SHIFT), 0))],
        out_specs=pl.BlockSpec((MW_TILE, D), lambda j: (j, 0)),
        out_shape=jax.ShapeDtypeStruct((N_BRANCH * D, D), BF16),
        compiler_params=_cparams("parallel"),
        name="merge_weights",
    )(w_in_t, w_in_t)


def _merge_kernel(h_ref, b0, b1, b2, b3, m0, m1, m2, m3, w0, w1, w2, w3, o_ref):
    h = h_ref[...]
    acc = None
    for br, wm, wb in ((b0, m0, w0), (b1, m1, w1), (b2, m2, w2), (b3, m3, w3)):
        gate = _sigmoid(_dot_nt(h, wm[...]))
        t = gate * jnp.dot(br[...], wb[...], preferred_element_type=F32)
        acc = t if acc is None else acc + t
    o_ref[...] = acc.astype(o_ref.dtype)


def merge_branches(h_all, branches, w_merge, w_branch, n_rows, tm, tn=256):
    D = D_MODEL
    BW = BRANCH_WIDTH
    nj = D // tn
    one = pl.Buffered(1)
    in_specs = [pl.BlockSpec((tm, D), lambda i, j: (i, 0), pipeline_mode=one)]
    in_specs += [pl.BlockSpec((tm, BW), lambda i, j: (i, 0), pipeline_mode=one) for _ in range(N_BRANCH)]
    in_specs += [pl.BlockSpec((tn, D), functools.partial(lambda i, j, b: (b * nj + j, 0), b=b))
                 for b in range(N_BRANCH)]
    in_specs += [pl.BlockSpec((None, BW, tn), functools.partial(lambda i, j, b: (b, 0, j), b=b))
                 for b in range(N_BRANCH)]
    return pl.pallas_call(
        _merge_kernel,
        grid=(n_rows // tm, nj),
        in_specs=in_specs,
        out_specs=pl.BlockSpec((tm, tn), lambda i, j: (i, j)),
        out_shape=jax.ShapeDtypeStruct((n_rows, D), BF16),
        compiler_params=_cparams("parallel", "arbitrary"),
        name="gated_merge",
    )(h_all, *branches, *([w_merge] * N_BRANCH), *([w_branch] * N_BRANCH))


def _out_ln_kernel(y_ref, w_ref, x_ref, g1_ref, lg_ref, lb_ref, sh_ref, sc_ref, x1_ref, hm_ref, *, nk, tm):
    k = pl.program_id(1)
    part = jnp.dot(y_ref[...], w_ref[...], preferred_element_type=F32)

    @pl.when(k == 0)
    def _():
        x1_ref[...] = part

    @pl.when(k > 0)
    def _():
        x1_ref[...] += part

    @pl.when(k == nk - 1)
    def _():
        sub = next(s for s in (64, 48, 32, 16) if tm % s == 0)

        def chunk(r, carry):
            r0 = pl.multiple_of(r * sub, 16)
            rows = pl.ds(r0, sub)
            is_ctx = pl.program_id(0) * tm + r0 + lax.broadcasted_iota(jnp.int32, (sub, 1), 0) >= SEQ
            z = ALPHA * x_ref[rows, :] + _pick(g1_ref, is_ctx) * x1_ref[rows, :]
            x1 = _ln_rows(z, lg_ref[...], lb_ref[...])
            x1_ref[rows, :] = x1
            hm_ref[rows, :] = (x1 * (1.0 + _pick(sc_ref, is_ctx)) + _pick(sh_ref, is_ctx)).astype(hm_ref.dtype)
            return carry

        lax.fori_loop(0, tm // sub, chunk, 0)


def out_proj_ln(y, w_out, x_all, mod_l, ln_g, ln_b, n_rows, tm, tk=512):
    D = D_MODEL
    nk = D // tk
    kern = functools.partial(_out_ln_kernel, nk=nk, tm=tm)
    vec = pl.BlockSpec((1, D), lambda i, k: (0, 0))

    def modspec(col):
        return pl.BlockSpec((16, D), lambda i, k: (0, col))

    return pl.pallas_call(
        kern,
        grid=(n_rows // tm, nk),
        in_specs=[pl.BlockSpec((tm, tk), lambda i, k: (i, k)),
                  pl.BlockSpec((tk, D), lambda i, k: (k, 0)),
                  pl.BlockSpec((tm, D), lambda i, k: (i, 0), pipeline_mode=pl.Buffered(1)),
                  modspec(2), vec, vec, modspec(3), modspec(4)],
        out_specs=[pl.BlockSpec((tm, D), lambda i, k: (i, 0)),
                   pl.BlockSpec((tm, D), lambda i, k: (i, 0))],
        out_shape=[jax.ShapeDtypeStruct((n_rows, D), F32),
                   jax.ShapeDtypeStruct((n_rows, D), BF16)],
        compiler_params=_cparams("parallel", "arbitrary"),
        name="out_proj_ln1",
    )(y, w_out, x_all, mod_l, ln_g.reshape(1, D), ln_b.reshape(1, D), mod_l, mod_l)


def _ln2_kernel(x_ref, y_ref, g2_ref, lg_ref, lb_ref, sh_ref, sc_ref, x2_ref, *h_ref, blk_ctx_from):
    is_ctx = pl.program_id(0) >= blk_ctx_from
    z = ALPHA * x_ref[...] + _pick(g2_ref, is_ctx) * y_ref[...]
    x2 = _ln_rows(z, lg_ref[...], lb_ref[...])
    x2_ref[...] = x2
    if h_ref:
        h_ref[0][...] = (x2 * (1.0 + _pick(sc_ref, is_ctx)) + _pick(sh_ref, is_ctx)).astype(BF16)


def ffn_residual_ln(x1, y_moe, mod_l, ln_g, ln_b, mod_next, n_rows):
    D = D_MODEL
    want_h = mod_next is not None
    nxt = mod_next if want_h else mod_l
    kern = functools.partial(_ln2_kernel, blk_ctx_from=LAT_ROW_BLKS)
    row = pl.BlockSpec((ROW_BLK, D), lambda i: (i, 0))
    vec = pl.BlockSpec((1, D), lambda i: (0, 0))
    out_specs = [row, row] if want_h else [row]
    out_shape = [jax.ShapeDtypeStruct((n_rows, D), F32)]
    if want_h:
        out_shape.append(jax.ShapeDtypeStruct((n_rows, D), BF16))
    return pl.pallas_call(
        kern,
        grid=(n_rows // ROW_BLK,),
        in_specs=[row, row, pl.BlockSpec((16, D), lambda i: (0, 5)), vec, vec,
                  pl.BlockSpec((16, D), lambda i: (0, 0)), pl.BlockSpec((16, D), lambda i: (0, 1))],
        out_specs=out_specs,
        out_shape=out_shape,
        compiler_params=_cparams("parallel"),
    )(x1, y_moe, mod_l, ln_g.reshape(1, D), ln_b.reshape(1, D), nxt, nxt)


def _tok_tile(n):
    return next(t for t in (1408, 1280, 1024, 768, 512, 384, 256, 128) if n % t == 0)


def _router_kernel(w_ref, h_ref, o_ref):
    z = _dot_nt(w_ref[...], h_ref[...])
    ez = jnp.exp(z - jnp.max(z, axis=0, keepdims=True))
    o_ref[...] = ez / jnp.sum(ez, axis=0, keepdims=True)


def router_affinities(hm, w_router_t, n_rows):
    D = D_MODEL
    tt = _tok_tile(n_rows)
    return pl.pallas_call(
        _router_kernel,
        grid=(n_rows // tt,),
        in_specs=[pl.BlockSpec((N_EXPERTS, D), lambda i: (0, 0)),
                  pl.BlockSpec((tt, D), lambda i: (i, 0))],
        out_specs=pl.BlockSpec((N_EXPERTS, tt), lambda i: (0, i)),
        out_shape=jax.ShapeDtypeStruct((N_EXPERTS, n_rows), F32),
        compiler_params=_cparams("parallel"),
        name="router_affinities",
    )(w_router_t, hm)


def _select_kernel(aff_ref, idx_ref, cum_ref, *, T, cap):
    E = N_EXPERTS
    nc = T // LANES
    bits = pltpu.bitcast(aff_ref[...], jnp.int32)

    def bisect(_, lohi):
        lo, hi = lohi
        mid = lo + ((hi - lo + 1) >> 1)
        cnt = jnp.sum((bits >= mid).astype(F32), axis=1, keepdims=True)
        ok = cnt >= cap
        return jnp.where(ok, mid, lo), jnp.where(ok, hi, mid - 1)

    thr, _ = lax.fori_loop(0, 31, bisect, (jnp.zeros((E, 1), jnp.int32), jnp.full((E, 1), 0x7F800000, jnp.int32)))
    gt = (bits > thr).astype(F32)
    eq = (bits == thr).astype(F32)
    need = cap - jnp.sum(gt, axis=1, keepdims=True)
    row = lax.broadcasted_iota(jnp.int32, (LANES, LANES), 0)
    col = lax.broadcasted_iota(jnp.int32, (LANES, LANES), 1)
    before = (row < col).astype(BF16)
    upto = (row <= col).astype(BF16)
    seen_eq = jnp.zeros((E, 1), F32)
    seen_sel = jnp.zeros((E, 1), F32)
    for c in range(nc):
        sl = slice(c * LANES, (c + 1) * LANES)
        eq_c = eq[:, sl]
        rank_eq = jnp.dot(eq_c.astype(BF16), before, preferred_element_type=F32) + seen_eq
        seen_eq = seen_eq + jnp.sum(eq_c, axis=1, keepdims=True)
        sel_c = jnp.maximum(gt[:, sl], jnp.where(rank_eq < need, eq_c, 0.0))
        cum_ref[:, sl] = jnp.dot(sel_c.astype(BF16), upto, preferred_element_type=F32) + seen_sel
        seen_sel = seen_sel + jnp.sum(sel_c, axis=1, keepdims=True)

    sb = min(cap, 256)
    lane = lax.broadcasted_iota(jnp.int32, (sb, LANES), 1)
    for b in range(cap // sb):
        slot = (b * sb + lax.broadcasted_iota(jnp.int32, (sb, LANES), 0)).astype(F32)
        res = jnp.zeros((sb, LANES), F32)
        for e in range(E):
            def count(c, acc):
                cum = cum_ref[e:e + 1, pl.ds(pl.multiple_of(c * LANES, LANES), LANES)]
                return acc + jnp.where(cum <= slot, 1.0, 0.0)

            acc = lax.fori_loop(0, nc, count, jnp.zeros((sb, LANES), F32))
            res = jnp.where(lane == e, jnp.sum(acc, axis=1, keepdims=True), res)
        idx_ref[b * sb:(b + 1) * sb, :] = res.astype(jnp.int32)


def expert_choice_indices(aff, first_tok, T, cap):
    kern = functools.partial(_select_kernel, T=T, cap=cap)
    out = pl.pallas_call(
        kern,
        grid=(1,),
        in_specs=[pl.BlockSpec((N_EXPERTS, T), lambda i: (0, first_tok // T))],
        out_specs=pl.BlockSpec((cap, LANES), lambda i: (0, 0)),
        out_shape=jax.ShapeDtypeStruct((cap, LANES), jnp.int32),
        scratch_shapes=[pltpu.VMEM((N_EXPERTS, T), F32)],
        compiler_params=_cparams("arbitrary"),
        name="expert_choice_select",
    )(aff)
    return out[:, :N_EXPERTS].T


def _gather_swiglu_kernel(idx_ref, nxt_ref, x_hbm, sh_ref, sc_ref, wr_ref, wg_ref, wu_ref, o_ref, gain_ref,
                          xbuf, xe, sem, *, M, cap_l):
    e = pl.program_id(0)
    j = pl.program_id(1)
    n_e = pl.num_programs(0)

    def row_copy(ref, r):
        return pltpu.make_async_copy(x_hbm.at[pl.ds(ref[0, r], 1), :], xbuf.at[pl.ds(r, 1), :], sem.at[0])

    def each_row(fn):
        def body(r, carry):
            fn(r)
            return carry
        lax.fori_loop(0, M, body, 0, unroll=8)

    @pl.when(j == 0)
    def _():
        @pl.when(e == 0)
        def _():
            each_row(lambda r: row_copy(idx_ref, r).start())

        each_row(lambda r: row_copy(idx_ref, r).wait())
        xe[0:cap_l, :] = (xbuf[0:cap_l, :] * (1.0 + sc_ref[0:1, :]) + sh_ref[0:1, :]).astype(BF16)
        if M > cap_l:
            xe[cap_l:, :] = (xbuf[cap_l:, :] * (1.0 + sc_ref[1:2, :]) + sh_ref[1:2, :]).astype(BF16)

        @pl.when(e + 1 < n_e)
        def _():
            each_row(lambda r: row_copy(nxt_ref, r).start())

        z = jnp.dot(xe[...], wr_ref[...], preferred_element_type=F32)
        lane = lax.broadcasted_iota(jnp.int32, z.shape, 1)
        z = jnp.where(lane < N_EXPERTS, z, -jnp.inf)
        ez = jnp.exp(z - jnp.max(z, axis=1, keepdims=True))
        gain_ref[...] = (jnp.sum(jnp.where(lane == e, ez, 0.0), axis=1, keepdims=True)
                         / jnp.sum(ez, axis=1, keepdims=True))

    x = xe[...]
    g = jnp.dot(x, wg_ref[...].astype(BF16), preferred_element_type=F32)
    u = jnp.dot(x, wu_ref[...].astype(BF16), preferred_element_type=F32)
    o_ref[...] = (g * _sigmoid(g) * u).astype(o_ref.dtype)


def expert_gather_swiglu(idx, x1, mod_l, w_router_pad, w_gate, w_up, layer, cap_l, tn=256):
    E, M = idx.shape
    D = D_MODEL
    N = w_gate.shape[-1]
    idx3 = idx.reshape(E, 1, M)
    wspec = pl.BlockSpec((None, None, D, tn), lambda e, j: (layer, e, 0, j))
    kern = functools.partial(_gather_swiglu_kernel, M=M, cap_l=cap_l)
    return pl.pallas_call(
        kern,
        grid=(E, N // tn),
        in_specs=[pl.BlockSpec((None, 1, M), lambda e, j: (e, 0, 0), memory_space=pltpu.SMEM),
                  pl.BlockSpec((None, 1, M), lambda e, j: (jnp.minimum(e + 1, E - 1), 0, 0), memory_space=pltpu.SMEM),
                  pl.BlockSpec(memory_space=pl.ANY),
                  pl.BlockSpec((16, D), lambda e, j: (0, 3)),
                  pl.BlockSpec((16, D), lambda e, j: (0, 4)),
                  pl.BlockSpec((D, LANES), lambda e, j: (0, 0)),
                  wspec, wspec],
        out_specs=[pl.BlockSpec((None, M, tn), lambda e, j: (e, 0, j)),
                   pl.BlockSpec((None, M, 1), lambda e, j: (e, 0, 0))],
        out_shape=[jax.ShapeDtypeStruct((E, M, N), BF16),
                   jax.ShapeDtypeStruct((E, M, 1), F32)],
        scratch_shapes=[pltpu.VMEM((M, D), F32), pltpu.VMEM((M, D), BF16), pltpu.SemaphoreType.DMA((1,))],
        compiler_params=_cparams("arbitrary", "arbitrary"),
        name="expert_gather_swiglu",
    )(idx3, idx3, x1, mod_l, mod_l, w_router_pad, w_gate, w_up)


def _down_kernel(h_ref, w_ref, g_ref, o_ref):
    y = jnp.dot(h_ref[...], w_ref[...].astype(BF16), preferred_element_type=F32)
    o_ref[...] = y * g_ref[...]


def expert_down(hid, w_down, gains, layer, tn=1024):
    E, M, K = hid.shape
    N = w_down.shape[-1]
    return pl.pallas_call(
        _down_kernel,
        grid=(E, N // tn),
        in_specs=[pl.BlockSpec((None, M, K), lambda e, j: (e, 0, 0)),
                  pl.BlockSpec((None, None, K, tn), lambda e, j: (layer, e, 0, j)),
                  pl.BlockSpec((None, M, 1), lambda e, j: (e, 0, 0))],
        out_specs=pl.BlockSpec((None, M, tn), lambda e, j: (e, 0, j)),
        out_shape=jax.ShapeDtypeStruct((E, M, N), F32),
        compiler_params=_cparams("parallel", "parallel"),
    )(hid, w_down, gains)


def _scatter_kernel(idx_ref, ye_ref, acc_in, acc_ref, buf, gsem, ssem, *, R, n_chunks):
    del acc_in
    c = pl.program_id(1)

    def gather(chunk, slot, r):
        t = idx_ref[0, chunk * R + r]
        return pltpu.make_async_copy(acc_ref.at[pl.ds(t, 1), :], buf.at[slot, pl.ds(r, 1), :], gsem.at[slot])

    def scatter(chunk, slot, r):
        t = idx_ref[0, chunk * R + r]
        return pltpu.make_async_copy(buf.at[slot, pl.ds(r, 1), :], acc_ref.at[pl.ds(t, 1), :], ssem.at[slot])

    def each_row(fn):
        def body(r, carry):
            fn(r)
            return carry
        lax.fori_loop(0, R, body, 0, unroll=8)

    slot = c % 2

    @pl.when(c == 0)
    def _():
        each_row(lambda r: gather(0, 0, r).start())

    each_row(lambda r: gather(c, slot, r).wait())

    @pl.when(c >= 1)
    def _():
        each_row(lambda r: scatter(c - 1, 1 - slot, r).wait())

    @pl.when(c + 1 < n_chunks)
    def _():
        each_row(lambda r: gather(c + 1, 1 - slot, r).start())

    buf[slot] = buf[slot] + ye_ref[...]
    each_row(lambda r: scatter(c, slot, r).start())

    @pl.when(c == n_chunks - 1)
    def _():
        each_row(lambda r: scatter(c, slot, r).wait())


def scatter_add_rows(idx, ye, n_rows):
    E, M, D = ye.shape
    R = next(r for r in (264, 256, 160, 128, 64, 32) if M % r == 0)
    n_chunks = M // R
    kern = functools.partial(_scatter_kernel, R=R, n_chunks=n_chunks)
    return pl.pallas_call(
        kern,
        grid=(E, n_chunks),
        in_specs=[pl.BlockSpec((None, 1, M), lambda e, c: (e, 0, 0), memory_space=pltpu.SMEM),
                  pl.BlockSpec((None, R, D), lambda e, c: (e, c, 0)),
                  pl.BlockSpec(memory_space=pl.ANY)],
        out_specs=pl.BlockSpec(memory_space=pl.ANY),
        out_shape=jax.ShapeDtypeStruct((n_rows, D), F32),
        scratch_shapes=[pltpu.VMEM((2, R, D), F32), pltpu.SemaphoreType.DMA((2,)), pltpu.SemaphoreType.DMA((2,))],
        input_output_aliases={2: 0},
        compiler_params=_cparams("arbitrary", "arbitrary"),
        name="scatter_add_rows",
    )(idx.reshape(E, 1, M), ye, jnp.zeros((n_rows, D), F32))


N_CHUNKS = N_TOK // ML_CHUNK
LAT_CHUNKS = SEQ // ML_CHUNK
CTX_CHUNKS = CTX_LEN // ML_CHUNK


def _mlstm_kernel(q_ref, k_ref, v_ref, og_ref, g_ref, ng_ref, out_ref, acc_ref, c_ref, n_ref, m_ref, *, hb, out_chunks):
    L, d = ML_CHUNK, ML_HEAD_DIM
    row = lax.broadcasted_iota(jnp.int32, (L, L), 0)
    col = lax.broadcasted_iota(jnp.int32, (L, L), 1)
    eye = row == col
    masks = (col <= row, col >= row)
    lasts = (L - 1, 0)
    k_scale = ML_HEAD_DIM ** -0.5

    acc_ref[...] = jnp.zeros(acc_ref.shape, F32)
    c_ref[...] = jnp.zeros(c_ref.shape, F32)
    n_ref[...] = jnp.zeros(n_ref.shape, F32)
    m_ref[...] = jnp.zeros(m_ref.shape, F32)

    def col_of(r):
        return jnp.sum(jnp.where(eye, r, 0.0), axis=1, keepdims=True)

    def step(t, carry):
        chunk = (jnp.where(t < CTX_CHUNKS, t + LAT_CHUNKS, t - CTX_CHUNKS), N_CHUNKS - 1 - t)
        for j in range(hb):
            hs = slice(j * d, (j + 1) * d)
            for dr in range(2):
                si = 2 * j + dr
                off = pl.multiple_of(chunk[dr] * L, L)
                rows = pl.ds(off, L)
                q = q_ref[rows, hs]
                kf = k_ref[rows, hs].astype(F32) * k_scale
                v = v_ref[rows, hs]
                b_row = g_ref[j, 2 * dr:2 * dr + 1, rows]
                a_row = g_ref[j, 2 * dr + 1:2 * dr + 2, rows]
                last = lasts[dr]
                m = m_ref[si][0:1, 0:1]
                n = n_ref[si][0:1, :]
                C = c_ref[si]
                b_col = col_of(b_row)
                a_col = col_of(a_row)
                dlog = jnp.where(masks[dr], b_col + a_row, -jnp.inf)
                inter = b_col + m
                mj = jnp.maximum(inter, jnp.max(dlog, axis=1, keepdims=True))
                w_inter = jnp.exp(inter - mj)
                qk = lax.dot_general(q, kf.astype(BF16), (((1,), (1,)), ((), ())), preferred_element_type=F32)
                s = qk * jnp.exp(dlog - mj)
                num = (w_inter * jnp.dot(q, C.astype(BF16), preferred_element_type=F32)
                       + jnp.dot(s.astype(BF16), v, preferred_element_type=F32))
                qn = jnp.sum(q.astype(F32) * n, axis=1, keepdims=True)
                den = w_inter * qn + jnp.sum(s, axis=1, keepdims=True)
                h = num / jnp.maximum(jnp.abs(den), jnp.exp(-mj))
                acc_ref[rows, hs] += h
                m_new = mj[last:last + 1, :]
                b_end = b_row[:, last:last + 1]
                w_c = jnp.exp(b_end + m - m_new)
                kw = kf * jnp.exp(b_end + a_col - m_new)
                c_ref[si] = w_c * C + lax.dot_general(kw.astype(BF16), v, (((0,), (0,)), ((), ())),
                                                      preferred_element_type=F32)
                n_ref[si] = jnp.broadcast_to(w_c * n + jnp.sum(kw, axis=0, keepdims=True), (8, d))
                m_ref[si] = jnp.broadcast_to(m_new, (8, LANES))
        return carry

    lax.fori_loop(0, N_CHUNKS, step, 0)

    def finish(c, carry):
        rows = pl.ds(pl.multiple_of(c * L, L), L)
        for j in range(hb):
            hs = slice(j * d, (j + 1) * d)
            hh = acc_ref[rows, hs]
            hc = hh - jnp.mean(hh, axis=1, keepdims=True)
            hn = hc * lax.rsqrt(jnp.mean(hc * hc, axis=1, keepdims=True) + EPS) * ng_ref[j]
            out_ref[rows, hs] = (_sigmoid(og_ref[rows, hs].astype(F32)) * hn).astype(out_ref.dtype)
        return carry

    lax.fori_loop(0, out_chunks, finish, 0)


def mlstm(P, G, norm_g, n_rows, hb=2):
    w = hb * ML_HEAD_DIM
    per = BRANCH_WIDTH // w

    def colspec(col):
        return pl.BlockSpec((N_TOK, w), lambda i: (0, col * per + i))

    kern = functools.partial(_mlstm_kernel, hb=hb, out_chunks=n_rows // ML_CHUNK)
    return pl.pallas_call(
        kern,
        grid=(ML_HEADS // hb,),
        in_specs=[colspec(COL_ML_Q), colspec(COL_ML_K), colspec(COL_ML_V),
                  pl.BlockSpec((n_rows, w), lambda i: (0, COL_ML_O * per + i)),
                  pl.BlockSpec((hb, 4, N_TOK), lambda i: (i, 0, 0)),
                  pl.BlockSpec((hb, 1, ML_HEAD_DIM), lambda i: (i, 0, 0))],
        out_specs=pl.BlockSpec((n_rows, w), lambda i: (0, i)),
        out_shape=jax.ShapeDtypeStruct((n_rows, BRANCH_WIDTH), BF16),
        scratch_shapes=[pltpu.VMEM((N_TOK, w), F32),
                        pltpu.VMEM((2 * hb, ML_HEAD_DIM, ML_HEAD_DIM), F32),
                        pltpu.VMEM((2 * hb, 8, ML_HEAD_DIM), F32),
                        pltpu.VMEM((2 * hb, 8, LANES), F32)],
        compiler_params=_cparams("parallel"),
        name="mlstm_scan",
    )(P, P, P, P, G, norm_g.reshape(ML_HEADS, 1, ML_HEAD_DIM))


def _norm_stats(x):
    xc = x - jnp.mean(x, axis=-1, keepdims=True)
    return xc * lax.rsqrt(jnp.mean(xc * xc, axis=-1, keepdims=True) + EPS)


def _rope_tables():
    rows = SEQ // GRID_W
    row = jnp.repeat(jnp.arange(rows), GRID_W).astype(F32)
    col = (jnp.arange(SEQ) % GRID_W).astype(F32)
    n_freq = HEAD_DIM // 4
    inv = ROPE_THETA ** (-jnp.arange(n_freq, dtype=F32) / n_freq)
    ang_r = row[:, None] * inv[None, :]
    ang_c = col[:, None] * inv[None, :]
    cos = jnp.concatenate([jnp.cos(ang_r), jnp.cos(ang_r), jnp.cos(ang_c), jnp.cos(ang_c)], axis=1)
    sin = jnp.concatenate([-jnp.sin(ang_r), jnp.sin(ang_r), -jnp.sin(ang_c), jnp.sin(ang_c)], axis=1)
    cos = jnp.concatenate([cos, jnp.ones((CTX_LEN, HEAD_DIM), F32)], axis=0)
    sin = jnp.concatenate([sin, jnp.zeros((CTX_LEN, HEAD_DIM), F32)], axis=0)
    return cos, sin


def _mlstm_chunk_scan(q, k, v, a_row, b_row, state, reverse):
    H, T, d = q.shape
    nc = T // ML_CHUNK

    def chunks(x):
        return jnp.moveaxis(x.reshape(H, nc, ML_CHUNK, *x.shape[2:]), 1, 0)

    tri = jnp.tril(jnp.ones((ML_CHUNK, ML_CHUNK), dtype=bool))
    mask = tri.T if reverse else tri
    last = 0 if reverse else ML_CHUNK - 1

    def step(carry, inp):
        C, n, m = carry
        qc, kc, vc, ac, bc = inp
        inter = bc + m[..., None]
        dlog = jnp.where(mask, bc[..., :, None] + ac[..., None, :], -jnp.inf)
        mj = jnp.maximum(inter, jnp.max(dlog, axis=-1))
        w_inter = jnp.exp(inter - mj)
        s = jnp.einsum('hjd,hsd->hjs', qc, kc) * jnp.exp(dlog - mj[..., None])
        num = (w_inter[..., None] * jnp.einsum('hjd,hde->hje', qc, C)
               + jnp.einsum('hjs,hse->hje', s, vc))
        den = w_inter * jnp.einsum('hjd,hd->hj', qc, n) + jnp.sum(s, axis=-1)
        h = num / jnp.maximum(jnp.abs(den), jnp.exp(-mj))[..., None]
        m_new = mj[..., last]
        b_end = bc[..., last]
        w_c = jnp.exp(b_end + m - m_new)
        w_s = jnp.exp(b_end[..., None] + ac - m_new[..., None])
        C_new = w_c[..., None, None] * C + jnp.einsum('hs,hsd,hse->hde', w_s, kc, vc)
        n_new = w_c[..., None] * n + jnp.einsum('hs,hsd->hd', w_s, kc)
        return (C_new, n_new, m_new), h

    final, h = lax.scan(step, state, (chunks(q), chunks(k), chunks(v), chunks(a_row), chunks(b_row)),
                        reverse=reverse)
    return jnp.moveaxis(h, 0, 1).reshape(H, T, d), final


def _mlstm(P, G, norm_g, rows_out):
    BW = BRANCH_WIDTH

    def heads(col, rows):
        a = P[rows, col * BW:(col + 1) * BW].astype(F32)
        return jnp.swapaxes(a.reshape(-1, ML_HEADS, ML_HEAD_DIM), 0, 1)

    H = ML_HEADS
    zero = (jnp.zeros((H, ML_HEAD_DIM, ML_HEAD_DIM), F32), jnp.zeros((H, ML_HEAD_DIM), F32), jnp.zeros((H,), F32))
    outs = {}
    st_f, st_b = zero, zero
    for name, rows in (('ctx', slice(SEQ, N_TOK)), ('lat', slice(0, SEQ))):
        q = heads(COL_ML_Q, rows)
        k = heads(COL_ML_K, rows) * (ML_HEAD_DIM ** -0.5)
        v = heads(COL_ML_V, rows)
        h_f, st_f = _mlstm_chunk_scan(q, k, v, G[H:2 * H, rows], G[0:H, rows], st_f, False)
        h_b, st_b = _mlstm_chunk_scan(q, k, v, G[3 * H:4 * H, rows], G[2 * H:3 * H, rows], st_b, True)
        outs[name] = h_f + h_b
    h = jnp.concatenate([outs['lat'], outs['ctx']], axis=1)[:, :rows_out]
    hn = _norm_stats(jnp.swapaxes(h, 0, 1)) * norm_g.reshape(H, ML_HEAD_DIM)
    o_pre = P[:rows_out, COL_ML_O * BW:(COL_ML_O + 1) * BW].astype(F32)
    return (jax.nn.sigmoid(o_pre) * hn.reshape(rows_out, BW)).astype(BF16)


def kernel(x, c, ctx, c_ctx, w_mod, b_mod, w_in, att_q_gain, att_k_gain, gm_ln_g, gm_ln_b,
           gm_w_s, gm_b_s, conv_w, conv_b, conv_ln_g, conv_ln_b, ml_gate_bias, ml_norm_g,
           w_branch, w_out, ln1_g, ln1_b, w_router, w_gate, w_up, w_down, ln2_g, ln2_b):
    cos_t, sin_t = _rope_tables()
    w_in_t = jnp.swapaxes(w_in, 1, 2)
    cc = jnp.zeros((16, D_MODEL), F32).at[0].set(c[0]).at[1].set(c_ctx)
    mods = modulation(cc, w_mod, b_mod)
    x_all = jnp.concatenate([x[0], ctx[0]], axis=0)
    h_all = modulate(x_all, mods[0])
    for l in range(DEPTH):
        last = l == DEPTH - 1
        n_rows = SEQ if last else N_TOK
        tm = (SEQ if last else N_TOK) // 8
        mod_l = mods[l]

        P = branch_projection(h_all, w_in_t, l)
        gbias = jnp.broadcast_to(ml_gate_bias[l].reshape(GATE_COLS, 1), (GATE_COLS, LANES))
        G = mlstm_gate_rows(h_all, w_in_t, l, gbias)
        q, k = qk_prepare(P, cos_t, sin_t, att_q_gain[l], att_k_gain[l])

        br_a = chunk_gmlp(P, gm_ln_g[l], gm_ln_b[l], gm_w_s[l], gm_b_s[l], n_rows)
        br_b = attention(q, k, P, True)
        if not last:
            br_b = jnp.concatenate([br_b, attention(q, k, P, False)], axis=0)
        br_c = conformer_conv(P, conv_w[l], conv_b[l], conv_ln_g[l], conv_ln_b[l], n_rows)
        G_heads = G.reshape(ML_N_GATES, ML_HEADS, N_TOK).transpose(1, 0, 2)
        br_d = mlstm(P, G_heads, ml_norm_g[l], n_rows)

        w_merge = merge_weights(w_in_t, l)
        y = merge_branches(h_all, [br_a, br_b, br_c, br_d], w_merge, w_branch[l].astype(BF16), n_rows, tm)
        x1, hm = out_proj_ln(y, w_out[l].astype(BF16), x_all, mod_l, ln1_g[l], ln1_b[l], n_rows, tm // 2)

        cap_l = CAPACITY_FACTOR * SEQ // N_EXPERTS
        aff = router_affinities(hm, w_router[l].T.astype(BF16), n_rows)
        idx = expert_choice_indices(aff, 0, SEQ, cap_l)
        if not last:
            idx_c = expert_choice_indices(aff, SEQ, CTX_LEN, CAPACITY_FACTOR * CTX_LEN // N_EXPERTS)
            idx = jnp.concatenate([idx, idx_c + SEQ], axis=1)
        w_router_pad = jnp.pad(w_router[l], ((0, 0), (0, LANES - N_EXPERTS))).astype(BF16)
        hid, gains = expert_gather_swiglu(idx, x1, mod_l, w_router_pad, w_gate, w_up, l, cap_l)
        ye = expert_down(hid, w_down, gains, l)
        y_moe = scatter_add_rows(idx, ye, n_rows)
        res = ffn_residual_ln(x1, y_moe, mod_l, ln2_g[l], ln2_b[l], None if last else mods[l + 1], n_rows)
        if not last:
            x_all, h_all = res
        else:
            x_all = res[0]
    return x_all[None]
```

```python
import functools

import jax
import jax.numpy as jnp
from jax import lax
from jax.experimental import pallas as pl
from jax.experimental.pallas import tpu as pltpu

F32 = jnp.float32
BF16 = jnp.bfloat16

D_MODEL = 4096
SEQ = 8192
DEPTH = 2
GRID_W = 64
CTX_LEN = 256
N_BRANCH = 4
BRANCH_WIDTH = D_MODEL // N_BRANCH
HEAD_DIM = 128
GM_CHUNK = 128
GM_GROUPS = BRANCH_WIDTH // HEAD_DIM
ATT_HEADS = BRANCH_WIDTH // HEAD_DIM
ATT_KV_HEADS = 2
ATT_REP = ATT_HEADS // ATT_KV_HEADS
KV_WIDTH = ATT_KV_HEADS * HEAD_DIM
ROPE_THETA = 10000.0
CONV_K = 31
ML_HEADS = BRANCH_WIDTH // HEAD_DIM
ML_HEAD_DIM = HEAD_DIM
ML_CHUNK = 128
ML_N_GATES = 4
N_EXPERTS = 16
EXPERT_FF = D_MODEL // 4
CAPACITY_FACTOR = 2
ALPHA = (2 * DEPTH) ** 0.25
EPS = 1e-6
N_MOD = 6

N_TOK = SEQ + CTX_LEN
SEG1 = 9 * BRANCH_WIDTH + 2 * KV_WIDTH
GATE_COLS = ML_N_GATES * ML_HEADS
MERGE_START = SEG1 + GATE_COLS

V7X_VMEM_LIMIT = 56 * 1024 * 1024
LANES = 128

GATE_TOK_TILE = next(t for t in (1408, 1280, 1024, 768, 512, 384, 256, 128) if N_TOK % t == 0)
ROW_BLK = 256
N_ROW_BLKS = N_TOK // ROW_BLK
LAT_ROW_BLKS = SEQ // ROW_BLK

P_TILE = 512
P_KV_TILE_IN = (3 * BRANCH_WIDTH) // P_TILE
P_N_TILES = SEG1 // P_TILE
COL_GM_U, COL_GM_V, COL_ATT_Q = 0, 1, 2
COL_CV_A, COL_CV_B, COL_ML_Q, COL_ML_K, COL_ML_V, COL_ML_O = 3, 4, 5, 6, 7, 8
P_K_COL = 9 * BRANCH_WIDTH


def _cparams(*sem):
    return pltpu.CompilerParams(dimension_semantics=sem, vmem_limit_bytes=V7X_VMEM_LIMIT)


def _sigmoid(x):
    return 1.0 / (1.0 + jnp.exp(-x))


def _gelu_tanh(x):
    return 0.5 * x * (1.0 + jnp.tanh(0.7978845608028654 * (x + 0.044715 * (x * x * x))))


def _ln_rows(z, g, b):
    mu = jnp.mean(z, axis=-1, keepdims=True)
    zc = z - mu
    var = jnp.mean(zc * zc, axis=-1, keepdims=True)
    return zc * lax.rsqrt(var + EPS) * g + b


def _dot_nt(a, b_t):
    return lax.dot_general(a, b_t, (((1,), (1,)), ((), ())), preferred_element_type=F32)


def _pick(ref, is_ctx):
    return jnp.where(is_ctx, ref[1:2, :], ref[0:1, :])


def _mod_kernel(c_ref, w_ref, b_ref, o_ref):
    c = c_ref[...]
    a = (c * _sigmoid(c)).astype(BF16)
    o_ref[...] = jnp.dot(a, w_ref[...].astype(BF16), preferred_element_type=F32) + b_ref[...]


def modulation(cc, w_mod, b_mod, tn=1024):
    L, D, N = w_mod.shape
    return pl.pallas_call(
        _mod_kernel,
        grid=(L, N // tn),
        in_specs=[pl.BlockSpec((16, D), lambda l, j: (0, 0)),
                  pl.BlockSpec((None, D, tn), lambda l, j: (l, 0, j)),
                  pl.BlockSpec((None, 1, tn), lambda l, j: (l, 0, j))],
        out_specs=pl.BlockSpec((None, 16, tn), lambda l, j: (l, 0, j)),
        out_shape=jax.ShapeDtypeStruct((L, 16, N), F32),
        compiler_params=_cparams("parallel", "parallel"),
    )(cc, w_mod, b_mod.reshape(L, 1, N))


def _modulate_kernel(x_ref, sh_ref, sc_ref, o_ref):
    is_ctx = pl.program_id(0) >= LAT_ROW_BLKS
    o_ref[...] = (x_ref[...] * (1.0 + _pick(sc_ref, is_ctx)) + _pick(sh_ref, is_ctx)).astype(o_ref.dtype)


def modulate(x_all, mod_l):
    D = D_MODEL
    return pl.pallas_call(
        _modulate_kernel,
        grid=(N_ROW_BLKS,),
        in_specs=[pl.BlockSpec((ROW_BLK, D), lambda i: (i, 0)),
                  pl.BlockSpec((16, D), lambda i: (0, 0)),
                  pl.BlockSpec((16, D), lambda i: (0, 1))],
        out_specs=pl.BlockSpec((ROW_BLK, D), lambda i: (i, 0)),
        out_shape=jax.ShapeDtypeStruct((N_TOK, D), BF16),
        compiler_params=_cparams("parallel"),
    )(x_all, mod_l, mod_l)


def _proj_kernel(a_ref, w_ref, o_ref, wb_ref):
    @pl.when(pl.program_id(1) == 0)
    def _():
        wb_ref[...] = w_ref[...].astype(BF16)

    o_ref[...] = _dot_nt(a_ref[...], wb_ref[...]).astype(o_ref.dtype)


def _p_out_tile(j):
    return jnp.where(j < P_KV_TILE_IN, j, jnp.where(j == P_KV_TILE_IN, P_N_TILES - 1, j - 1))


def branch_projection(h_all, w_in_t, layer, tm=N_TOK // 8):
    M, K = h_all.shape
    return pl.pallas_call(
        _proj_kernel,
        grid=(P_N_TILES, M // tm),
        in_specs=[pl.BlockSpec((tm, K), lambda j, i: (i, 0)),
                  pl.BlockSpec((None, P_TILE, K), lambda j, i: (layer, j, 0))],
        out_specs=pl.BlockSpec((tm, P_TILE), lambda j, i: (i, _p_out_tile(j))),
        out_shape=jax.ShapeDtypeStruct((M, SEG1), BF16),
        scratch_shapes=[pltpu.VMEM((P_TILE, K), BF16)],
        compiler_params=_cparams("parallel", "arbitrary"),
        name="branch_projection",
    )(h_all, w_in_t)


def _gates_kernel(w_ref, h_ref, bias_ref, o_ref, *, n_chunks):
    z = _dot_nt(w_ref[...].astype(BF16), h_ref[...])
    z = z + bias_ref[:, 0:1]
    lf = jnp.minimum(z, 0.0) - jnp.log1p(jnp.exp(-jnp.abs(z)))
    row = lax.broadcasted_iota(jnp.int32, (ML_CHUNK, ML_CHUNK), 0)
    col = lax.broadcasted_iota(jnp.int32, (ML_CHUNK, ML_CHUNK), 1)
    upper = (row <= col).astype(F32)
    lower = (row >= col).astype(F32)
    H = ML_HEADS
    for c in range(n_chunks):
        sl = slice(c * ML_CHUNK, (c + 1) * ML_CHUNK)
        b_f = jnp.dot(lf[H:2 * H, sl], upper, preferred_element_type=F32, precision=lax.Precision.HIGHEST)
        b_b = jnp.dot(lf[3 * H:4 * H, sl], lower, preferred_element_type=F32, precision=lax.Precision.HIGHEST)
        o_ref[:, sl] = jnp.concatenate([b_f, z[0:H, sl] - b_f, b_b, z[2 * H:3 * H, sl] - b_b], axis=0)


def mlstm_gate_rows(h_all, w_in_t, layer, bias, tt=GATE_TOK_TILE):
    M, K = h_all.shape
    kern = functools.partial(_gates_kernel, n_chunks=tt // ML_CHUNK)
    return pl.pallas_call(
        kern,
        grid=(M // tt,),
        in_specs=[pl.BlockSpec((None, GATE_COLS, K), lambda i: (layer, SEG1 // GATE_COLS, 0)),
                  pl.BlockSpec((tt, K), lambda i: (i, 0)),
                  pl.BlockSpec((GATE_COLS, LANES), lambda i: (0, 0))],
        out_specs=pl.BlockSpec((GATE_COLS, tt), lambda i: (0, i)),
        out_shape=jax.ShapeDtypeStruct((GATE_COLS, M), F32),
        compiler_params=_cparams("parallel"),
    )(w_in_t, h_all, bias)


def _rope_head(x, cos, sin, first_half):
    partner = jnp.where(first_half, pltpu.roll(x, 96, 1), pltpu.roll(x, 32, 1))
    return x * cos + partner * sin


def _qk_kernel(q_ref, kv_ref, cos_ref, sin_ref, qg_ref, kg_ref, qo_ref, ko_ref):
    cos = cos_ref[...]
    sin = sin_ref[...]
    lane = lax.broadcasted_iota(jnp.int32, (ROW_BLK, HEAD_DIM), 1)
    first_half = jnp.bitwise_and(lane, 63) < 32

    def prep(x, gain, scale):
        x = x.astype(F32)
        y = x * lax.rsqrt(jnp.mean(x * x, axis=-1, keepdims=True) + EPS) * gain
        return (_rope_head(y, cos, sin, first_half) * scale).astype(BF16)

    for h in range(ATT_HEADS):
        sl = slice(h * HEAD_DIM, (h + 1) * HEAD_DIM)
        qo_ref[:, sl] = prep(q_ref[:, sl], qg_ref[...], HEAD_DIM ** -0.5)
    for h in range(ATT_KV_HEADS):
        sl = slice(h * HEAD_DIM, (h + 1) * HEAD_DIM)
        ko_ref[:, sl] = prep(kv_ref[:, sl], kg_ref[...], 1.0)


def qk_prepare(P, cos_t, sin_t, q_gain, k_gain):
    return pl.pallas_call(
        _qk_kernel,
        grid=(N_ROW_BLKS,),
        in_specs=[pl.BlockSpec((ROW_BLK, BRANCH_WIDTH), lambda i: (i, COL_ATT_Q)),
                  pl.BlockSpec((ROW_BLK, P_TILE), lambda i: (i, P_N_TILES - 1)),
                  pl.BlockSpec((ROW_BLK, HEAD_DIM), lambda i: (i, 0)),
                  pl.BlockSpec((ROW_BLK, HEAD_DIM), lambda i: (i, 0)),
                  pl.BlockSpec((1, HEAD_DIM), lambda i: (0, 0)),
                  pl.BlockSpec((1, HEAD_DIM), lambda i: (0, 0))],
        out_specs=[pl.BlockSpec((ROW_BLK, BRANCH_WIDTH), lambda i: (i, 0)),
                   pl.BlockSpec((ROW_BLK, KV_WIDTH), lambda i: (i, 0))],
        out_shape=[jax.ShapeDtypeStruct((N_TOK, BRANCH_WIDTH), BF16),
                   jax.ShapeDtypeStruct((N_TOK, KV_WIDTH), BF16)],
        compiler_params=_cparams("parallel"),
    )(P, P, cos_t, sin_t, q_gain.reshape(1, HEAD_DIM), k_gain.reshape(1, HEAD_DIM))


def _attn_kernel(q_ref, k1_ref, v1_ref, k2_ref, v2_ref, o_ref, *, tq, ck, n_chunks):
    rows = ATT_REP * tq
    q = jnp.concatenate([q_ref[:, h * HEAD_DIM:(h + 1) * HEAD_DIM] for h in range(ATT_REP)], axis=0)

    def step(k, v, carry):
        m, l, acc = carry
        s = lax.dot_general(q, k, (((1,), (1,)), ((), ())), preferred_element_type=F32)
        m_new = jnp.maximum(m, jnp.max(s, axis=-1, keepdims=True))
        p = jnp.exp(s - m_new)
        a = jnp.exp(m - m_new)
        l = a * l + jnp.sum(p, axis=-1, keepdims=True)
        acc = a * acc + jnp.dot(p.astype(BF16), v, preferred_element_type=F32)
        return m_new, l, acc

    def body(c, carry):
        off = pl.multiple_of(c * ck, ck)
        return step(k1_ref[pl.ds(off, ck), :], v1_ref[pl.ds(off, ck), :], carry)

    carry = (jnp.full((rows, 1), -jnp.inf, F32), jnp.zeros((rows, 1), F32),
             jnp.zeros((rows, HEAD_DIM), F32))
    if n_chunks:
        carry = lax.fori_loop(0, n_chunks, body, carry, unroll=min(8, n_chunks))
    _, l, acc = step(k2_ref[...], v2_ref[...], carry)
    o = acc / l
    for h in range(ATT_REP):
        o_ref[:, h * HEAD_DIM:(h + 1) * HEAD_DIM] = o[h * tq:(h + 1) * tq].astype(o_ref.dtype)


def attention(q, k, P, latent, tq=128, ck=1024):
    gw = ATT_REP * HEAD_DIM
    v_col = (P_K_COL + KV_WIDTH) // HEAD_DIM
    ctx_blk = SEQ // CTX_LEN
    n_rows = SEQ if latent else CTX_LEN
    q_blk0 = 0 if latent else SEQ // tq
    kern = functools.partial(_attn_kernel, tq=tq, ck=ck, n_chunks=SEQ // ck if latent else 0)
    return pl.pallas_call(
        kern,
        grid=(ATT_KV_HEADS, n_rows // tq),
        in_specs=[pl.BlockSpec((tq, gw), lambda g, i: (q_blk0 + i, g)),
                  pl.BlockSpec((SEQ, HEAD_DIM), lambda g, i: (0, g)),
                  pl.BlockSpec((SEQ, HEAD_DIM), lambda g, i: (0, v_col + g)),
                  pl.BlockSpec((CTX_LEN, HEAD_DIM), lambda g, i: (ctx_blk, g)),
                  pl.BlockSpec((CTX_LEN, HEAD_DIM), lambda g, i: (ctx_blk, v_col + g))],
        out_specs=pl.BlockSpec((tq, gw), lambda g, i: (i, g)),
        out_shape=jax.ShapeDtypeStruct((n_rows, ATT_HEADS * HEAD_DIM), BF16),
        compiler_params=_cparams("parallel", "parallel"),
        name="gqa_attention",
    )(q, k, P, k, P)


def _gmlp_kernel(u_ref, v_ref, g_ref, b_ref, ws_ref, bs_ref, o_ref):
    u = _gelu_tanh(u_ref[...].astype(F32))
    v = _ln_rows(_gelu_tanh(v_ref[...].astype(F32)), g_ref[...], b_ref[...]).astype(BF16)
    for c in range(ROW_BLK // GM_CHUNK):
        rs = slice(c * GM_CHUNK, (c + 1) * GM_CHUNK)
        for g in range(GM_GROUPS):
            cs = slice(g * HEAD_DIM, (g + 1) * HEAD_DIM)
            s = jnp.dot(ws_ref[g], v[rs, cs], preferred_element_type=F32) + bs_ref[g]
            o_ref[rs, cs] = (u[rs, cs] * s).astype(o_ref.dtype)


def chunk_gmlp(P, ln_g, ln_b, w_s, b_s, n_rows):
    bs_rep = jnp.broadcast_to(b_s[:, :, None], (GM_GROUPS, GM_CHUNK, HEAD_DIM))
    BW = BRANCH_WIDTH
    return pl.pallas_call(
        _gmlp_kernel,
        grid=(n_rows // ROW_BLK,),
        in_specs=[pl.BlockSpec((ROW_BLK, BW), lambda i: (i, COL_GM_U)),
                  pl.BlockSpec((ROW_BLK, BW), lambda i: (i, COL_GM_V)),
                  pl.BlockSpec((1, BW), lambda i: (0, 0)),
                  pl.BlockSpec((1, BW), lambda i: (0, 0)),
                  pl.BlockSpec((GM_GROUPS, GM_CHUNK, GM_CHUNK), lambda i: (0, 0, 0)),
                  pl.BlockSpec((GM_GROUPS, GM_CHUNK, HEAD_DIM), lambda i: (0, 0, 0))],
        out_specs=pl.BlockSpec((ROW_BLK, BW), lambda i: (i, 0)),
        out_shape=jax.ShapeDtypeStruct((n_rows, BW), BF16),
        compiler_params=_cparams("parallel"),
    )(P, P, ln_g.reshape(1, BW), ln_b.reshape(1, BW), w_s.astype(BF16), bs_rep)


CONV_HALO = 16
CONV_SUB = 32


def _conv_kernel(a_ref, b_ref, ap_ref, bp_ref, an_ref, bn_ref, w_ref, cb_ref, g_ref, beta_ref, o_ref, gbuf):
    i = pl.program_id(0)

    def glu(a, b):
        return a.astype(F32) * _sigmoid(b.astype(F32))

    prev_ok = jnp.logical_and(i != 0, i != LAT_ROW_BLKS)
    next_ok = jnp.logical_and(i != LAT_ROW_BLKS - 1, i != N_ROW_BLKS - 1)
    gbuf[0:CONV_HALO, :] = jnp.where(prev_ok, glu(ap_ref[...], bp_ref[...]), 0.0)
    gbuf[CONV_HALO:CONV_HALO + ROW_BLK, :] = glu(a_ref[...], b_ref[...])
    gbuf[CONV_HALO + ROW_BLK:, :] = jnp.where(next_ok, glu(an_ref[...], bn_ref[...]), 0.0)
    first = CONV_HALO - CONV_K // 2
    for r in range(ROW_BLK // CONV_SUB):
        base = r * CONV_SUB
        acc = jnp.zeros((CONV_SUB, BRANCH_WIDTH), F32)
        for k in range(CONV_K):
            acc = acc + gbuf[base + first + k:base + first + k + CONV_SUB, :] * w_ref[k:k + 1, :]
        y = _ln_rows(acc + cb_ref[...], g_ref[...], beta_ref[...])
        o_ref[base:base + CONV_SUB, :] = (y * _sigmoid(y)).astype(o_ref.dtype)


def conformer_conv(P, conv_w, conv_b, ln_g, ln_b, n_rows):
    BW = BRANCH_WIDTH
    hpb = ROW_BLK // CONV_HALO
    last_halo = N_TOK // CONV_HALO - 1

    def prev_map(col):
        return lambda i: (jnp.maximum(i * hpb - 1, 0), col)

    def next_map(col):
        return lambda i: (jnp.minimum((i + 1) * hpb, last_halo), col)

    vec = pl.BlockSpec((1, BW), lambda i: (0, 0))
    return pl.pallas_call(
        _conv_kernel,
        grid=(n_rows // ROW_BLK,),
        in_specs=[pl.BlockSpec((ROW_BLK, BW), lambda i: (i, COL_CV_A)),
                  pl.BlockSpec((ROW_BLK, BW), lambda i: (i, COL_CV_B)),
                  pl.BlockSpec((CONV_HALO, BW), prev_map(COL_CV_A)),
                  pl.BlockSpec((CONV_HALO, BW), prev_map(COL_CV_B)),
                  pl.BlockSpec((CONV_HALO, BW), next_map(COL_CV_A)),
                  pl.BlockSpec((CONV_HALO, BW), next_map(COL_CV_B)),
                  pl.BlockSpec((CONV_K + 1, BW), lambda i: (0, 0)),
                  vec, vec, vec],
        out_specs=pl.BlockSpec((ROW_BLK, BW), lambda i: (i, 0)),
        out_shape=jax.ShapeDtypeStruct((n_rows, BW), BF16),
        scratch_shapes=[pltpu.VMEM((ROW_BLK + 2 * CONV_HALO, BW), F32)],
        compiler_params=_cparams("parallel"),
    )(P, P, P, P, P, P, jnp.pad(conv_w, ((0, 1), (0, 0))), conv_b.reshape(1, BW),
      ln_g.reshape(1, BW), ln_b.reshape(1, BW))


MW_TILE = 512
MW_SHIFT = MERGE_START % MW_TILE


def _merge_weights_kernel(a_ref, b_ref, o_ref):
    o_ref[0:MW_TILE - MW_SHIFT, :] = a_ref[MW_SHIFT:, :].astype(o_ref.dtype)
    o_ref[MW_TILE - MW_SHIFT:, :] = b_ref[...].astype(o_ref.dtype)


def merge_weights(w_in_t, layer):
    D = D_MODEL
    a0 = (MERGE_START - MW_SHIFT) // MW_TILE
    b0 = (MERGE_START - MW_SHIFT) // MW_SHIFT
    return pl.pallas_call(
        _merge_weights_kernel,
        grid=(N_BRANCH * D // MW_TILE,),
        in_specs=[pl.BlockSpec((None, MW_TILE, D), lambda j: (layer, a0 + j, 0)),
                  pl.BlockSpec((None, MW_SHIFT, D), lambda j: (layer, b0 + (j + 1) * (MW_TILE // MW_SHIFT), 0))],
        out_specs=pl.BlockSpec((MW_TILE, D), lambda j: (j, 0)),
        out_shape=jax.ShapeDtypeStruct((N_BRANCH * D, D), BF16),
        compiler_params=_cparams("parallel"),
        name="merge_weights",
    )(w_in_t, w_in_t)


def _merge_kernel(h_ref, b0, b1, b2, b3, m0, m1, m2, m3, w0, w1, w2, w3, o_ref):
    h = h_ref[...]
    acc = None
    for br, wm, wb in ((b0, m0, w0), (b1, m1, w1), (b2, m2, w2), (b3, m3, w3)):
        gate = _sigmoid(_dot_nt(h, wm[...]))
        t = gate * jnp.dot(br[...], wb[...], preferred_element_type=F32)
        acc = t if acc is None else acc + t
    o_ref[...] = acc.astype(o_ref.dtype)


def merge_branches(h_all, branches, w_merge, w_branch, n_rows, tm, tn=256):
    D = D_MODEL
    BW = BRANCH_WIDTH
    nj = D // tn
    one = pl.Buffered(1)
    in_specs = [pl.BlockSpec((tm, D), lambda i, j: (i, 0), pipeline_mode=one)]
    in_specs += [pl.BlockSpec((tm, BW), lambda i, j: (i, 0), pipeline_mode=one) for _ in range(N_BRANCH)]
    in_specs += [pl.BlockSpec((tn, D), functools.partial(lambda i, j, b: (b * nj + j, 0), b=b))
                 for b in range(N_BRANCH)]
    in_specs += [pl.BlockSpec((None, BW, tn), functools.partial(lambda i, j, b: (b, 0, j), b=b))
                 for b in range(N_BRANCH)]
    return pl.pallas_call(
        _merge_kernel,
        grid=(n_rows // tm, nj),
        in_specs=in_specs,
        out_specs=pl.BlockSpec((tm, tn), lambda i, j: (i, j)),
        out_shape=jax.ShapeDtypeStruct((n_rows, D), BF16),
        compiler_params=_cparams("parallel", "arbitrary"),
        name="gated_merge",
    )(h_all, *branches, *([w_merge] * N_BRANCH), *([w_branch] * N_BRANCH))


def _out_ln_kernel(y_ref, w_ref, x_ref, g1_ref, lg_ref, lb_ref, sh_ref, sc_ref, x1_ref, hm_ref, *, nj, tm, tn):
    j = pl.program_id(1)
    cols = pl.ds(pl.multiple_of(j * tn, tn), tn)
    row_is_ctx = pl.program_id(0) * tm + lax.broadcasted_iota(jnp.int32, (tm, 1), 0) >= SEQ
    g1 = jnp.where(row_is_ctx, g1_ref[1:2, cols], g1_ref[0:1, cols])
    part = jnp.dot(y_ref[...], w_ref[...], preferred_element_type=F32)
    x1_ref[:, cols] = ALPHA * x_ref[...] + g1 * part

    @pl.when(j == nj - 1)
    def _():
        sub = next(s for s in (64, 48, 32, 16) if tm % s == 0)

        def chunk(r, carry):
            r0 = pl.multiple_of(r * sub, 16)
            rows = pl.ds(r0, sub)
            is_ctx = pl.program_id(0) * tm + r0 + lax.broadcasted_iota(jnp.int32, (sub, 1), 0) >= SEQ
            x1 = _ln_rows(x1_ref[rows, :], lg_ref[...], lb_ref[...])
            x1_ref[rows, :] = x1
            hm_ref[rows, :] = (x1 * (1.0 + _pick(sc_ref, is_ctx)) + _pick(sh_ref, is_ctx)).astype(hm_ref.dtype)
            return carry

        lax.fori_loop(0, tm // sub, chunk, 0)


def out_proj_ln(y, w_out, x_all, mod_l, ln_g, ln_b, n_rows, tm, tn=512):
    D = D_MODEL
    nj = D // tn
    kern = functools.partial(_out_ln_kernel, nj=nj, tm=tm, tn=tn)
    vec = pl.BlockSpec((1, D), lambda i, j: (0, 0))

    def modspec(col):
        return pl.BlockSpec((16, D), lambda i, j: (0, col))

    return pl.pallas_call(
        kern,
        grid=(n_rows // tm, nj),
        in_specs=[pl.BlockSpec((tm, D), lambda i, j: (i, 0)),
                  pl.BlockSpec((D, tn), lambda i, j: (0, j)),
                  pl.BlockSpec((tm, tn), lambda i, j: (i, j)),
                  modspec(2), vec, vec, modspec(3), modspec(4)],
        out_specs=[pl.BlockSpec((tm, D), lambda i, j: (i, 0)),
                   pl.BlockSpec((tm, D), lambda i, j: (i, 0))],
        out_shape=[jax.ShapeDtypeStruct((n_rows, D), F32),
                   jax.ShapeDtypeStruct((n_rows, D), BF16)],
        compiler_params=_cparams("parallel", "arbitrary"),
        name="out_proj_ln1",
    )(y, w_out, x_all, mod_l, ln_g.reshape(1, D), ln_b.reshape(1, D), mod_l, mod_l)


def _ln2_kernel(x_ref, y_ref, g2_ref, lg_ref, lb_ref, sh_ref, sc_ref, x2_ref, *h_ref, blk_ctx_from):
    is_ctx = pl.program_id(0) >= blk_ctx_from
    z = ALPHA * x_ref[...] + _pick(g2_ref, is_ctx) * y_ref[...]
    x2 = _ln_rows(z, lg_ref[...], lb_ref[...])
    x2_ref[...] = x2
    if h_ref:
        h_ref[0][...] = (x2 * (1.0 + _pick(sc_ref, is_ctx)) + _pick(sh_ref, is_ctx)).astype(BF16)


def ffn_residual_ln(x1, y_moe, mod_l, ln_g, ln_b, mod_next, n_rows):
    D = D_MODEL
    want_h = mod_next is not None
    nxt = mod_next if want_h else mod_l
    kern = functools.partial(_ln2_kernel, blk_ctx_from=LAT_ROW_BLKS)
    row = pl.BlockSpec((ROW_BLK, D), lambda i: (i, 0))
    vec = pl.BlockSpec((1, D), lambda i: (0, 0))
    out_specs = [row, row] if want_h else [row]
    out_shape = [jax.ShapeDtypeStruct((n_rows, D), F32)]
    if want_h:
        out_shape.append(jax.ShapeDtypeStruct((n_rows, D), BF16))
    return pl.pallas_call(
        kern,
        grid=(n_rows // ROW_BLK,),
        in_specs=[row, row, pl.BlockSpec((16, D), lambda i: (0, 5)), vec, vec,
                  pl.BlockSpec((16, D), lambda i: (0, 0)), pl.BlockSpec((16, D), lambda i: (0, 1))],
        out_specs=out_specs,
        out_shape=out_shape,
        compiler_params=_cparams("parallel"),
    )(x1, y_moe, mod_l, ln_g.reshape(1, D), ln_b.reshape(1, D), nxt, nxt)


def _tok_tile(n):
    return next(t for t in (1408, 1280, 1024, 768, 512, 384, 256, 128) if n % t == 0)


def _router_kernel(w_ref, h_ref, o_ref):
    z = _dot_nt(w_ref[...], h_ref[...])
    ez = jnp.exp(z - jnp.max(z, axis=0, keepdims=True))
    o_ref[...] = ez / jnp.sum(ez, axis=0, keepdims=True)


def router_affinities(hm, w_router_t, n_rows):
    D = D_MODEL
    tt = _tok_tile(n_rows)
    return pl.pallas_call(
        _router_kernel,
        grid=(n_rows // tt,),
        in_specs=[pl.BlockSpec((N_EXPERTS, D), lambda i: (0, 0)),
                  pl.BlockSpec((tt, D), lambda i: (i, 0))],
        out_specs=pl.BlockSpec((N_EXPERTS, tt), lambda i: (0, i)),
        out_shape=jax.ShapeDtypeStruct((N_EXPERTS, n_rows), F32),
        compiler_params=_cparams("parallel"),
        name="router_affinities",
    )(w_router_t, hm)


def _select_kernel(aff_ref, idx_ref, cum_ref, *, T, cap):
    E = N_EXPERTS
    nc = T // LANES
    bits = pltpu.bitcast(aff_ref[...], jnp.int32)

    def bisect(_, lohi):
        lo, hi = lohi
        mid = lo + ((hi - lo + 1) >> 1)
        cnt = jnp.sum((bits >= mid).astype(F32), axis=1, keepdims=True)
        ok = cnt >= cap
        return jnp.where(ok, mid, lo), jnp.where(ok, hi, mid - 1)

    thr, _ = lax.fori_loop(0, 31, bisect, (jnp.zeros((E, 1), jnp.int32), jnp.full((E, 1), 0x7F800000, jnp.int32)))
    gt = (bits > thr).astype(F32)
    eq = (bits == thr).astype(F32)
    need = cap - jnp.sum(gt, axis=1, keepdims=True)
    row = lax.broadcasted_iota(jnp.int32, (LANES, LANES), 0)
    col = lax.broadcasted_iota(jnp.int32, (LANES, LANES), 1)
    before = (row < col).astype(BF16)
    upto = (row <= col).astype(BF16)
    seen_eq = jnp.zeros((E, 1), F32)
    seen_sel = jnp.zeros((E, 1), F32)
    for c in range(nc):
        sl = slice(c * LANES, (c + 1) * LANES)
        eq_c = eq[:, sl]
        rank_eq = jnp.dot(eq_c.astype(BF16), before, preferred_element_type=F32) + seen_eq
        seen_eq = seen_eq + jnp.sum(eq_c, axis=1, keepdims=True)
        sel_c = jnp.maximum(gt[:, sl], jnp.where(rank_eq < need, eq_c, 0.0))
        cum_ref[:, sl] = jnp.dot(sel_c.astype(BF16), upto, preferred_element_type=F32) + seen_sel
        seen_sel = seen_sel + jnp.sum(sel_c, axis=1, keepdims=True)

    sb = min(cap, 256)
    lane = lax.broadcasted_iota(jnp.int32, (sb, LANES), 1)
    for b in range(cap // sb):
        slot = (b * sb + lax.broadcasted_iota(jnp.int32, (sb, LANES), 0)).astype(F32)
        res = jnp.zeros((sb, LANES), F32)
        for e in range(E):
            def count(c, acc):
                cum = cum_ref[e:e + 1, pl.ds(pl.multiple_of(c * LANES, LANES), LANES)]
                return acc + jnp.where(cum <= slot, 1.0, 0.0)

            acc = lax.fori_loop(0, nc, count, jnp.zeros((sb, LANES), F32))
            res = jnp.where(lane == e, jnp.sum(acc, axis=1, keepdims=True), res)
        idx_ref[b * sb:(b + 1) * sb, :] = res.astype(jnp.int32)


def expert_choice_indices(aff, first_tok, T, cap):
    kern = functools.partial(_select_kernel, T=T, cap=cap)
    out = pl.pallas_call(
        kern,
        grid=(1,),
        in_specs=[pl.BlockSpec((N_EXPERTS, T), lambda i: (0, first_tok // T))],
        out_specs=pl.BlockSpec((cap, LANES), lambda i: (0, 0)),
        out_shape=jax.ShapeDtypeStruct((cap, LANES), jnp.int32),
        scratch_shapes=[pltpu.VMEM((N_EXPERTS, T), F32)],
        compiler_params=_cparams("arbitrary"),
        name="expert_choice_select",
    )(aff)
    return out[:, :N_EXPERTS].T


def _gather_swiglu_kernel(idx_ref, nxt_ref, x_hbm, sh_ref, sc_ref, wr_ref, wg_ref, wu_ref, o_ref, gain_ref,
                          xbuf, xe, sem, *, M, cap_l):
    e = pl.program_id(0)
    j = pl.program_id(1)
    n_e = pl.num_programs(0)

    def row_copy(ref, r):
        return pltpu.make_async_copy(x_hbm.at[pl.ds(ref[0, r], 1), :], xbuf.at[pl.ds(r, 1), :], sem.at[0])

    def each_row(fn):
        def body(r, carry):
            fn(r)
            return carry
        lax.fori_loop(0, M, body, 0, unroll=8)

    @pl.when(j == 0)
    def _():
        @pl.when(e == 0)
        def _():
            each_row(lambda r: row_copy(idx_ref, r).start())

        each_row(lambda r: row_copy(idx_ref, r).wait())
        xe[0:cap_l, :] = (xbuf[0:cap_l, :] * (1.0 + sc_ref[0:1, :]) + sh_ref[0:1, :]).astype(BF16)
        if M > cap_l:
            xe[cap_l:, :] = (xbuf[cap_l:, :] * (1.0 + sc_ref[1:2, :]) + sh_ref[1:2, :]).astype(BF16)

        @pl.when(e + 1 < n_e)
        def _():
            each_row(lambda r: row_copy(nxt_ref, r).start())

        z = jnp.dot(xe[...], wr_ref[...], preferred_element_type=F32)
        lane = lax.broadcasted_iota(jnp.int32, z.shape, 1)
        z = jnp.where(lane < N_EXPERTS, z, -jnp.inf)
        ez = jnp.exp(z - jnp.max(z, axis=1, keepdims=True))
        gain_ref[...] = (jnp.sum(jnp.where(lane == e, ez, 0.0), axis=1, keepdims=True)
                         / jnp.sum(ez, axis=1, keepdims=True))

    x = xe[...]
    g = jnp.dot(x, wg_ref[...].astype(BF16), preferred_element_type=F32)
    u = jnp.dot(x, wu_ref[...].astype(BF16), preferred_element_type=F32)
    o_ref[...] = (g * _sigmoid(g) * u).astype(o_ref.dtype)


def expert_gather_swiglu(idx, x1, mod_l, w_router_pad, w_gate, w_up, layer, cap_l, tn=256):
    E, M = idx.shape
    D = D_MODEL
    N = w_gate.shape[-1]
    idx3 = idx.reshape(E, 1, M)
    wspec = pl.BlockSpec((None, None, D, tn), lambda e, j: (layer, e, 0, j))
    kern = functools.partial(_gather_swiglu_kernel, M=M, cap_l=cap_l)
    return pl.pallas_call(
        kern,
        grid=(E, N // tn),
        in_specs=[pl.BlockSpec((None, 1, M), lambda e, j: (e, 0, 0), memory_space=pltpu.SMEM),
                  pl.BlockSpec((None, 1, M), lambda e, j: (jnp.minimum(e + 1, E - 1), 0, 0), memory_space=pltpu.SMEM),
                  pl.BlockSpec(memory_space=pl.ANY),
                  pl.BlockSpec((16, D), lambda e, j: (0, 3)),
                  pl.BlockSpec((16, D), lambda e, j: (0, 4)),
                  pl.BlockSpec((D, LANES), lambda e, j: (0, 0)),
                  wspec, wspec],
        out_specs=[pl.BlockSpec((None, M, tn), lambda e, j: (e, 0, j)),
                   pl.BlockSpec((None, M, 1), lambda e, j: (e, 0, 0))],
        out_shape=[jax.ShapeDtypeStruct((E, M, N), BF16),
                   jax.ShapeDtypeStruct((E, M, 1), F32)],
        scratch_shapes=[pltpu.VMEM((M, D), F32), pltpu.VMEM((M, D), BF16), pltpu.SemaphoreType.DMA((1,))],
        compiler_params=_cparams("arbitrary", "arbitrary"),
        name="expert_gather_swiglu",
    )(idx3, idx3, x1, mod_l, mod_l, w_router_pad, w_gate, w_up)


def _down_kernel(h_ref, w_ref, g_ref, o_ref):
    y = jnp.dot(h_ref[...], w_ref[...].astype(BF16), preferred_element_type=F32)
    o_ref[...] = y * g_ref[...]


def expert_down(hid, w_down, gains, layer, tn=1024):
    E, M, K = hid.shape
    N = w_down.shape[-1]
    return pl.pallas_call(
        _down_kernel,
        grid=(E, N // tn),
        in_specs=[pl.BlockSpec((None, M, K), lambda e, j: (e, 0, 0)),
                  pl.BlockSpec((None, None, K, tn), lambda e, j: (layer, e, 0, j)),
                  pl.BlockSpec((None, M, 1), lambda e, j: (e, 0, 0))],
        out_specs=pl.BlockSpec((None, M, tn), lambda e, j: (e, 0, j)),
        out_shape=jax.ShapeDtypeStruct((E, M, N), F32),
        compiler_params=_cparams("parallel", "parallel"),
    )(hid, w_down, gains)


def _scatter_kernel(idx_ref, ye_ref, acc_in, acc_ref, buf, gsem, ssem, *, R, n_chunks):
    del acc_in
    c = pl.program_id(1)

    def gather(chunk, slot, r):
        t = idx_ref[0, chunk * R + r]
        return pltpu.make_async_copy(acc_ref.at[pl.ds(t, 1), :], buf.at[slot, pl.ds(r, 1), :], gsem.at[slot])

    def scatter(chunk, slot, r):
        t = idx_ref[0, chunk * R + r]
        return pltpu.make_async_copy(buf.at[slot, pl.ds(r, 1), :], acc_ref.at[pl.ds(t, 1), :], ssem.at[slot])

    def each_row(fn):
        def body(r, carry):
            fn(r)
            return carry
        lax.fori_loop(0, R, body, 0, unroll=8)

    slot = c % 2

    @pl.when(c == 0)
    def _():
        each_row(lambda r: gather(0, 0, r).start())

    each_row(lambda r: gather(c, slot, r).wait())

    @pl.when(c >= 1)
    def _():
        each_row(lambda r: scatter(c - 1, 1 - slot, r).wait())

    @pl.when(c + 1 < n_chunks)
    def _():
        each_row(lambda r: gather(c + 1, 1 - slot, r).start())

    buf[slot] = buf[slot] + ye_ref[...]
    each_row(lambda r: scatter(c, slot, r).start())

    @pl.when(c == n_chunks - 1)
    def _():
        each_row(lambda r: scatter(c, slot, r).wait())


def scatter_add_rows(idx, ye, n_rows):
    E, M, D = ye.shape
    R = next(r for r in (264, 256, 160, 128, 64, 32) if M % r == 0)
    n_chunks = M // R
    kern = functools.partial(_scatter_kernel, R=R, n_chunks=n_chunks)
    return pl.pallas_call(
        kern,
        grid=(E, n_chunks),
        in_specs=[pl.BlockSpec((None, 1, M), lambda e, c: (e, 0, 0), memory_space=pltpu.SMEM),
                  pl.BlockSpec((None, R, D), lambda e, c: (e, c, 0)),
                  pl.BlockSpec(memory_space=pl.ANY)],
        out_specs=pl.BlockSpec(memory_space=pl.ANY),
        out_shape=jax.ShapeDtypeStruct((n_rows, D), F32),
        scratch_shapes=[pltpu.VMEM((2, R, D), F32), pltpu.SemaphoreType.DMA((2,)), pltpu.SemaphoreType.DMA((2,))],
        input_output_aliases={2: 0},
        compiler_params=_cparams("arbitrary", "arbitrary"),
        name="scatter_add_rows",
    )(idx.reshape(E, 1, M), ye, jnp.zeros((n_rows, D), F32))


N_CHUNKS = N_TOK // ML_CHUNK
LAT_CHUNKS = SEQ // ML_CHUNK
CTX_CHUNKS = CTX_LEN // ML_CHUNK


def _mlstm_kernel(q_ref, k_ref, v_ref, og_ref, g_ref, ng_ref, out_ref, acc_ref, c_ref, n_ref, m_ref, *, hb, out_chunks):
    L, d = ML_CHUNK, ML_HEAD_DIM
    row = lax.broadcasted_iota(jnp.int32, (L, L), 0)
    col = lax.broadcasted_iota(jnp.int32, (L, L), 1)
    eye = row == col
    masks = (col <= row, col >= row)
    lasts = (L - 1, 0)
    k_scale = ML_HEAD_DIM ** -0.5

    acc_ref[...] = jnp.zeros(acc_ref.shape, F32)
    c_ref[...] = jnp.zeros(c_ref.shape, F32)
    n_ref[...] = jnp.zeros(n_ref.shape, F32)
    m_ref[...] = jnp.zeros(m_ref.shape, F32)

    def col_of(r):
        return jnp.sum(jnp.where(eye, r, 0.0), axis=1, keepdims=True)

    def step(t, carry):
        chunk = (jnp.where(t < CTX_CHUNKS, t + LAT_CHUNKS, t - CTX_CHUNKS), N_CHUNKS - 1 - t)
        for j in range(hb):
            hs = slice(j * d, (j + 1) * d)
            for dr in range(2):
                si = 2 * j + dr
                off = pl.multiple_of(chunk[dr] * L, L)
                rows = pl.ds(off, L)
                q = q_ref[rows, hs]
                kf = k_ref[rows, hs].astype(F32) * k_scale
                v = v_ref[rows, hs]
                b_row = g_ref[j, 2 * dr:2 * dr + 1, rows]
                a_row = g_ref[j, 2 * dr + 1:2 * dr + 2, rows]
                last = lasts[dr]
                m = m_ref[si][0:1, 0:1]
                n = n_ref[si][0:1, :]
                C = c_ref[si]
                b_col = col_of(b_row)
                a_col = col_of(a_row)
                dlog = jnp.where(masks[dr], b_col + a_row, -jnp.inf)
                inter = b_col + m
                mj = jnp.maximum(inter, jnp.max(dlog, axis=1, keepdims=True))
                w_inter = jnp.exp(inter - mj)
                qk = lax.dot_general(q, kf.astype(BF16), (((1,), (1,)), ((), ())), preferred_element_type=F32)
                s = qk * jnp.exp(dlog - mj)
                num = (w_inter * jnp.dot(q, C.astype(BF16), preferred_element_type=F32)
                       + jnp.dot(s.astype(BF16), v, preferred_element_type=F32))
                qn = jnp.sum(q.astype(F32) * n, axis=1, keepdims=True)
                den = w_inter * qn + jnp.sum(s, axis=1, keepdims=True)
                h = num / jnp.maximum(jnp.abs(den), jnp.exp(-mj))
                acc_ref[rows, hs] += h
                m_new = mj[last:last + 1, :]
                b_end = b_row[:, last:last + 1]
                w_c = jnp.exp(b_end + m - m_new)
                kw = kf * jnp.exp(b_end + a_col - m_new)
                c_ref[si] = w_c * C + lax.dot_general(kw.astype(BF16), v, (((0,), (0,)), ((), ())),
                                                      preferred_element_type=F32)
                n_ref[si] = jnp.broadcast_to(w_c * n + jnp.sum(kw, axis=0, keepdims=True), (8, d))
                m_ref[si] = jnp.broadcast_to(m_new, (8, LANES))
        return carry

    lax.fori_loop(0, N_CHUNKS, step, 0)

    def finish(c, carry):
        rows = pl.ds(pl.multiple_of(c * L, L), L)
        for j in range(hb):
            hs = slice(j * d, (j + 1) * d)
            hh = acc_ref[rows, hs]
            hc = hh - jnp.mean(hh, axis=1, keepdims=True)
            hn = hc * lax.rsqrt(jnp.mean(hc * hc, axis=1, keepdims=True) + EPS) * ng_ref[j]
            out_ref[rows, hs] = (_sigmoid(og_ref[rows, hs].astype(F32)) * hn).astype(out_ref.dtype)
        return carry

    lax.fori_loop(0, out_chunks, finish, 0)


def mlstm(P, G, norm_g, n_rows, hb=2):
    w = hb * ML_HEAD_DIM
    per = BRANCH_WIDTH // w

    def colspec(col):
        return pl.BlockSpec((N_TOK, w), lambda i: (0, col * per + i))

    kern = functools.partial(_mlstm_kernel, hb=hb, out_chunks=n_rows // ML_CHUNK)
    return pl.pallas_call(
        kern,
        grid=(ML_HEADS // hb,),
        in_specs=[colspec(COL_ML_Q), colspec(COL_ML_K), colspec(COL_ML_V),
                  pl.BlockSpec((n_rows, w), lambda i: (0, COL_ML_O * per + i)),
                  pl.BlockSpec((hb, 4, N_TOK), lambda i: (i, 0, 0)),
                  pl.BlockSpec((hb, 1, ML_HEAD_DIM), lambda i: (i, 0, 0))],
        out_specs=pl.BlockSpec((n_rows, w), lambda i: (0, i)),
        out_shape=jax.ShapeDtypeStruct((n_rows, BRANCH_WIDTH), BF16),
        scratch_shapes=[pltpu.VMEM((N_TOK, w), F32),
                        pltpu.VMEM((2 * hb, ML_HEAD_DIM, ML_HEAD_DIM), F32),
                        pltpu.VMEM((2 * hb, 8, ML_HEAD_DIM), F32),
                        pltpu.VMEM((2 * hb, 8, LANES), F32)],
        compiler_params=_cparams("parallel"),
        name="mlstm_scan",
    )(P, P, P, P, G, norm_g.reshape(ML_HEADS, 1, ML_HEAD_DIM))


def _norm_stats(x):
    xc = x - jnp.mean(x, axis=-1, keepdims=True)
    return xc * lax.rsqrt(jnp.mean(xc * xc, axis=-1, keepdims=True) + EPS)


def _rope_tables():
    rows = SEQ // GRID_W
    row = jnp.repeat(jnp.arange(rows), GRID_W).astype(F32)
    col = (jnp.arange(SEQ) % GRID_W).astype(F32)
    n_freq = HEAD_DIM // 4
    inv = ROPE_THETA ** (-jnp.arange(n_freq, dtype=F32) / n_freq)
    ang_r = row[:, None] * inv[None, :]
    ang_c = col[:, None] * inv[None, :]
    cos = jnp.concatenate([jnp.cos(ang_r), jnp.cos(ang_r), jnp.cos(ang_c), jnp.cos(ang_c)], axis=1)
    sin = jnp.concatenate([-jnp.sin(ang_r), jnp.sin(ang_r), -jnp.sin(ang_c), jnp.sin(ang_c)], axis=1)
    cos = jnp.concatenate([cos, jnp.ones((CTX_LEN, HEAD_DIM), F32)], axis=0)
    sin = jnp.concatenate([sin, jnp.zeros((CTX_LEN, HEAD_DIM), F32)], axis=0)
    return cos, sin


def _mlstm_chunk_scan(q, k, v, a_row, b_row, state, reverse):
    H, T, d = q.shape
    nc = T // ML_CHUNK

    def chunks(x):
        return jnp.moveaxis(x.reshape(H, nc, ML_CHUNK, *x.shape[2:]), 1, 0)

    tri = jnp.tril(jnp.ones((ML_CHUNK, ML_CHUNK), dtype=bool))
    mask = tri.T if reverse else tri
    last = 0 if reverse else ML_CHUNK - 1

    def step(carry, inp):
        C, n, m = carry
        qc, kc, vc, ac, bc = inp
        inter = bc + m[..., None]
        dlog = jnp.where(mask, bc[..., :, None] + ac[..., None, :], -jnp.inf)
        mj = jnp.maximum(inter, jnp.max(dlog, axis=-1))
        w_inter = jnp.exp(inter - mj)
        s = jnp.einsum('hjd,hsd->hjs', qc, kc) * jnp.exp(dlog - mj[..., None])
        num = (w_inter[..., None] * jnp.einsum('hjd,hde->hje', qc, C)
               + jnp.einsum('hjs,hse->hje', s, vc))
        den = w_inter * jnp.einsum('hjd,hd->hj', qc, n) + jnp.sum(s, axis=-1)
        h = num / jnp.maximum(jnp.abs(den), jnp.exp(-mj))[..., None]
        m_new = mj[..., last]
        b_end = bc[..., last]
        w_c = jnp.exp(b_end + m - m_new)
        w_s = jnp.exp(b_end[..., None] + ac - m_new[..., None])
        C_new = w_c[..., None, None] * C + jnp.einsum('hs,hsd,hse->hde', w_s, kc, vc)
        n_new = w_c[..., None] * n + jnp.einsum('hs,hsd->hd', w_s, kc)
        return (C_new, n_new, m_new), h

    final, h = lax.scan(step, state, (chunks(q), chunks(k), chunks(v), chunks(a_row), chunks(b_row)),
                        reverse=reverse)
    return jnp.moveaxis(h, 0, 1).reshape(H, T, d), final


def _mlstm(P, G, norm_g, rows_out):
    BW = BRANCH_WIDTH

    def heads(col, rows):
        a = P[rows, col * BW:(col + 1) * BW].astype(F32)
        return jnp.swapaxes(a.reshape(-1, ML_HEADS, ML_HEAD_DIM), 0, 1)

    H = ML_HEADS
    zero = (jnp.zeros((H, ML_HEAD_DIM, ML_HEAD_DIM), F32), jnp.zeros((H, ML_HEAD_DIM), F32), jnp.zeros((H,), F32))
    outs = {}
    st_f, st_b = zero, zero
    for name, rows in (('ctx', slice(SEQ, N_TOK)), ('lat', slice(0, SEQ))):
        q = heads(COL_ML_Q, rows)
        k = heads(COL_ML_K, rows) * (ML_HEAD_DIM ** -0.5)
        v = heads(COL_ML_V, rows)
        h_f, st_f = _mlstm_chunk_scan(q, k, v, G[H:2 * H, rows], G[0:H, rows], st_f, False)
        h_b, st_b = _mlstm_chunk_scan(q, k, v, G[3 * H:4 * H, rows], G[2 * H:3 * H, rows], st_b, True)
        outs[name] = h_f + h_b
    h = jnp.concatenate([outs['lat'], outs['ctx']], axis=1)[:, :rows_out]
    hn = _norm_stats(jnp.swapaxes(h, 0, 1)) * norm_g.reshape(H, ML_HEAD_DIM)
    o_pre = P[:rows_out, COL_ML_O * BW:(COL_ML_O + 1) * BW].astype(F32)
    return (jax.nn.sigmoid(o_pre) * hn.reshape(rows_out, BW)).astype(BF16)


def kernel(x, c, ctx, c_ctx, w_mod, b_mod, w_in, att_q_gain, att_k_gain, gm_ln_g, gm_ln_b,
           gm_w_s, gm_b_s, conv_w, conv_b, conv_ln_g, conv_ln_b, ml_gate_bias, ml_norm_g,
           w_branch, w_out, ln1_g, ln1_b, w_router, w_gate, w_up, w_down, ln2_g, ln2_b):
    cos_t, sin_t = _rope_tables()
    w_in_t = jnp.swapaxes(w_in, 1, 2)
    cc = jnp.zeros((16, D_MODEL), F32).at[0].set(c[0]).at[1].set(c_ctx)
    mods = modulation(cc, w_mod, b_mod)
    x_all = jnp.concatenate([x[0], ctx[0]], axis=0)
    h_all = modulate(x_all, mods[0])
    for l in range(DEPTH):
        last = l == DEPTH - 1
        n_rows = SEQ if last else N_TOK
        tm = (SEQ if last else N_TOK) // 8
        mod_l = mods[l]

        P = branch_projection(h_all, w_in_t, l)
        gbias = jnp.broadcast_to(ml_gate_bias[l].reshape(GATE_COLS, 1), (GATE_COLS, LANES))
        G = mlstm_gate_rows(h_all, w_in_t, l, gbias)
        q, k = qk_prepare(P, cos_t, sin_t, att_q_gain[l], att_k_gain[l])

        br_a = chunk_gmlp(P, gm_ln_g[l], gm_ln_b[l], gm_w_s[l], gm_b_s[l], n_rows)
        br_b = attention(q, k, P, True)
        if not last:
            br_b = jnp.concatenate([br_b, attention(q, k, P, False)], axis=0)
        br_c = conformer_conv(P, conv_w[l], conv_b[l], conv_ln_g[l], conv_ln_b[l], n_rows)
        G_heads = G.reshape(ML_N_GATES, ML_HEADS, N_TOK).transpose(1, 0, 2)
        br_d = mlstm(P, G_heads, ml_norm_g[l], n_rows)

        w_merge = merge_weights(w_in_t, l)
        y = merge_branches(h_all, [br_a, br_b, br_c, br_d], w_merge, w_branch[l].astype(BF16), n_rows, tm)
        x1, hm = out_proj_ln(y, w_out[l].astype(BF16), x_all, mod_l, ln1_g[l], ln1_b[l], n_rows, tm // 2)

        cap_l = CAPACITY_FACTOR * SEQ // N_EXPERTS
        aff = router_affinities(hm, w_router[l].T.astype(BF16), n_rows)
        idx = expert_choice_indices(aff, 0, SEQ, cap_l)
        if not last:
            idx_c = expert_choice_indices(aff, SEQ, CTX_LEN, CAPACITY_FACTOR * CTX_LEN // N_EXPERTS)
            idx = jnp.concatenate([idx, idx_c + SEQ], axis=1)
        w_router_pad = jnp.pad(w_router[l], ((0, 0), (0, LANES - N_EXPERTS))).astype(BF16)
        hid, gains = expert_gather_swiglu(idx, x1, mod_l, w_router_pad, w_gate, w_up, l, cap_l)
        ye = expert_down(hid, w_down, gains, l)
        y_moe = scatter_add_rows(idx, ye, n_rows)
        res = ffn_residual_ln(x1, y_moe, mod_l, ln2_g[l], ln2_b[l], None if last else mods[l + 1], n_rows)
        if not last:
            x_all, h_all = res
        else:
            x_all = res[0]
    return x_all[None]
```

```python
import functools

import jax
import jax.numpy as jnp
from jax import lax
from jax.experimental import pallas as pl
from jax.experimental.pallas import tpu as pltpu

F32 = jnp.float32
BF16 = jnp.bfloat16

D_MODEL = 4096
SEQ = 8192
DEPTH = 2
GRID_W = 64
CTX_LEN = 256
N_BRANCH = 4
BRANCH_WIDTH = D_MODEL // N_BRANCH
HEAD_DIM = 128
GM_CHUNK = 128
GM_GROUPS = BRANCH_WIDTH // HEAD_DIM
ATT_HEADS = BRANCH_WIDTH // HEAD_DIM
ATT_KV_HEADS = 2
ATT_REP = ATT_HEADS // ATT_KV_HEADS
KV_WIDTH = ATT_KV_HEADS * HEAD_DIM
ROPE_THETA = 10000.0
CONV_K = 31
ML_HEADS = BRANCH_WIDTH // HEAD_DIM
ML_HEAD_DIM = HEAD_DIM
ML_CHUNK = 128
ML_N_GATES = 4
N_EXPERTS = 16
EXPERT_FF = D_MODEL // 4
CAPACITY_FACTOR = 2
ALPHA = (2 * DEPTH) ** 0.25
EPS = 1e-6
N_MOD = 6

N_TOK = SEQ + CTX_LEN
SEG1 = 9 * BRANCH_WIDTH + 2 * KV_WIDTH
GATE_COLS = ML_N_GATES * ML_HEADS
MERGE_START = SEG1 + GATE_COLS

V7X_VMEM_LIMIT = 56 * 1024 * 1024
LANES = 128

GATE_TOK_TILE = next(t for t in (1408, 1280, 1024, 768, 512, 384, 256, 128) if N_TOK % t == 0)
ROW_BLK = 256
N_ROW_BLKS = N_TOK // ROW_BLK
LAT_ROW_BLKS = SEQ // ROW_BLK

P_TILE = 512
P_KV_TILE_IN = (3 * BRANCH_WIDTH) // P_TILE
P_N_TILES = SEG1 // P_TILE
COL_GM_U, COL_GM_V, COL_ATT_Q = 0, 1, 2
COL_CV_A, COL_CV_B, COL_ML_Q, COL_ML_K, COL_ML_V, COL_ML_O = 3, 4, 5, 6, 7, 8
P_K_COL = 9 * BRANCH_WIDTH


def _cparams(*sem):
    return pltpu.CompilerParams(dimension_semantics=sem, vmem_limit_bytes=V7X_VMEM_LIMIT)


def _sigmoid(x):
    return 1.0 / (1.0 + jnp.exp(-x))


def _gelu_tanh(x):
    return 0.5 * x * (1.0 + jnp.tanh(0.7978845608028654 * (x + 0.044715 * (x * x * x))))


def _ln_rows(z, g, b):
    mu = jnp.mean(z, axis=-1, keepdims=True)
    zc = z - mu
    var = jnp.mean(zc * zc, axis=-1, keepdims=True)
    return zc * lax.rsqrt(var + EPS) * g + b


def _dot_nt(a, b_t):
    return lax.dot_general(a, b_t, (((1,), (1,)), ((), ())), preferred_element_type=F32)


def _pick(ref, is_ctx):
    return jnp.where(is_ctx, ref[1:2, :], ref[0:1, :])


def _mod_kernel(c_ref, w_ref, b_ref, o_ref):
    c = c_ref[...]
    a = (c * _sigmoid(c)).astype(BF16)
    o_ref[...] = jnp.dot(a, w_ref[...].astype(BF16), preferred_element_type=F32) + b_ref[...]


def modulation(cc, w_mod, b_mod, tn=1024):
    L, D, N = w_mod.shape
    return pl.pallas_call(
        _mod_kernel,
        grid=(L, N // tn),
        in_specs=[pl.BlockSpec((16, D), lambda l, j: (0, 0)),
                  pl.BlockSpec((None, D, tn), lambda l, j: (l, 0, j)),
                  pl.BlockSpec((None, 1, tn), lambda l, j: (l, 0, j))],
        out_specs=pl.BlockSpec((None, 16, tn), lambda l, j: (l, 0, j)),
        out_shape=jax.ShapeDtypeStruct((L, 16, N), F32),
        compiler_params=_cparams("parallel", "parallel"),
    )(cc, w_mod, b_mod.reshape(L, 1, N))


def _modulate_kernel(x_ref, sh_ref, sc_ref, o_ref):
    is_ctx = pl.program_id(0) >= LAT_ROW_BLKS
    o_ref[...] = (x_ref[...] * (1.0 + _pick(sc_ref, is_ctx)) + _pick(sh_ref, is_ctx)).astype(o_ref.dtype)


def modulate(x_all, mod_l):
    D = D_MODEL
    return pl.pallas_call(
        _modulate_kernel,
        grid=(N_ROW_BLKS,),
        in_specs=[pl.BlockSpec((ROW_BLK, D), lambda i: (i, 0)),
                  pl.BlockSpec((16, D), lambda i: (0, 0)),
                  pl.BlockSpec((16, D), lambda i: (0, 1))],
        out_specs=pl.BlockSpec((ROW_BLK, D), lambda i: (i, 0)),
        out_shape=jax.ShapeDtypeStruct((N_TOK, D), BF16),
        compiler_params=_cparams("parallel"),
    )(x_all, mod_l, mod_l)


def _proj_kernel(a_ref, w_ref, o_ref, wb_ref):
    @pl.when(pl.program_id(1) == 0)
    def _():
        wb_ref[...] = w_ref[...].astype(BF16)

    o_ref[...] = _dot_nt(a_ref[...], wb_ref[...]).astype(o_ref.dtype)


def _p_out_tile(j):
    return jnp.where(j < P_KV_TILE_IN, j, jnp.where(j == P_KV_TILE_IN, P_N_TILES - 1, j - 1))


def branch_projection(h_all, w_in_t, layer, tm=N_TOK // 8):
    M, K = h_all.shape
    return pl.pallas_call(
        _proj_kernel,
        grid=(P_N_TILES, M // tm),
        in_specs=[pl.BlockSpec((tm, K), lambda j, i: (i, 0)),
                  pl.BlockSpec((None, P_TILE, K), lambda j, i: (layer, j, 0))],
        out_specs=pl.BlockSpec((tm, P_TILE), lambda j, i: (i, _p_out_tile(j))),
        out_shape=jax.ShapeDtypeStruct((M, SEG1), BF16),
        scratch_shapes=[pltpu.VMEM((P_TILE, K), BF16)],
        compiler_params=_cparams("parallel", "arbitrary"),
        name="branch_projection",
    )(h_all, w_in_t)


def _gates_kernel(w_ref, h_ref, bias_ref, o_ref, *, n_chunks):
    z = _dot_nt(w_ref[...].astype(BF16), h_ref[...])
    z = z + bias_ref[:, 0:1]
    lf = jnp.minimum(z, 0.0) - jnp.log1p(jnp.exp(-jnp.abs(z)))
    row = lax.broadcasted_iota(jnp.int32, (ML_CHUNK, ML_CHUNK), 0)
    col = lax.broadcasted_iota(jnp.int32, (ML_CHUNK, ML_CHUNK), 1)
    upper = (row <= col).astype(F32)
    lower = (row >= col).astype(F32)
    H = ML_HEADS
    for c in range(n_chunks):
        sl = slice(c * ML_CHUNK, (c + 1) * ML_CHUNK)
        b_f = jnp.dot(lf[H:2 * H, sl], upper, preferred_element_type=F32, precision=lax.Precision.HIGHEST)
        b_b = jnp.dot(lf[3 * H:4 * H, sl], lower, preferred_element_type=F32, precision=lax.Precision.HIGHEST)
        o_ref[:, sl] = jnp.concatenate([b_f, z[0:H, sl] - b_f, b_b, z[2 * H:3 * H, sl] - b_b], axis=0)


def mlstm_gate_rows(h_all, w_in_t, layer, bias, tt=GATE_TOK_TILE):
    M, K = h_all.shape
    kern = functools.partial(_gates_kernel, n_chunks=tt // ML_CHUNK)
    return pl.pallas_call(
        kern,
        grid=(M // tt,),
        in_specs=[pl.BlockSpec((None, GATE_COLS, K), lambda i: (layer, SEG1 // GATE_COLS, 0)),
                  pl.BlockSpec((tt, K), lambda i: (i, 0)),
                  pl.BlockSpec((GATE_COLS, LANES), lambda i: (0, 0))],
        out_specs=pl.BlockSpec((GATE_COLS, tt), lambda i: (0, i)),
        out_shape=jax.ShapeDtypeStruct((GATE_COLS, M), F32),
        compiler_params=_cparams("parallel"),
    )(w_in_t, h_all, bias)


def _rope_head(x, cos, sin, first_half):
    partner = jnp.where(first_half, pltpu.roll(x, 96, 1), pltpu.roll(x, 32, 1))
    return x * cos + partner * sin


def _qk_kernel(q_ref, kv_ref, cos_ref, sin_ref, qg_ref, kg_ref, qo_ref, ko_ref):
    cos = cos_ref[...]
    sin = sin_ref[...]
    lane = lax.broadcasted_iota(jnp.int32, (ROW_BLK, HEAD_DIM), 1)
    first_half = jnp.bitwise_and(lane, 63) < 32

    def prep(x, gain, scale):
        x = x.astype(F32)
        y = x * lax.rsqrt(jnp.mean(x * x, axis=-1, keepdims=True) + EPS) * gain
        return (_rope_head(y, cos, sin, first_half) * scale).astype(BF16)

    for h in range(ATT_HEADS):
        sl = slice(h * HEAD_DIM, (h + 1) * HEAD_DIM)
        qo_ref[:, sl] = prep(q_ref[:, sl], qg_ref[...], HEAD_DIM ** -0.5)
    for h in range(ATT_KV_HEADS):
        sl = slice(h * HEAD_DIM, (h + 1) * HEAD_DIM)
        ko_ref[:, sl] = prep(kv_ref[:, sl], kg_ref[...], 1.0)


def qk_prepare(P, cos_t, sin_t, q_gain, k_gain):
    return pl.pallas_call(
        _qk_kernel,
        grid=(N_ROW_BLKS,),
        in_specs=[pl.BlockSpec((ROW_BLK, BRANCH_WIDTH), lambda i: (i, COL_ATT_Q)),
                  pl.BlockSpec((ROW_BLK, P_TILE), lambda i: (i, P_N_TILES - 1)),
                  pl.BlockSpec((ROW_BLK, HEAD_DIM), lambda i: (i, 0)),
                  pl.BlockSpec((ROW_BLK, HEAD_DIM), lambda i: (i, 0)),
                  pl.BlockSpec((1, HEAD_DIM), lambda i: (0, 0)),
                  pl.BlockSpec((1, HEAD_DIM), lambda i: (0, 0))],
        out_specs=[pl.BlockSpec((ROW_BLK, BRANCH_WIDTH), lambda i: (i, 0)),
                   pl.BlockSpec((ROW_BLK, KV_WIDTH), lambda i: (i, 0))],
        out_shape=[jax.ShapeDtypeStruct((N_TOK, BRANCH_WIDTH), BF16),
                   jax.ShapeDtypeStruct((N_TOK, KV_WIDTH), BF16)],
        compiler_params=_cparams("parallel"),
    )(P, P, cos_t, sin_t, q_gain.reshape(1, HEAD_DIM), k_gain.reshape(1, HEAD_DIM))


def _attn_kernel(q_ref, k1_ref, v1_ref, k2_ref, v2_ref, o_ref, *, tq, ck, n_chunks):
    rows = ATT_REP * tq
    q = jnp.concatenate([q_ref[:, h * HEAD_DIM:(h + 1) * HEAD_DIM] for h in range(ATT_REP)], axis=0)

    def step(k, v, carry):
        m, l, acc = carry
        s = lax.dot_general(q, k, (((1,), (1,)), ((), ())), preferred_element_type=F32)
        m_new = jnp.maximum(m, jnp.max(s, axis=-1, keepdims=True))
        p = jnp.exp(s - m_new)
        a = jnp.exp(m - m_new)
        l = a * l + jnp.sum(p, axis=-1, keepdims=True)
        acc = a * acc + jnp.dot(p.astype(BF16), v, preferred_element_type=F32)
        return m_new, l, acc

    def body(c, carry):
        off = pl.multiple_of(c * ck, ck)
        return step(k1_ref[pl.ds(off, ck), :], v1_ref[pl.ds(off, ck), :], carry)

    carry = (jnp.full((rows, 1), -jnp.inf, F32), jnp.zeros((rows, 1), F32),
             jnp.zeros((rows, HEAD_DIM), F32))
    if n_chunks:
        carry = lax.fori_loop(0, n_chunks, body, carry, unroll=min(8, n_chunks))
    _, l, acc = step(k2_ref[...], v2_ref[...], carry)
    o = acc / l
    for h in range(ATT_REP):
        o_ref[:, h * HEAD_DIM:(h + 1) * HEAD_DIM] = o[h * tq:(h + 1) * tq].astype(o_ref.dtype)


def attention(q, k, P, latent, tq=256, ck=1024):
    gw = ATT_REP * HEAD_DIM
    v_col = (P_K_COL + KV_WIDTH) // HEAD_DIM
    ctx_blk = SEQ // CTX_LEN
    n_rows = SEQ if latent else CTX_LEN
    q_blk0 = 0 if latent else SEQ // tq
    kern = functools.partial(_attn_kernel, tq=tq, ck=ck, n_chunks=SEQ // ck if latent else 0)
    return pl.pallas_call(
        kern,
        grid=(ATT_KV_HEADS, n_rows // tq),
        in_specs=[pl.BlockSpec((tq, gw), lambda g, i: (q_blk0 + i, g)),
                  pl.BlockSpec((SEQ, HEAD_DIM), lambda g, i: (0, g)),
                  pl.BlockSpec((SEQ, HEAD_DIM), lambda g, i: (0, v_col + g)),
                  pl.BlockSpec((CTX_LEN, HEAD_DIM), lambda g, i: (ctx_blk, g)),
                  pl.BlockSpec((CTX_LEN, HEAD_DIM), lambda g, i: (ctx_blk, v_col + g))],
        out_specs=pl.BlockSpec((tq, gw), lambda g, i: (i, g)),
        out_shape=jax.ShapeDtypeStruct((n_rows, ATT_HEADS * HEAD_DIM), BF16),
        compiler_params=_cparams("parallel", "parallel"),
        name="gqa_attention",
    )(q, k, P, k, P)


def _gmlp_kernel(u_ref, v_ref, g_ref, b_ref, ws_ref, bs_ref, o_ref):
    u = _gelu_tanh(u_ref[...].astype(F32))
    v = _ln_rows(_gelu_tanh(v_ref[...].astype(F32)), g_ref[...], b_ref[...]).astype(BF16)
    for c in range(ROW_BLK // GM_CHUNK):
        rs = slice(c * GM_CHUNK, (c + 1) * GM_CHUNK)
        for g in range(GM_GROUPS):
            cs = slice(g * HEAD_DIM, (g + 1) * HEAD_DIM)
            s = jnp.dot(ws_ref[g], v[rs, cs], preferred_element_type=F32) + bs_ref[g]
            o_ref[rs, cs] = (u[rs, cs] * s).astype(o_ref.dtype)


def chunk_gmlp(P, ln_g, ln_b, w_s, b_s, n_rows):
    bs_rep = jnp.broadcast_to(b_s[:, :, None], (GM_GROUPS, GM_CHUNK, HEAD_DIM))
    BW = BRANCH_WIDTH
    return pl.pallas_call(
        _gmlp_kernel,
        grid=(n_rows // ROW_BLK,),
        in_specs=[pl.BlockSpec((ROW_BLK, BW), lambda i: (i, COL_GM_U)),
                  pl.BlockSpec((ROW_BLK, BW), lambda i: (i, COL_GM_V)),
                  pl.BlockSpec((1, BW), lambda i: (0, 0)),
                  pl.BlockSpec((1, BW), lambda i: (0, 0)),
                  pl.BlockSpec((GM_GROUPS, GM_CHUNK, GM_CHUNK), lambda i: (0, 0, 0)),
                  pl.BlockSpec((GM_GROUPS, GM_CHUNK, HEAD_DIM), lambda i: (0, 0, 0))],
        out_specs=pl.BlockSpec((ROW_BLK, BW), lambda i: (i, 0)),
        out_shape=jax.ShapeDtypeStruct((n_rows, BW), BF16),
        compiler_params=_cparams("parallel"),
    )(P, P, ln_g.reshape(1, BW), ln_b.reshape(1, BW), w_s.astype(BF16), bs_rep)


CONV_HALO = 16
CONV_SUB = 32


def _conv_kernel(a_ref, b_ref, ap_ref, bp_ref, an_ref, bn_ref, w_ref, cb_ref, g_ref, beta_ref, o_ref, gbuf):
    i = pl.program_id(0)

    def glu(a, b):
        return a.astype(F32) * _sigmoid(b.astype(F32))

    prev_ok = jnp.logical_and(i != 0, i != LAT_ROW_BLKS)
    next_ok = jnp.logical_and(i != LAT_ROW_BLKS - 1, i != N_ROW_BLKS - 1)
    gbuf[0:CONV_HALO, :] = jnp.where(prev_ok, glu(ap_ref[...], bp_ref[...]), 0.0)
    gbuf[CONV_HALO:CONV_HALO + ROW_BLK, :] = glu(a_ref[...], b_ref[...])
    gbuf[CONV_HALO + ROW_BLK:, :] = jnp.where(next_ok, glu(an_ref[...], bn_ref[...]), 0.0)
    first = CONV_HALO - CONV_K // 2
    for r in range(ROW_BLK // CONV_SUB):
        base = r * CONV_SUB
        acc = jnp.zeros((CONV_SUB, BRANCH_WIDTH), F32)
        for k in range(CONV_K):
            acc = acc + gbuf[base + first + k:base + first + k + CONV_SUB, :] * w_ref[k:k + 1, :]
        y = _ln_rows(acc + cb_ref[...], g_ref[...], beta_ref[...])
        o_ref[base:base + CONV_SUB, :] = (y * _sigmoid(y)).astype(o_ref.dtype)


def conformer_conv(P, conv_w, conv_b, ln_g, ln_b, n_rows):
    BW = BRANCH_WIDTH
    hpb = ROW_BLK // CONV_HALO
    last_halo = N_TOK // CONV_HALO - 1

    def prev_map(col):
        return lambda i: (jnp.maximum(i * hpb - 1, 0), col)

    def next_map(col):
        return lambda i: (jnp.minimum((i + 1) * hpb, last_halo), col)

    vec = pl.BlockSpec((1, BW), lambda i: (0, 0))
    return pl.pallas_call(
        _conv_kernel,
        grid=(n_rows // ROW_BLK,),
        in_specs=[pl.BlockSpec((ROW_BLK, BW), lambda i: (i, COL_CV_A)),
                  pl.BlockSpec((ROW_BLK, BW), lambda i: (i, COL_CV_B)),
                  pl.BlockSpec((CONV_HALO, BW), prev_map(COL_CV_A)),
                  pl.BlockSpec((CONV_HALO, BW), prev_map(COL_CV_B)),
                  pl.BlockSpec((CONV_HALO, BW), next_map(COL_CV_A)),
                  pl.BlockSpec((CONV_HALO, BW), next_map(COL_CV_B)),
                  pl.BlockSpec((CONV_K + 1, BW), lambda i: (0, 0)),
                  vec, vec, vec],
        out_specs=pl.BlockSpec((ROW_BLK, BW), lambda i: (i, 0)),
        out_shape=jax.ShapeDtypeStruct((n_rows, BW), BF16),
        scratch_shapes=[pltpu.VMEM((ROW_BLK + 2 * CONV_HALO, BW), F32)],
        compiler_params=_cparams("parallel"),
    )(P, P, P, P, P, P, jnp.pad(conv_w, ((0, 1), (0, 0))), conv_b.reshape(1, BW),
      ln_g.reshape(1, BW), ln_b.reshape(1, BW))


MW_TILE = 512
MW_SHIFT = MERGE_START % MW_TILE


def _merge_weights_kernel(a_ref, b_ref, o_ref):
    o_ref[0:MW_TILE - MW_SHIFT, :] = a_ref[MW_SHIFT:, :].astype(o_ref.dtype)
    o_ref[MW_TILE - MW_SHIFT:, :] = b_ref[...].astype(o_ref.dtype)


def merge_weights(w_in_t, layer):
    D = D_MODEL
    a0 = (MERGE_START - MW_SHIFT) // MW_TILE
    b0 = (MERGE_START - MW_SHIFT) // MW_SHIFT
    return pl.pallas_call(
        _merge_weights_kernel,
        grid=(N_BRANCH * D // MW_TILE,),
        in_specs=[pl.BlockSpec((None, MW_TILE, D), lambda j: (layer, a0 + j, 0)),
                  pl.BlockSpec((None, MW_SHIFT, D), lambda j: (layer, b0 + (j + 1) * (MW_TILE // MW_SHIFT), 0))],
        out_specs=pl.BlockSpec((MW_TILE, D), lambda j: (j, 0)),
        out_shape=jax.ShapeDtypeStruct((N_BRANCH * D, D), BF16),
        compiler_params=_cparams("parallel"),
        name="merge_weights",
    )(w_in_t, w_in_t)


def _merge_kernel(h_ref, b0, b1, b2, b3, m0, m1, m2, m3, w0, w1, w2, w3, o_ref):
    h = h_ref[...]
    acc = None
    for br, wm, wb in ((b0, m0, w0), (b1, m1, w1), (b2, m2, w2), (b3, m3, w3)):
        gate = _sigmoid(_dot_nt(h, wm[...]))
        t = gate * jnp.dot(br[...], wb[...], preferred_element_type=F32)
        acc = t if acc is None else acc + t
    o_ref[...] = acc.astype(o_ref.dtype)


def merge_branches(h_all, branches, w_merge, w_branch, n_rows, tm, tn=256):
    D = D_MODEL
    BW = BRANCH_WIDTH
    nj = D // tn
    one = pl.Buffered(1)
    in_specs = [pl.BlockSpec((tm, D), lambda i, j: (i, 0), pipeline_mode=one)]
    in_specs += [pl.BlockSpec((tm, BW), lambda i, j: (i, 0), pipeline_mode=one) for _ in range(N_BRANCH)]
    in_specs += [pl.BlockSpec((tn, D), functools.partial(lambda i, j, b: (b * nj + j, 0), b=b))
                 for b in range(N_BRANCH)]
    in_specs += [pl.BlockSpec((None, BW, tn), functools.partial(lambda i, j, b: (b, 0, j), b=b))
                 for b in range(N_BRANCH)]
    return pl.pallas_call(
        _merge_kernel,
        grid=(n_rows // tm, nj),
        in_specs=in_specs,
        out_specs=pl.BlockSpec((tm, tn), lambda i, j: (i, j)),
        out_shape=jax.ShapeDtypeStruct((n_rows, D), BF16),
        compiler_params=_cparams("parallel", "arbitrary"),
        name="gated_merge",
    )(h_all, *branches, *([w_merge] * N_BRANCH), *([w_branch] * N_BRANCH))


def _out_ln_kernel(y_ref, w_ref, x_ref, g1_ref, lg_ref, lb_ref, sh_ref, sc_ref, x1_ref, hm_ref, *, nj, tm, tn):
    j = pl.program_id(1)
    cols = pl.ds(pl.multiple_of(j * tn, tn), tn)
    row_is_ctx = pl.program_id(0) * tm + lax.broadcasted_iota(jnp.int32, (tm, 1), 0) >= SEQ
    g1 = jnp.where(row_is_ctx, g1_ref[1:2, cols], g1_ref[0:1, cols])
    part = jnp.dot(y_ref[...], w_ref[...], preferred_element_type=F32)
    x1_ref[:, cols] = ALPHA * x_ref[...] + g1 * part

    @pl.when(j == nj - 1)
    def _():
        sub = next(s for s in (64, 48, 32, 16) if tm % s == 0)

        def chunk(r, carry):
            r0 = pl.multiple_of(r * sub, 16)
            rows = pl.ds(r0, sub)
            is_ctx = pl.program_id(0) * tm + r0 + lax.broadcasted_iota(jnp.int32, (sub, 1), 0) >= SEQ
            x1 = _ln_rows(x1_ref[rows, :], lg_ref[...], lb_ref[...])
            x1_ref[rows, :] = x1
            hm_ref[rows, :] = (x1 * (1.0 + _pick(sc_ref, is_ctx)) + _pick(sh_ref, is_ctx)).astype(hm_ref.dtype)
            return carry

        lax.fori_loop(0, tm // sub, chunk, 0)


def out_proj_ln(y, w_out, x_all, mod_l, ln_g, ln_b, n_rows, tm, tn=512):
    D = D_MODEL
    nj = D // tn
    kern = functools.partial(_out_ln_kernel, nj=nj, tm=tm, tn=tn)
    vec = pl.BlockSpec((1, D), lambda i, j: (0, 0))

    def modspec(col):
        return pl.BlockSpec((16, D), lambda i, j: (0, col))

    return pl.pallas_call(
        kern,
        grid=(n_rows // tm, nj),
        in_specs=[pl.BlockSpec((tm, D), lambda i, j: (i, 0)),
                  pl.BlockSpec((D, tn), lambda i, j: (0, j)),
                  pl.BlockSpec((tm, tn), lambda i, j: (i, j)),
                  modspec(2), vec, vec, modspec(3), modspec(4)],
        out_specs=[pl.BlockSpec((tm, D), lambda i, j: (i, 0)),
                   pl.BlockSpec((tm, D), lambda i, j: (i, 0))],
        out_shape=[jax.ShapeDtypeStruct((n_rows, D), F32),
                   jax.ShapeDtypeStruct((n_rows, D), BF16)],
        compiler_params=_cparams("parallel", "arbitrary"),
        name="out_proj_ln1",
    )(y, w_out, x_all, mod_l, ln_g.reshape(1, D), ln_b.reshape(1, D), mod_l, mod_l)


def _ln2_kernel(x_ref, y_ref, g2_ref, lg_ref, lb_ref, sh_ref, sc_ref, x2_ref, *h_ref, blk_ctx_from):
    is_ctx = pl.program_id(0) >= blk_ctx_from
    z = ALPHA * x_ref[...] + _pick(g2_ref, is_ctx) * y_ref[...]
    x2 = _ln_rows(z, lg_ref[...], lb_ref[...])
    x2_ref[...] = x2
    if h_ref:
        h_ref[0][...] = (x2 * (1.0 + _pick(sc_ref, is_ctx)) + _pick(sh_ref, is_ctx)).astype(BF16)


def ffn_residual_ln(x1, y_moe, mod_l, ln_g, ln_b, mod_next, n_rows):
    D = D_MODEL
    want_h = mod_next is not None
    nxt = mod_next if want_h else mod_l
    kern = functools.partial(_ln2_kernel, blk_ctx_from=LAT_ROW_BLKS)
    row = pl.BlockSpec((ROW_BLK, D), lambda i: (i, 0))
    vec = pl.BlockSpec((1, D), lambda i: (0, 0))
    out_specs = [row, row] if want_h else [row]
    out_shape = [jax.ShapeDtypeStruct((n_rows, D), F32)]
    if want_h:
        out_shape.append(jax.ShapeDtypeStruct((n_rows, D), BF16))
    return pl.pallas_call(
        kern,
        grid=(n_rows // ROW_BLK,),
        in_specs=[row, row, pl.BlockSpec((16, D), lambda i: (0, 5)), vec, vec,
                  pl.BlockSpec((16, D), lambda i: (0, 0)), pl.BlockSpec((16, D), lambda i: (0, 1))],
        out_specs=out_specs,
        out_shape=out_shape,
        compiler_params=_cparams("parallel"),
    )(x1, y_moe, mod_l, ln_g.reshape(1, D), ln_b.reshape(1, D), nxt, nxt)


def _tok_tile(n):
    return next(t for t in (1408, 1280, 1024, 768, 512, 384, 256, 128) if n % t == 0)


def _router_kernel(w_ref, h_ref, o_ref):
    z = _dot_nt(w_ref[...], h_ref[...])
    ez = jnp.exp(z - jnp.max(z, axis=0, keepdims=True))
    o_ref[...] = ez / jnp.sum(ez, axis=0, keepdims=True)


def router_affinities(hm, w_router_t, n_rows):
    D = D_MODEL
    tt = _tok_tile(n_rows)
    return pl.pallas_call(
        _router_kernel,
        grid=(n_rows // tt,),
        in_specs=[pl.BlockSpec((N_EXPERTS, D), lambda i: (0, 0)),
                  pl.BlockSpec((tt, D), lambda i: (i, 0))],
        out_specs=pl.BlockSpec((N_EXPERTS, tt), lambda i: (0, i)),
        out_shape=jax.ShapeDtypeStruct((N_EXPERTS, n_rows), F32),
        compiler_params=_cparams("parallel"),
        name="router_affinities",
    )(w_router_t, hm)


def _select_kernel(aff_ref, idx_ref, cum_ref, *, T, cap):
    E = N_EXPERTS
    nc = T // LANES
    bits = pltpu.bitcast(aff_ref[...], jnp.int32)

    def bisect(_, lohi):
        lo, hi = lohi
        mid = lo + ((hi - lo + 1) >> 1)
        cnt = jnp.sum((bits >= mid).astype(F32), axis=1, keepdims=True)
        ok = cnt >= cap
        return jnp.where(ok, mid, lo), jnp.where(ok, hi, mid - 1)

    thr, _ = lax.fori_loop(0, 31, bisect, (jnp.zeros((E, 1), jnp.int32), jnp.full((E, 1), 0x7F800000, jnp.int32)))
    gt = (bits > thr).astype(F32)
    eq = (bits == thr).astype(F32)
    need = cap - jnp.sum(gt, axis=1, keepdims=True)
    row = lax.broadcasted_iota(jnp.int32, (LANES, LANES), 0)
    col = lax.broadcasted_iota(jnp.int32, (LANES, LANES), 1)
    before = (row < col).astype(BF16)
    upto = (row <= col).astype(BF16)
    seen_eq = jnp.zeros((E, 1), F32)
    seen_sel = jnp.zeros((E, 1), F32)
    for c in range(nc):
        sl = slice(c * LANES, (c + 1) * LANES)
        eq_c = eq[:, sl]
        rank_eq = jnp.dot(eq_c.astype(BF16), before, preferred_element_type=F32) + seen_eq
        seen_eq = seen_eq + jnp.sum(eq_c, axis=1, keepdims=True)
        sel_c = jnp.maximum(gt[:, sl], jnp.where(rank_eq < need, eq_c, 0.0))
        cum_ref[:, sl] = jnp.dot(sel_c.astype(BF16), upto, preferred_element_type=F32) + seen_sel
        seen_sel = seen_sel + jnp.sum(sel_c, axis=1, keepdims=True)

    sb = min(cap, 256)
    lane = lax.broadcasted_iota(jnp.int32, (sb, LANES), 1)
    for b in range(cap // sb):
        slot = (b * sb + lax.broadcasted_iota(jnp.int32, (sb, LANES), 0)).astype(F32)
        res = jnp.zeros((sb, LANES), F32)
        for e in range(E):
            def count(c, acc):
                cum = cum_ref[e:e + 1, pl.ds(pl.multiple_of(c * LANES, LANES), LANES)]
                return acc + jnp.where(cum <= slot, 1.0, 0.0)

            acc = lax.fori_loop(0, nc, count, jnp.zeros((sb, LANES), F32))
            res = jnp.where(lane == e, jnp.sum(acc, axis=1, keepdims=True), res)
        idx_ref[b * sb:(b + 1) * sb, :] = res.astype(jnp.int32)


def expert_choice_indices(aff, first_tok, T, cap):
    kern = functools.partial(_select_kernel, T=T, cap=cap)
    out = pl.pallas_call(
        kern,
        grid=(1,),
        in_specs=[pl.BlockSpec((N_EXPERTS, T), lambda i: (0, first_tok // T))],
        out_specs=pl.BlockSpec((cap, LANES), lambda i: (0, 0)),
        out_shape=jax.ShapeDtypeStruct((cap, LANES), jnp.int32),
        scratch_shapes=[pltpu.VMEM((N_EXPERTS, T), F32)],
        compiler_params=_cparams("arbitrary"),
        name="expert_choice_select",
    )(aff)
    return out[:, :N_EXPERTS].T


def _gather_swiglu_kernel(idx_ref, nxt_ref, x_hbm, sh_ref, sc_ref, wr_ref, wg_ref, wu_ref, o_ref, gain_ref,
                          xbuf, xe, sem, *, M, cap_l):
    e = pl.program_id(0)
    j = pl.program_id(1)
    n_e = pl.num_programs(0)

    def row_copy(ref, r):
        return pltpu.make_async_copy(x_hbm.at[pl.ds(ref[0, r], 1), :], xbuf.at[pl.ds(r, 1), :], sem.at[0])

    def each_row(fn):
        def body(r, carry):
            fn(r)
            return carry
        lax.fori_loop(0, M, body, 0, unroll=8)

    @pl.when(j == 0)
    def _():
        @pl.when(e == 0)
        def _():
            each_row(lambda r: row_copy(idx_ref, r).start())

        each_row(lambda r: row_copy(idx_ref, r).wait())
        xe[0:cap_l, :] = (xbuf[0:cap_l, :] * (1.0 + sc_ref[0:1, :]) + sh_ref[0:1, :]).astype(BF16)
        if M > cap_l:
            xe[cap_l:, :] = (xbuf[cap_l:, :] * (1.0 + sc_ref[1:2, :]) + sh_ref[1:2, :]).astype(BF16)

        @pl.when(e + 1 < n_e)
        def _():
            each_row(lambda r: row_copy(nxt_ref, r).start())

        z = jnp.dot(xe[...], wr_ref[...], preferred_element_type=F32)
        lane = lax.broadcasted_iota(jnp.int32, z.shape, 1)
        z = jnp.where(lane < N_EXPERTS, z, -jnp.inf)
        ez = jnp.exp(z - jnp.max(z, axis=1, keepdims=True))
        gain_ref[...] = (jnp.sum(jnp.where(lane == e, ez, 0.0), axis=1, keepdims=True)
                         / jnp.sum(ez, axis=1, keepdims=True))

    x = xe[...]
    g = jnp.dot(x, wg_ref[...].astype(BF16), preferred_element_type=F32)
    u = jnp.dot(x, wu_ref[...].astype(BF16), preferred_element_type=F32)
    o_ref[...] = (g * _sigmoid(g) * u).astype(o_ref.dtype)


def expert_gather_swiglu(idx, x1, mod_l, w_router_pad, w_gate, w_up, layer, cap_l, tn=256):
    E, M = idx.shape
    D = D_MODEL
    N = w_gate.shape[-1]
    idx3 = idx.reshape(E, 1, M)
    wspec = pl.BlockSpec((None, None, D, tn), lambda e, j: (layer, e, 0, j))
    kern = functools.partial(_gather_swiglu_kernel, M=M, cap_l=cap_l)
    return pl.pallas_call(
        kern,
        grid=(E, N // tn),
        in_specs=[pl.BlockSpec((None, 1, M), lambda e, j: (e, 0, 0), memory_space=pltpu.SMEM),
                  pl.BlockSpec((None, 1, M), lambda e, j: (jnp.minimum(e + 1, E - 1), 0, 0), memory_space=pltpu.SMEM),
                  pl.BlockSpec(memory_space=pl.ANY),
                  pl.BlockSpec((16, D), lambda e, j: (0, 3)),
                  pl.BlockSpec((16, D), lambda e, j: (0, 4)),
                  pl.BlockSpec((D, LANES), lambda e, j: (0, 0)),
                  wspec, wspec],
        out_specs=[pl.BlockSpec((None, M, tn), lambda e, j: (e, 0, j)),
                   pl.BlockSpec((None, M, 1), lambda e, j: (e, 0, 0))],
        out_shape=[jax.ShapeDtypeStruct((E, M, N), BF16),
                   jax.ShapeDtypeStruct((E, M, 1), F32)],
        scratch_shapes=[pltpu.VMEM((M, D), F32), pltpu.VMEM((M, D), BF16), pltpu.SemaphoreType.DMA((1,))],
        compiler_params=_cparams("arbitrary", "arbitrary"),
        name="expert_gather_swiglu",
    )(idx3, idx3, x1, mod_l, mod_l, w_router_pad, w_gate, w_up)


def _down_kernel(h_ref, w_ref, g_ref, o_ref):
    y = jnp.dot(h_ref[...], w_ref[...].astype(BF16), preferred_element_type=F32)
    o_ref[...] = y * g_ref[...]


def expert_down(hid, w_down, gains, layer, tn=1024):
    E, M, K = hid.shape
    N = w_down.shape[-1]
    return pl.pallas_call(
        _down_kernel,
        grid=(E, N // tn),
        in_specs=[pl.BlockSpec((None, M, K), lambda e, j: (e, 0, 0)),
                  pl.BlockSpec((None, None, K, tn), lambda e, j: (layer, e, 0, j)),
                  pl.BlockSpec((None, M, 1), lambda e, j: (e, 0, 0))],
        out_specs=pl.BlockSpec((None, M, tn), lambda e, j: (e, 0, j)),
        out_shape=jax.ShapeDtypeStruct((E, M, N), F32),
        compiler_params=_cparams("parallel", "parallel"),
    )(hid, w_down, gains)


def _scatter_kernel(idx_ref, ye_ref, acc_in, acc_ref, buf, gsem, ssem, *, R, n_chunks):
    del acc_in
    c = pl.program_id(1)

    def gather(chunk, slot, r):
        t = idx_ref[0, chunk * R + r]
        return pltpu.make_async_copy(acc_ref.at[pl.ds(t, 1), :], buf.at[slot, pl.ds(r, 1), :], gsem.at[slot])

    def scatter(chunk, slot, r):
        t = idx_ref[0, chunk * R + r]
        return pltpu.make_async_copy(buf.at[slot, pl.ds(r, 1), :], acc_ref.at[pl.ds(t, 1), :], ssem.at[slot])

    def each_row(fn):
        def body(r, carry):
            fn(r)
            return carry
        lax.fori_loop(0, R, body, 0, unroll=8)

    slot = c % 2

    @pl.when(c == 0)
    def _():
        each_row(lambda r: gather(0, 0, r).start())

    each_row(lambda r: gather(c, slot, r).wait())

    @pl.when(c >= 1)
    def _():
        each_row(lambda r: scatter(c - 1, 1 - slot, r).wait())

    @pl.when(c + 1 < n_chunks)
    def _():
        each_row(lambda r: gather(c + 1, 1 - slot, r).start())

    buf[slot] = buf[slot] + ye_ref[...]
    each_row(lambda r: scatter(c, slot, r).start())

    @pl.when(c == n_chunks - 1)
    def _():
        each_row(lambda r: scatter(c, slot, r).wait())


def scatter_add_rows(idx, ye, n_rows):
    E, M, D = ye.shape
    R = next(r for r in (264, 256, 160, 128, 64, 32) if M % r == 0)
    n_chunks = M // R
    kern = functools.partial(_scatter_kernel, R=R, n_chunks=n_chunks)
    return pl.pallas_call(
        kern,
        grid=(E, n_chunks),
        in_specs=[pl.BlockSpec((None, 1, M), lambda e, c: (e, 0, 0), memory_space=pltpu.SMEM),
                  pl.BlockSpec((None, R, D), lambda e, c: (e, c, 0)),
                  pl.BlockSpec(memory_space=pl.ANY)],
        out_specs=pl.BlockSpec(memory_space=pl.ANY),
        out_shape=jax.ShapeDtypeStruct((n_rows, D), F32),
        scratch_shapes=[pltpu.VMEM((2, R, D), F32), pltpu.SemaphoreType.DMA((2,)), pltpu.SemaphoreType.DMA((2,))],
        input_output_aliases={2: 0},
        compiler_params=_cparams("arbitrary", "arbitrary"),
        name="scatter_add_rows",
    )(idx.reshape(E, 1, M), ye, jnp.zeros((n_rows, D), F32))


N_CHUNKS = N_TOK // ML_CHUNK
LAT_CHUNKS = SEQ // ML_CHUNK
CTX_CHUNKS = CTX_LEN // ML_CHUNK


def _mlstm_kernel(q_ref, k_ref, v_ref, og_ref, g_ref, ng_ref, out_ref, acc_ref, c_ref, n_ref, m_ref, *, hb, out_chunks):
    L, d = ML_CHUNK, ML_HEAD_DIM
    row = lax.broadcasted_iota(jnp.int32, (L, L), 0)
    col = lax.broadcasted_iota(jnp.int32, (L, L), 1)
    eye = row == col
    masks = (col <= row, col >= row)
    lasts = (L - 1, 0)
    k_scale = ML_HEAD_DIM ** -0.5

    acc_ref[...] = jnp.zeros(acc_ref.shape, F32)
    c_ref[...] = jnp.zeros(c_ref.shape, F32)
    n_ref[...] = jnp.zeros(n_ref.shape, F32)
    m_ref[...] = jnp.zeros(m_ref.shape, F32)

    def col_of(r):
        return jnp.sum(jnp.where(eye, r, 0.0), axis=1, keepdims=True)

    def step(t, carry):
        chunk = (jnp.where(t < CTX_CHUNKS, t + LAT_CHUNKS, t - CTX_CHUNKS), N_CHUNKS - 1 - t)
        for j in range(hb):
            hs = slice(j * d, (j + 1) * d)
            for dr in range(2):
                si = 2 * j + dr
                off = pl.multiple_of(chunk[dr] * L, L)
                rows = pl.ds(off, L)
                q = q_ref[rows, hs]
                kf = k_ref[rows, hs].astype(F32) * k_scale
                v = v_ref[rows, hs]
                b_row = g_ref[j, 2 * dr:2 * dr + 1, rows]
                a_row = g_ref[j, 2 * dr + 1:2 * dr + 2, rows]
                last = lasts[dr]
                m = m_ref[si][0:1, 0:1]
                n = n_ref[si][0:1, :]
                C = c_ref[si]
                b_col = col_of(b_row)
                a_col = col_of(a_row)
                dlog = jnp.where(masks[dr], b_col + a_row, -jnp.inf)
                inter = b_col + m
                mj = jnp.maximum(inter, jnp.max(dlog, axis=1, keepdims=True))
                w_inter = jnp.exp(inter - mj)
                qk = lax.dot_general(q, kf.astype(BF16), (((1,), (1,)), ((), ())), preferred_element_type=F32)
                s = qk * jnp.exp(dlog - mj)
                num = (w_inter * jnp.dot(q, C.astype(BF16), preferred_element_type=F32)
                       + jnp.dot(s.astype(BF16), v, preferred_element_type=F32))
                qn = jnp.sum(q.astype(F32) * n, axis=1, keepdims=True)
                den = w_inter * qn + jnp.sum(s, axis=1, keepdims=True)
                h = num / jnp.maximum(jnp.abs(den), jnp.exp(-mj))
                acc_ref[rows, hs] += h
                m_new = mj[last:last + 1, :]
                b_end = b_row[:, last:last + 1]
                w_c = jnp.exp(b_end + m - m_new)
                kw = kf * jnp.exp(b_end + a_col - m_new)
                c_ref[si] = w_c * C + lax.dot_general(kw.astype(BF16), v, (((0,), (0,)), ((), ())),
                                                      preferred_element_type=F32)
                n_ref[si] = jnp.broadcast_to(w_c * n + jnp.sum(kw, axis=0, keepdims=True), (8, d))
                m_ref[si] = jnp.broadcast_to(m_new, (8, LANES))
        return carry

    lax.fori_loop(0, N_CHUNKS, step, 0)

    def finish(c, carry):
        rows = pl.ds(pl.multiple_of(c * L, L), L)
        for j in range(hb):
            hs = slice(j * d, (j + 1) * d)
            hh = acc_ref[rows, hs]
            hc = hh - jnp.mean(hh, axis=1, keepdims=True)
            hn = hc * lax.rsqrt(jnp.mean(hc * hc, axis=1, keepdims=True) + EPS) * ng_ref[j]
            out_ref[rows, hs] = (_sigmoid(og_ref[rows, hs].astype(F32)) * hn).astype(out_ref.dtype)
        return carry

    lax.fori_loop(0, out_chunks, finish, 0)


def mlstm(P, G, norm_g, n_rows, hb=2):
    w = hb * ML_HEAD_DIM
    per = BRANCH_WIDTH // w

    def colspec(col):
        return pl.BlockSpec((N_TOK, w), lambda i: (0, col * per + i))

    kern = functools.partial(_mlstm_kernel, hb=hb, out_chunks=n_rows // ML_CHUNK)
    return pl.pallas_call(
        kern,
        grid=(ML_HEADS // hb,),
        in_specs=[colspec(COL_ML_Q), colspec(COL_ML_K), colspec(COL_ML_V),
                  pl.BlockSpec((n_rows, w), lambda i: (0, COL_ML_O * per + i)),
                  pl.BlockSpec((hb, 4, N_TOK), lambda i: (i, 0, 0)),
                  pl.BlockSpec((hb, 1, ML_HEAD_DIM), lambda i: (i, 0, 0))],
        out_specs=pl.BlockSpec((n_rows, w), lambda i: (0, i)),
        out_shape=jax.ShapeDtypeStruct((n_rows, BRANCH_WIDTH), BF16),
        scratch_shapes=[pltpu.VMEM((N_TOK, w), F32),
                        pltpu.VMEM((2 * hb, ML_HEAD_DIM, ML_HEAD_DIM), F32),
                        pltpu.VMEM((2 * hb, 8, ML_HEAD_DIM), F32),
                        pltpu.VMEM((2 * hb, 8, LANES), F32)],
        compiler_params=_cparams("parallel"),
        name="mlstm_scan",
    )(P, P, P, P, G, norm_g.reshape(ML_HEADS, 1, ML_HEAD_DIM))


def _rope_tables():
    rows = SEQ // GRID_W
    row = jnp.repeat(jnp.arange(rows), GRID_W).astype(F32)
    col = (jnp.arange(SEQ) % GRID_W).astype(F32)
    n_freq = HEAD_DIM // 4
    inv = ROPE_THETA ** (-jnp.arange(n_freq, dtype=F32) / n_freq)
    ang_r = row[:, None] * inv[None, :]
    ang_c = col[:, None] * inv[None, :]
    cos = jnp.concatenate([jnp.cos(ang_r), jnp.cos(ang_r), jnp.cos(ang_c), jnp.cos(ang_c)], axis=1)
    sin = jnp.concatenate([-jnp.sin(ang_r), jnp.sin(ang_r), -jnp.sin(ang_c), jnp.sin(ang_c)], axis=1)
    cos = jnp.concatenate([cos, jnp.ones((CTX_LEN, HEAD_DIM), F32)], axis=0)
    sin = jnp.concatenate([sin, jnp.zeros((CTX_LEN, HEAD_DIM), F32)], axis=0)
    return cos, sin


def kernel(x, c, ctx, c_ctx, w_mod, b_mod, w_in, att_q_gain, att_k_gain, gm_ln_g, gm_ln_b,
           gm_w_s, gm_b_s, conv_w, conv_b, conv_ln_g, conv_ln_b, ml_gate_bias, ml_norm_g,
           w_branch, w_out, ln1_g, ln1_b, w_router, w_gate, w_up, w_down, ln2_g, ln2_b):
    cos_t, sin_t = _rope_tables()
    w_in_t = jnp.swapaxes(w_in, 1, 2)
    cc = jnp.zeros((16, D_MODEL), F32).at[0].set(c[0]).at[1].set(c_ctx)
    mods = modulation(cc, w_mod, b_mod)
    x_all = jnp.concatenate([x[0], ctx[0]], axis=0)
    h_all = modulate(x_all, mods[0])
    for l in range(DEPTH):
        last = l == DEPTH - 1
        n_rows = SEQ if last else N_TOK
        tm = (SEQ if last else N_TOK) // 8
        mod_l = mods[l]

        P = branch_projection(h_all, w_in_t, l)
        gbias = jnp.broadcast_to(ml_gate_bias[l].reshape(GATE_COLS, 1), (GATE_COLS, LANES))
        G = mlstm_gate_rows(h_all, w_in_t, l, gbias)
        q, k = qk_prepare(P, cos_t, sin_t, att_q_gain[l], att_k_gain[l])

        br_a = chunk_gmlp(P, gm_ln_g[l], gm_ln_b[l], gm_w_s[l], gm_b_s[l], n_rows)
        br_b = attention(q, k, P, True)
        if not last:
            br_b = jnp.concatenate([br_b, attention(q, k, P, False)], axis=0)
        br_c = conformer_conv(P, conv_w[l], conv_b[l], conv_ln_g[l], conv_ln_b[l], n_rows)
        G_heads = G.reshape(ML_N_GATES, ML_HEADS, N_TOK).transpose(1, 0, 2)
        br_d = mlstm(P, G_heads, ml_norm_g[l], n_rows)

        w_merge = merge_weights(w_in_t, l)
        y = merge_branches(h_all, [br_a, br_b, br_c, br_d], w_merge, w_branch[l].astype(BF16), n_rows, tm)
        x1, hm = out_proj_ln(y, w_out[l].astype(BF16), x_all, mod_l, ln1_g[l], ln1_b[l], n_rows, tm // 2)

        cap_l = CAPACITY_FACTOR * SEQ // N_EXPERTS
        aff = router_affinities(hm, w_router[l].T.astype(BF16), n_rows)
        idx = expert_choice_indices(aff, 0, SEQ, cap_l)
        if not last:
            idx_c = expert_choice_indices(aff, SEQ, CTX_LEN, CAPACITY_FACTOR * CTX_LEN // N_EXPERTS)
            idx = jnp.concatenate([idx, idx_c + SEQ], axis=1)
        w_router_pad = jnp.pad(w_router[l], ((0, 0), (0, LANES - N_EXPERTS))).astype(BF16)
        hid, gains = expert_gather_swiglu(idx, x1, mod_l, w_router_pad, w_gate, w_up, l, cap_l)
        ye = expert_down(hid, w_down, gains, l)
        y_moe = scatter_add_rows(idx, ye, n_rows)
        res = ffn_residual_ln(x1, y_moe, mod_l, ln2_g[l], ln2_b[l], None if last else mods[l + 1], n_rows)
        if not last:
            x_all, h_all = res
        else:
            x_all = res[0]
    return x_all[None]
```

```python
import functools

import jax
import jax.numpy as jnp
from jax import lax
from jax.experimental import pallas as pl
from jax.experimental.pallas import tpu as pltpu

F32 = jnp.float32
BF16 = jnp.bfloat16

D_MODEL = 4096
SEQ = 8192
DEPTH = 2
GRID_W = 64
CTX_LEN = 256
N_BRANCH = 4
BRANCH_WIDTH = D_MODEL // N_BRANCH
HEAD_DIM = 128
GM_CHUNK = 128
GM_GROUPS = BRANCH_WIDTH // HEAD_DIM
ATT_HEADS = BRANCH_WIDTH // HEAD_DIM
ATT_KV_HEADS = 2
ATT_REP = ATT_HEADS // ATT_KV_HEADS
KV_WIDTH = ATT_KV_HEADS * HEAD_DIM
ROPE_THETA = 10000.0
CONV_K = 31
ML_HEADS = BRANCH_WIDTH // HEAD_DIM
ML_HEAD_DIM = HEAD_DIM
ML_CHUNK = 128
ML_N_GATES = 4
N_EXPERTS = 16
EXPERT_FF = D_MODEL // 4
CAPACITY_FACTOR = 2
ALPHA = (2 * DEPTH) ** 0.25
EPS = 1e-6
N_MOD = 6

N_TOK = SEQ + CTX_LEN
SEG1 = 9 * BRANCH_WIDTH + 2 * KV_WIDTH
GATE_COLS = ML_N_GATES * ML_HEADS
MERGE_START = SEG1 + GATE_COLS

V7X_VMEM_LIMIT = 56 * 1024 * 1024
LANES = 128

GATE_TOK_TILE = next(t for t in (1408, 1280, 1024, 768, 512, 384, 256, 128) if N_TOK % t == 0)
ROW_BLK = 256
N_ROW_BLKS = N_TOK // ROW_BLK
LAT_ROW_BLKS = SEQ // ROW_BLK

P_TILE = 512
P_KV_TILE_IN = (3 * BRANCH_WIDTH) // P_TILE
P_N_TILES = SEG1 // P_TILE
COL_GM_U, COL_GM_V, COL_ATT_Q = 0, 1, 2
COL_CV_A, COL_CV_B, COL_ML_Q, COL_ML_K, COL_ML_V, COL_ML_O = 3, 4, 5, 6, 7, 8
P_K_COL = 9 * BRANCH_WIDTH


def _cparams(*sem):
    return pltpu.CompilerParams(dimension_semantics=sem, vmem_limit_bytes=V7X_VMEM_LIMIT)


def _sigmoid(x):
    return 1.0 / (1.0 + jnp.exp(-x))


def _gelu_tanh(x):
    return 0.5 * x * (1.0 + jnp.tanh(0.7978845608028654 * (x + 0.044715 * (x * x * x))))


def _ln_rows(z, g, b):
    mu = jnp.mean(z, axis=-1, keepdims=True)
    zc = z - mu
    var = jnp.mean(zc * zc, axis=-1, keepdims=True)
    return zc * lax.rsqrt(var + EPS) * g + b


def _dot_nt(a, b_t):
    return lax.dot_general(a, b_t, (((1,), (1,)), ((), ())), preferred_element_type=F32)


def _pick(ref, is_ctx):
    return jnp.where(is_ctx, ref[1:2, :], ref[0:1, :])


def _mod_kernel(c_ref, w_ref, b_ref, o_ref):
    c = c_ref[...]
    a = (c * _sigmoid(c)).astype(BF16)
    o_ref[...] = jnp.dot(a, w_ref[...].astype(BF16), preferred_element_type=F32) + b_ref[...]


def modulation(cc, w_mod, b_mod, tn=1024):
    L, D, N = w_mod.shape
    return pl.pallas_call(
        _mod_kernel,
        grid=(L, N // tn),
        in_specs=[pl.BlockSpec((16, D), lambda l, j: (0, 0)),
                  pl.BlockSpec((None, D, tn), lambda l, j: (l, 0, j)),
                  pl.BlockSpec((None, 1, tn), lambda l, j: (l, 0, j))],
        out_specs=pl.BlockSpec((None, 16, tn), lambda l, j: (l, 0, j)),
        out_shape=jax.ShapeDtypeStruct((L, 16, N), F32),
        compiler_params=_cparams("parallel", "parallel"),
    )(cc, w_mod, b_mod.reshape(L, 1, N))


def _modulate_kernel(x_ref, sh_ref, sc_ref, o_ref):
    is_ctx = pl.program_id(0) >= LAT_ROW_BLKS
    o_ref[...] = (x_ref[...] * (1.0 + _pick(sc_ref, is_ctx)) + _pick(sh_ref, is_ctx)).astype(o_ref.dtype)


def modulate(x_all, mod_l):
    D = D_MODEL
    return pl.pallas_call(
        _modulate_kernel,
        grid=(N_ROW_BLKS,),
        in_specs=[pl.BlockSpec((ROW_BLK, D), lambda i: (i, 0)),
                  pl.BlockSpec((16, D), lambda i: (0, 0)),
                  pl.BlockSpec((16, D), lambda i: (0, 1))],
        out_specs=pl.BlockSpec((ROW_BLK, D), lambda i: (i, 0)),
        out_shape=jax.ShapeDtypeStruct((N_TOK, D), BF16),
        compiler_params=_cparams("parallel"),
    )(x_all, mod_l, mod_l)


def _proj_kernel(a_ref, w_ref, o_ref, wb_ref):
    @pl.when(pl.program_id(1) == 0)
    def _():
        wb_ref[...] = w_ref[...].astype(BF16)

    o_ref[...] = _dot_nt(a_ref[...], wb_ref[...]).astype(o_ref.dtype)


def _p_out_tile(j):
    return jnp.where(j < P_KV_TILE_IN, j, jnp.where(j == P_KV_TILE_IN, P_N_TILES - 1, j - 1))


def branch_projection(h_all, w_in_t, layer, tm=N_TOK // 8):
    M, K = h_all.shape
    return pl.pallas_call(
        _proj_kernel,
        grid=(P_N_TILES, M // tm),
        in_specs=[pl.BlockSpec((tm, K), lambda j, i: (i, 0)),
                  pl.BlockSpec((None, P_TILE, K), lambda j, i: (layer, j, 0))],
        out_specs=pl.BlockSpec((tm, P_TILE), lambda j, i: (i, _p_out_tile(j))),
        out_shape=jax.ShapeDtypeStruct((M, SEG1), BF16),
        scratch_shapes=[pltpu.VMEM((P_TILE, K), BF16)],
        compiler_params=_cparams("parallel", "arbitrary"),
        name="branch_projection",
    )(h_all, w_in_t)


def _gates_kernel(w_ref, h_ref, bias_ref, o_ref, *, n_chunks):
    z = _dot_nt(w_ref[...].astype(BF16), h_ref[...])
    z = z + bias_ref[:, 0:1]
    lf = jnp.minimum(z, 0.0) - jnp.log1p(jnp.exp(-jnp.abs(z)))
    row = lax.broadcasted_iota(jnp.int32, (ML_CHUNK, ML_CHUNK), 0)
    col = lax.broadcasted_iota(jnp.int32, (ML_CHUNK, ML_CHUNK), 1)
    upper = (row <= col).astype(F32)
    lower = (row >= col).astype(F32)
    H = ML_HEADS
    for c in range(n_chunks):
        sl = slice(c * ML_CHUNK, (c + 1) * ML_CHUNK)
        b_f = jnp.dot(lf[H:2 * H, sl], upper, preferred_element_type=F32, precision=lax.Precision.HIGHEST)
        b_b = jnp.dot(lf[3 * H:4 * H, sl], lower, preferred_element_type=F32, precision=lax.Precision.HIGHEST)
        o_ref[:, sl] = jnp.concatenate([b_f, z[0:H, sl] - b_f, b_b, z[2 * H:3 * H, sl] - b_b], axis=0)


def mlstm_gate_rows(h_all, w_in_t, layer, bias, tt=GATE_TOK_TILE):
    M, K = h_all.shape
    kern = functools.partial(_gates_kernel, n_chunks=tt // ML_CHUNK)
    return pl.pallas_call(
        kern,
        grid=(M // tt,),
        in_specs=[pl.BlockSpec((None, GATE_COLS, K), lambda i: (layer, SEG1 // GATE_COLS, 0)),
                  pl.BlockSpec((tt, K), lambda i: (i, 0)),
                  pl.BlockSpec((GATE_COLS, LANES), lambda i: (0, 0))],
        out_specs=pl.BlockSpec((GATE_COLS, tt), lambda i: (0, i)),
        out_shape=jax.ShapeDtypeStruct((GATE_COLS, M), F32),
        compiler_params=_cparams("parallel"),
    )(w_in_t, h_all, bias)


def _rope_head(x, cos, sin, first_half):
    partner = jnp.where(first_half, pltpu.roll(x, 96, 1), pltpu.roll(x, 32, 1))
    return x * cos + partner * sin


def _qk_kernel(q_ref, kv_ref, cos_ref, sin_ref, qg_ref, kg_ref, qo_ref, ko_ref):
    cos = cos_ref[...]
    sin = sin_ref[...]
    lane = lax.broadcasted_iota(jnp.int32, (ROW_BLK, HEAD_DIM), 1)
    first_half = jnp.bitwise_and(lane, 63) < 32

    def prep(x, gain, scale):
        x = x.astype(F32)
        y = x * lax.rsqrt(jnp.mean(x * x, axis=-1, keepdims=True) + EPS) * gain
        return (_rope_head(y, cos, sin, first_half) * scale).astype(BF16)

    for h in range(ATT_HEADS):
        sl = slice(h * HEAD_DIM, (h + 1) * HEAD_DIM)
        qo_ref[:, sl] = prep(q_ref[:, sl], qg_ref[...], HEAD_DIM ** -0.5)
    for h in range(ATT_KV_HEADS):
        sl = slice(h * HEAD_DIM, (h + 1) * HEAD_DIM)
        ko_ref[:, sl] = prep(kv_ref[:, sl], kg_ref[...], 1.0)


def qk_prepare(P, cos_t, sin_t, q_gain, k_gain):
    return pl.pallas_call(
        _qk_kernel,
        grid=(N_ROW_BLKS,),
        in_specs=[pl.BlockSpec((ROW_BLK, BRANCH_WIDTH), lambda i: (i, COL_ATT_Q)),
                  pl.BlockSpec((ROW_BLK, P_TILE), lambda i: (i, P_N_TILES - 1)),
                  pl.BlockSpec((ROW_BLK, HEAD_DIM), lambda i: (i, 0)),
                  pl.BlockSpec((ROW_BLK, HEAD_DIM), lambda i: (i, 0)),
                  pl.BlockSpec((1, HEAD_DIM), lambda i: (0, 0)),
                  pl.BlockSpec((1, HEAD_DIM), lambda i: (0, 0))],
        out_specs=[pl.BlockSpec((ROW_BLK, BRANCH_WIDTH), lambda i: (i, 0)),
                   pl.BlockSpec((ROW_BLK, KV_WIDTH), lambda i: (i, 0))],
        out_shape=[jax.ShapeDtypeStruct((N_TOK, BRANCH_WIDTH), BF16),
                   jax.ShapeDtypeStruct((N_TOK, KV_WIDTH), BF16)],
        compiler_params=_cparams("parallel"),
    )(P, P, cos_t, sin_t, q_gain.reshape(1, HEAD_DIM), k_gain.reshape(1, HEAD_DIM))


def _attn_kernel(q_ref, k1_ref, v1_ref, k2_ref, v2_ref, o_ref, *, tq, ck, n_chunks):
    rows = ATT_REP * tq
    q = jnp.concatenate([q_ref[:, h * HEAD_DIM:(h + 1) * HEAD_DIM] for h in range(ATT_REP)], axis=0)

    def step(k, v, carry):
        m, l, acc = carry
        s = lax.dot_general(q, k, (((1,), (1,)), ((), ())), preferred_element_type=F32)
        m_new = jnp.maximum(m, jnp.max(s, axis=-1, keepdims=True))
        p = jnp.exp(s - m_new)
        a = jnp.exp(m - m_new)
        l = a * l + jnp.sum(p, axis=-1, keepdims=True)
        acc = a * acc + jnp.dot(p.astype(BF16), v, preferred_element_type=F32)
        return m_new, l, acc

    def body(c, carry):
        off = pl.multiple_of(c * ck, ck)
        return step(k1_ref[pl.ds(off, ck), :], v1_ref[pl.ds(off, ck), :], carry)

    carry = (jnp.full((rows, 1), -jnp.inf, F32), jnp.zeros((rows, 1), F32),
             jnp.zeros((rows, HEAD_DIM), F32))
    if n_chunks:
        carry = lax.fori_loop(0, n_chunks, body, carry, unroll=min(8, n_chunks))
    _, l, acc = step(k2_ref[...], v2_ref[...], carry)
    o = acc / l
    for h in range(ATT_REP):
        o_ref[:, h * HEAD_DIM:(h + 1) * HEAD_DIM] = o[h * tq:(h + 1) * tq].astype(o_ref.dtype)


def attention(q, k, P, latent, tq=256, ck=1024):
    gw = ATT_REP * HEAD_DIM
    v_col = (P_K_COL + KV_WIDTH) // HEAD_DIM
    ctx_blk = SEQ // CTX_LEN
    n_rows = SEQ if latent else CTX_LEN
    q_blk0 = 0 if latent else SEQ // tq
    kern = functools.partial(_attn_kernel, tq=tq, ck=ck, n_chunks=SEQ // ck if latent else 0)
    return pl.pallas_call(
        kern,
        grid=(ATT_KV_HEADS, n_rows // tq),
        in_specs=[pl.BlockSpec((tq, gw), lambda g, i: (q_blk0 + i, g)),
                  pl.BlockSpec((SEQ, HEAD_DIM), lambda g, i: (0, g)),
                  pl.BlockSpec((SEQ, HEAD_DIM), lambda g, i: (0, v_col + g)),
                  pl.BlockSpec((CTX_LEN, HEAD_DIM), lambda g, i: (ctx_blk, g)),
                  pl.BlockSpec((CTX_LEN, HEAD_DIM), lambda g, i: (ctx_blk, v_col + g))],
        out_specs=pl.BlockSpec((tq, gw), lambda g, i: (i, g)),
        out_shape=jax.ShapeDtypeStruct((n_rows, ATT_HEADS * HEAD_DIM), BF16),
        compiler_params=_cparams("parallel", "parallel"),
        name="gqa_attention",
    )(q, k, P, k, P)


def _gmlp_kernel(u_ref, v_ref, g_ref, b_ref, ws_ref, bs_ref, o_ref):
    u = _gelu_tanh(u_ref[...].astype(F32))
    v = _ln_rows(_gelu_tanh(v_ref[...].astype(F32)), g_ref[...], b_ref[...]).astype(BF16)
    for c in range(ROW_BLK // GM_CHUNK):
        rs = slice(c * GM_CHUNK, (c + 1) * GM_CHUNK)
        for g in range(GM_GROUPS):
            cs = slice(g * HEAD_DIM, (g + 1) * HEAD_DIM)
            s = jnp.dot(ws_ref[g], v[rs, cs], preferred_element_type=F32) + bs_ref[g]
            o_ref[rs, cs] = (u[rs, cs] * s).astype(o_ref.dtype)


def chunk_gmlp(P, ln_g, ln_b, w_s, b_s, n_rows):
    bs_rep = jnp.broadcast_to(b_s[:, :, None], (GM_GROUPS, GM_CHUNK, HEAD_DIM))
    BW = BRANCH_WIDTH
    return pl.pallas_call(
        _gmlp_kernel,
        grid=(n_rows // ROW_BLK,),
        in_specs=[pl.BlockSpec((ROW_BLK, BW), lambda i: (i, COL_GM_U)),
                  pl.BlockSpec((ROW_BLK, BW), lambda i: (i, COL_GM_V)),
                  pl.BlockSpec((1, BW), lambda i: (0, 0)),
                  pl.BlockSpec((1, BW), lambda i: (0, 0)),
                  pl.BlockSpec((GM_GROUPS, GM_CHUNK, GM_CHUNK), lambda i: (0, 0, 0)),
                  pl.BlockSpec((GM_GROUPS, GM_CHUNK, HEAD_DIM), lambda i: (0, 0, 0))],
        out_specs=pl.BlockSpec((ROW_BLK, BW), lambda i: (i, 0)),
        out_shape=jax.ShapeDtypeStruct((n_rows, BW), BF16),
        compiler_params=_cparams("parallel"),
    )(P, P, ln_g.reshape(1, BW), ln_b.reshape(1, BW), w_s.astype(BF16), bs_rep)


CONV_HALO = 16
CONV_SUB = 32


def _conv_kernel(a_ref, b_ref, ap_ref, bp_ref, an_ref, bn_ref, w_ref, cb_ref, g_ref, beta_ref, o_ref, gbuf, gsh):
    i = pl.program_id(0)

    def glu(a, b):
        return a.astype(F32) * _sigmoid(b.astype(F32))

    prev_ok = jnp.logical_and(i != 0, i != LAT_ROW_BLKS)
    next_ok = jnp.logical_and(i != LAT_ROW_BLKS - 1, i != N_ROW_BLKS - 1)
    gbuf[0:CONV_HALO, :] = jnp.where(prev_ok, glu(ap_ref[...], bp_ref[...]), 0.0)
    gbuf[CONV_HALO:CONV_HALO + ROW_BLK, :] = glu(a_ref[...], b_ref[...])
    gbuf[CONV_HALO + ROW_BLK:, :] = jnp.where(next_ok, glu(an_ref[...], bn_ref[...]), 0.0)
    first = CONV_HALO - CONV_K // 2
    span = ROW_BLK + 2 * CONV_HALO - 8
    gsh[0, 0:span, :] = gbuf[0:span, :]
    for s in range(1, 8):
        gsh[s, 0:span, :] = gbuf[s:s + span, :]
    for r in range(ROW_BLK // CONV_SUB):
        base = r * CONV_SUB
        acc = jnp.zeros((CONV_SUB, BRANCH_WIDTH), F32)
        for k in range(CONV_K):
            off = first + k
            row0 = base + (off // 8) * 8
            acc = acc + gsh[off % 8, row0:row0 + CONV_SUB, :] * w_ref[k:k + 1, :]
        y = _ln_rows(acc + cb_ref[...], g_ref[...], beta_ref[...])
        o_ref[base:base + CONV_SUB, :] = (y * _sigmoid(y)).astype(o_ref.dtype)


def conformer_conv(P, conv_w, conv_b, ln_g, ln_b, n_rows):
    BW = BRANCH_WIDTH
    hpb = ROW_BLK // CONV_HALO
    last_halo = N_TOK // CONV_HALO - 1

    def prev_map(col):
        return lambda i: (jnp.maximum(i * hpb - 1, 0), col)

    def next_map(col):
        return lambda i: (jnp.minimum((i + 1) * hpb, last_halo), col)

    vec = pl.BlockSpec((1, BW), lambda i: (0, 0))
    return pl.pallas_call(
        _conv_kernel,
        grid=(n_rows // ROW_BLK,),
        in_specs=[pl.BlockSpec((ROW_BLK, BW), lambda i: (i, COL_CV_A)),
                  pl.BlockSpec((ROW_BLK, BW), lambda i: (i, COL_CV_B)),
                  pl.BlockSpec((CONV_HALO, BW), prev_map(COL_CV_A)),
                  pl.BlockSpec((CONV_HALO, BW), prev_map(COL_CV_B)),
                  pl.BlockSpec((CONV_HALO, BW), next_map(COL_CV_A)),
                  pl.BlockSpec((CONV_HALO, BW), next_map(COL_CV_B)),
                  pl.BlockSpec((CONV_K + 1, BW), lambda i: (0, 0)),
                  vec, vec, vec],
        out_specs=pl.BlockSpec((ROW_BLK, BW), lambda i: (i, 0)),
        out_shape=jax.ShapeDtypeStruct((n_rows, BW), BF16),
        scratch_shapes=[pltpu.VMEM((ROW_BLK + 2 * CONV_HALO, BW), F32),
                        pltpu.VMEM((8, ROW_BLK + 2 * CONV_HALO - 8, BW), F32)],
        compiler_params=_cparams("parallel"),
    )(P, P, P, P, P, P, jnp.pad(conv_w, ((0, 1), (0, 0))), conv_b.reshape(1, BW),
      ln_g.reshape(1, BW), ln_b.reshape(1, BW))


MW_TILE = 512
MW_SHIFT = MERGE_START % MW_TILE


def _merge_weights_kernel(a_ref, b_ref, o_ref):
    o_ref[0:MW_TILE - MW_SHIFT, :] = a_ref[MW_SHIFT:, :].astype(o_ref.dtype)
    o_ref[MW_TILE - MW_SHIFT:, :] = b_ref[...].astype(o_ref.dtype)


def merge_weights(w_in_t, layer):
    D = D_MODEL
    a0 = (MERGE_START - MW_SHIFT) // MW_TILE
    b0 = (MERGE_START - MW_SHIFT) // MW_SHIFT
    return pl.pallas_call(
        _merge_weights_kernel,
        grid=(N_BRANCH * D // MW_TILE,),
        in_specs=[pl.BlockSpec((None, MW_TILE, D), lambda j: (layer, a0 + j, 0)),
                  pl.BlockSpec((None, MW_SHIFT, D), lambda j: (layer, b0 + (j + 1) * (MW_TILE // MW_SHIFT), 0))],
        out_specs=pl.BlockSpec((MW_TILE, D), lambda j: (j, 0)),
        out_shape=jax.ShapeDtypeStruct((N_BRANCH * D, D), BF16),
        compiler_params=_cparams("parallel"),
        name="merge_weights",
    )(w_in_t, w_in_t)


def _merge_kernel(h_ref, b0, b1, b2, b3, m0, m1, m2, m3, w0, w1, w2, w3, o_ref):
    h = h_ref[...]
    acc = None
    for br, wm, wb in ((b0, m0, w0), (b1, m1, w1), (b2, m2, w2), (b3, m3, w3)):
        gate = _sigmoid(_dot_nt(h, wm[...]))
        t = gate * jnp.dot(br[...], wb[...], preferred_element_type=F32)
        acc = t if acc is None else acc + t
    o_ref[...] = acc.astype(o_ref.dtype)


def merge_branches(h_all, branches, w_merge, w_branch, n_rows, tm, tn=256):
    D = D_MODEL
    BW = BRANCH_WIDTH
    nj = D // tn
    one = pl.Buffered(1)
    in_specs = [pl.BlockSpec((tm, D), lambda i, j: (i, 0), pipeline_mode=one)]
    in_specs += [pl.BlockSpec((tm, BW), lambda i, j: (i, 0), pipeline_mode=one) for _ in range(N_BRANCH)]
    in_specs += [pl.BlockSpec((tn, D), functools.partial(lambda i, j, b: (b * nj + j, 0), b=b))
                 for b in range(N_BRANCH)]
    in_specs += [pl.BlockSpec((None, BW, tn), functools.partial(lambda i, j, b: (b, 0, j), b=b))
                 for b in range(N_BRANCH)]
    return pl.pallas_call(
        _merge_kernel,
        grid=(n_rows // tm, nj),
        in_specs=in_specs,
        out_specs=pl.BlockSpec((tm, tn), lambda i, j: (i, j)),
        out_shape=jax.ShapeDtypeStruct((n_rows, D), BF16),
        compiler_params=_cparams("parallel", "arbitrary"),
        name="gated_merge",
    )(h_all, *branches, *([w_merge] * N_BRANCH), *([w_branch] * N_BRANCH))


def _out_ln_kernel(y_ref, w_ref, x_ref, g1_ref, lg_ref, lb_ref, sh_ref, sc_ref, x1_ref, hm_ref, *, nj, tm, tn):
    j = pl.program_id(1)
    cols = pl.ds(pl.multiple_of(j * tn, tn), tn)
    row_is_ctx = pl.program_id(0) * tm + lax.broadcasted_iota(jnp.int32, (tm, 1), 0) >= SEQ
    g1 = jnp.where(row_is_ctx, g1_ref[1:2, cols], g1_ref[0:1, cols])
    part = jnp.dot(y_ref[...], w_ref[...], preferred_element_type=F32)
    x1_ref[:, cols] = ALPHA * x_ref[...] + g1 * part

    @pl.when(j == nj - 1)
    def _():
        sub = next(s for s in (64, 48, 32, 16) if tm % s == 0)

        def chunk(r, carry):
            r0 = pl.multiple_of(r * sub, 16)
            rows = pl.ds(r0, sub)
            is_ctx = pl.program_id(0) * tm + r0 + lax.broadcasted_iota(jnp.int32, (sub, 1), 0) >= SEQ
            x1 = _ln_rows(x1_ref[rows, :], lg_ref[...], lb_ref[...])
            x1_ref[rows, :] = x1
            hm_ref[rows, :] = (x1 * (1.0 + _pick(sc_ref, is_ctx)) + _pick(sh_ref, is_ctx)).astype(hm_ref.dtype)
            return carry

        lax.fori_loop(0, tm // sub, chunk, 0)


def out_proj_ln(y, w_out, x_all, mod_l, ln_g, ln_b, n_rows, tm, tn=512):
    D = D_MODEL
    nj = D // tn
    kern = functools.partial(_out_ln_kernel, nj=nj, tm=tm, tn=tn)
    vec = pl.BlockSpec((1, D), lambda i, j: (0, 0))

    def modspec(col):
        return pl.BlockSpec((16, D), lambda i, j: (0, col))

    return pl.pallas_call(
        kern,
        grid=(n_rows // tm, nj),
        in_specs=[pl.BlockSpec((tm, D), lambda i, j: (i, 0)),
                  pl.BlockSpec((D, tn), lambda i, j: (0, j)),
                  pl.BlockSpec((tm, tn), lambda i, j: (i, j)),
                  modspec(2), vec, vec, modspec(3), modspec(4)],
        out_specs=[pl.BlockSpec((tm, D), lambda i, j: (i, 0)),
                   pl.BlockSpec((tm, D), lambda i, j: (i, 0))],
        out_shape=[jax.ShapeDtypeStruct((n_rows, D), F32),
                   jax.ShapeDtypeStruct((n_rows, D), BF16)],
        compiler_params=_cparams("parallel", "arbitrary"),
        name="out_proj_ln1",
    )(y, w_out, x_all, mod_l, ln_g.reshape(1, D), ln_b.reshape(1, D), mod_l, mod_l)


def _ln2_kernel(x_ref, y_ref, g2_ref, lg_ref, lb_ref, sh_ref, sc_ref, x2_ref, *h_ref, blk_ctx_from):
    is_ctx = pl.program_id(0) >= blk_ctx_from
    z = ALPHA * x_ref[...] + _pick(g2_ref, is_ctx) * y_ref[...]
    x2 = _ln_rows(z, lg_ref[...], lb_ref[...])
    x2_ref[...] = x2
    if h_ref:
        h_ref[0][...] = (x2 * (1.0 + _pick(sc_ref, is_ctx)) + _pick(sh_ref, is_ctx)).astype(BF16)


def ffn_residual_ln(x1, y_moe, mod_l, ln_g, ln_b, mod_next, n_rows):
    D = D_MODEL
    want_h = mod_next is not None
    nxt = mod_next if want_h else mod_l
    kern = functools.partial(_ln2_kernel, blk_ctx_from=LAT_ROW_BLKS)
    row = pl.BlockSpec((ROW_BLK, D), lambda i: (i, 0))
    vec = pl.BlockSpec((1, D), lambda i: (0, 0))
    out_specs = [row, row] if want_h else [row]
    out_shape = [jax.ShapeDtypeStruct((n_rows, D), F32)]
    if want_h:
        out_shape.append(jax.ShapeDtypeStruct((n_rows, D), BF16))
    return pl.pallas_call(
        kern,
        grid=(n_rows // ROW_BLK,),
        in_specs=[row, row, pl.BlockSpec((16, D), lambda i: (0, 5)), vec, vec,
                  pl.BlockSpec((16, D), lambda i: (0, 0)), pl.BlockSpec((16, D), lambda i: (0, 1))],
        out_specs=out_specs,
        out_shape=out_shape,
        compiler_params=_cparams("parallel"),
    )(x1, y_moe, mod_l, ln_g.reshape(1, D), ln_b.reshape(1, D), nxt, nxt)


def _tok_tile(n):
    return next(t for t in (1408, 1280, 1024, 768, 512, 384, 256, 128) if n % t == 0)


def _router_kernel(w_ref, h_ref, o_ref):
    z = _dot_nt(w_ref[...], h_ref[...])
    ez = jnp.exp(z - jnp.max(z, axis=0, keepdims=True))
    o_ref[...] = ez / jnp.sum(ez, axis=0, keepdims=True)


def router_affinities(hm, w_router_t, n_rows):
    D = D_MODEL
    tt = _tok_tile(n_rows)
    return pl.pallas_call(
        _router_kernel,
        grid=(n_rows // tt,),
        in_specs=[pl.BlockSpec((N_EXPERTS, D), lambda i: (0, 0)),
                  pl.BlockSpec((tt, D), lambda i: (i, 0))],
        out_specs=pl.BlockSpec((N_EXPERTS, tt), lambda i: (0, i)),
        out_shape=jax.ShapeDtypeStruct((N_EXPERTS, n_rows), F32),
        compiler_params=_cparams("parallel"),
        name="router_affinities",
    )(w_router_t, hm)


def _select_kernel(aff_ref, idx_ref, cum_ref, *, T, cap):
    E = N_EXPERTS
    nc = T // LANES
    bits = pltpu.bitcast(aff_ref[...], jnp.int32)

    def bisect(_, lohi):
        lo, hi = lohi
        mid = lo + ((hi - lo + 1) >> 1)
        cnt = jnp.sum((bits >= mid).astype(F32), axis=1, keepdims=True)
        ok = cnt >= cap
        return jnp.where(ok, mid, lo), jnp.where(ok, hi, mid - 1)

    thr, _ = lax.fori_loop(0, 31, bisect, (jnp.zeros((E, 1), jnp.int32), jnp.full((E, 1), 0x7F800000, jnp.int32)))
    gt = (bits > thr).astype(F32)
    eq = (bits == thr).astype(F32)
    need = cap - jnp.sum(gt, axis=1, keepdims=True)
    row = lax.broadcasted_iota(jnp.int32, (LANES, LANES), 0)
    col = lax.broadcasted_iota(jnp.int32, (LANES, LANES), 1)
    before = (row < col).astype(BF16)
    upto = (row <= col).astype(BF16)
    seen_eq = jnp.zeros((E, 1), F32)
    seen_sel = jnp.zeros((E, 1), F32)
    for c in range(nc):
        sl = slice(c * LANES, (c + 1) * LANES)
        eq_c = eq[:, sl]
        rank_eq = jnp.dot(eq_c.astype(BF16), before, preferred_element_type=F32) + seen_eq
        seen_eq = seen_eq + jnp.sum(eq_c, axis=1, keepdims=True)
        sel_c = jnp.maximum(gt[:, sl], jnp.where(rank_eq < need, eq_c, 0.0))
        cum_ref[:, sl] = jnp.dot(sel_c.astype(BF16), upto, preferred_element_type=F32) + seen_sel
        seen_sel = seen_sel + jnp.sum(sel_c, axis=1, keepdims=True)

    sb = min(cap, 256)
    lane = lax.broadcasted_iota(jnp.int32, (sb, LANES), 1)
    for b in range(cap // sb):
        slot = (b * sb + lax.broadcasted_iota(jnp.int32, (sb, LANES), 0)).astype(F32)
        res = jnp.zeros((sb, LANES), F32)
        for e in range(E):
            def count(c, acc):
                cum = cum_ref[e:e + 1, pl.ds(pl.multiple_of(c * LANES, LANES), LANES)]
                return acc + jnp.where(cum <= slot, 1.0, 0.0)

            acc = lax.fori_loop(0, nc, count, jnp.zeros((sb, LANES), F32))
            res = jnp.where(lane == e, jnp.sum(acc, axis=1, keepdims=True), res)
        idx_ref[b * sb:(b + 1) * sb, :] = res.astype(jnp.int32)


def expert_choice_indices(aff, first_tok, T, cap):
    kern = functools.partial(_select_kernel, T=T, cap=cap)
    out = pl.pallas_call(
        kern,
        grid=(1,),
        in_specs=[pl.BlockSpec((N_EXPERTS, T), lambda i: (0, first_tok // T))],
        out_specs=pl.BlockSpec((cap, LANES), lambda i: (0, 0)),
        out_shape=jax.ShapeDtypeStruct((cap, LANES), jnp.int32),
        scratch_shapes=[pltpu.VMEM((N_EXPERTS, T), F32)],
        compiler_params=_cparams("arbitrary"),
        name="expert_choice_select",
    )(aff)
    return out[:, :N_EXPERTS].T


def _gather_swiglu_kernel(idx_ref, nxt_ref, x_hbm, sh_ref, sc_ref, wr_ref, wg_ref, wu_ref, o_ref, gain_ref,
                          xbuf, xe, sem, *, M, cap_l):
    e = pl.program_id(0)
    j = pl.program_id(1)
    n_e = pl.num_programs(0)

    def row_copy(ref, r):
        return pltpu.make_async_copy(x_hbm.at[pl.ds(ref[0, r], 1), :], xbuf.at[pl.ds(r, 1), :], sem.at[0])

    def each_row(fn):
        def body(r, carry):
            fn(r)
            return carry
        lax.fori_loop(0, M, body, 0, unroll=8)

    @pl.when(j == 0)
    def _():
        @pl.when(e == 0)
        def _():
            each_row(lambda r: row_copy(idx_ref, r).start())

        each_row(lambda r: row_copy(idx_ref, r).wait())
        xe[0:cap_l, :] = (xbuf[0:cap_l, :] * (1.0 + sc_ref[0:1, :]) + sh_ref[0:1, :]).astype(BF16)
        if M > cap_l:
            xe[cap_l:, :] = (xbuf[cap_l:, :] * (1.0 + sc_ref[1:2, :]) + sh_ref[1:2, :]).astype(BF16)

        @pl.when(e + 1 < n_e)
        def _():
            each_row(lambda r: row_copy(nxt_ref, r).start())

        z = jnp.dot(xe[...], wr_ref[...], preferred_element_type=F32)
        lane = lax.broadcasted_iota(jnp.int32, z.shape, 1)
        z = jnp.where(lane < N_EXPERTS, z, -jnp.inf)
        ez = jnp.exp(z - jnp.max(z, axis=1, keepdims=True))
        gain_ref[...] = (jnp.sum(jnp.where(lane == e, ez, 0.0), axis=1, keepdims=True)
                         / jnp.sum(ez, axis=1, keepdims=True))

    x = xe[...]
    g = jnp.dot(x, wg_ref[...].astype(BF16), preferred_element_type=F32)
    u = jnp.dot(x, wu_ref[...].astype(BF16), preferred_element_type=F32)
    o_ref[...] = (g * _sigmoid(g) * u).astype(o_ref.dtype)


def expert_gather_swiglu(idx, x1, mod_l, w_router_pad, w_gate, w_up, layer, cap_l, tn=256):
    E, M = idx.shape
    D = D_MODEL
    N = w_gate.shape[-1]
    idx3 = idx.reshape(E, 1, M)
    wspec = pl.BlockSpec((None, None, D, tn), lambda e, j: (layer, e, 0, j))
    kern = functools.partial(_gather_swiglu_kernel, M=M, cap_l=cap_l)
    return pl.pallas_call(
        kern,
        grid=(E, N // tn),
        in_specs=[pl.BlockSpec((None, 1, M), lambda e, j: (e, 0, 0), memory_space=pltpu.SMEM),
                  pl.BlockSpec((None, 1, M), lambda e, j: (jnp.minimum(e + 1, E - 1), 0, 0), memory_space=pltpu.SMEM),
                  pl.BlockSpec(memory_space=pl.ANY),
                  pl.BlockSpec((16, D), lambda e, j: (0, 3)),
                  pl.BlockSpec((16, D), lambda e, j: (0, 4)),
                  pl.BlockSpec((D, LANES), lambda e, j: (0, 0)),
                  wspec, wspec],
        out_specs=[pl.BlockSpec((None, M, tn), lambda e, j: (e, 0, j)),
                   pl.BlockSpec((None, M, 1), lambda e, j: (e, 0, 0))],
        out_shape=[jax.ShapeDtypeStruct((E, M, N), BF16),
                   jax.ShapeDtypeStruct((E, M, 1), F32)],
        scratch_shapes=[pltpu.VMEM((M, D), F32), pltpu.VMEM((M, D), BF16), pltpu.SemaphoreType.DMA((1,))],
        compiler_params=_cparams("arbitrary", "arbitrary"),
        name="expert_gather_swiglu",
    )(idx3, idx3, x1, mod_l, mod_l, w_router_pad, w_gate, w_up)


def _down_kernel(h_ref, w_ref, g_ref, o_ref):
    y = jnp.dot(h_ref[...], w_ref[...].astype(BF16), preferred_element_type=F32)
    o_ref[...] = y * g_ref[...]


def expert_down(hid, w_down, gains, layer, tn=1024):
    E, M, K = hid.shape
    N = w_down.shape[-1]
    return pl.pallas_call(
        _down_kernel,
        grid=(E, N // tn),
        in_specs=[pl.BlockSpec((None, M, K), lambda e, j: (e, 0, 0)),
                  pl.BlockSpec((None, None, K, tn), lambda e, j: (layer, e, 0, j)),
                  pl.BlockSpec((None, M, 1), lambda e, j: (e, 0, 0))],
        out_specs=pl.BlockSpec((None, M, tn), lambda e, j: (e, 0, j)),
        out_shape=jax.ShapeDtypeStruct((E, M, N), F32),
        compiler_params=_cparams("parallel", "parallel"),
    )(hid, w_down, gains)


def _scatter_kernel(idx_ref, ye_ref, acc_in, acc_ref, buf, gsem, ssem, *, R, n_chunks):
    del acc_in
    c = pl.program_id(1)

    def gather(chunk, slot, r):
        t = idx_ref[0, chunk * R + r]
        return pltpu.make_async_copy(acc_ref.at[pl.ds(t, 1), :], buf.at[slot, pl.ds(r, 1), :], gsem.at[slot])

    def scatter(chunk, slot, r):
        t = idx_ref[0, chunk * R + r]
        return pltpu.make_async_copy(buf.at[slot, pl.ds(r, 1), :], acc_ref.at[pl.ds(t, 1), :], ssem.at[slot])

    def each_row(fn):
        def body(r, carry):
            fn(r)
            return carry
        lax.fori_loop(0, R, body, 0, unroll=8)

    slot = c % 2

    @pl.when(c == 0)
    def _():
        each_row(lambda r: gather(0, 0, r).start())

    each_row(lambda r: gather(c, slot, r).wait())

    @pl.when(c >= 1)
    def _():
        each_row(lambda r: scatter(c - 1, 1 - slot, r).wait())

    @pl.when(c + 1 < n_chunks)
    def _():
        each_row(lambda r: gather(c + 1, 1 - slot, r).start())

    buf[slot] = buf[slot] + ye_ref[...]
    each_row(lambda r: scatter(c, slot, r).start())

    @pl.when(c == n_chunks - 1)
    def _():
        each_row(lambda r: scatter(c, slot, r).wait())


def scatter_add_rows(idx, ye, n_rows):
    E, M, D = ye.shape
    R = next(r for r in (264, 256, 160, 128, 64, 32) if M % r == 0)
    n_chunks = M // R
    kern = functools.partial(_scatter_kernel, R=R, n_chunks=n_chunks)
    return pl.pallas_call(
        kern,
        grid=(E, n_chunks),
        in_specs=[pl.BlockSpec((None, 1, M), lambda e, c: (e, 0, 0), memory_space=pltpu.SMEM),
                  pl.BlockSpec((None, R, D), lambda e, c: (e, c, 0)),
                  pl.BlockSpec(memory_space=pl.ANY)],
        out_specs=pl.BlockSpec(memory_space=pl.ANY),
        out_shape=jax.ShapeDtypeStruct((n_rows, D), F32),
        scratch_shapes=[pltpu.VMEM((2, R, D), F32), pltpu.SemaphoreType.DMA((2,)), pltpu.SemaphoreType.DMA((2,))],
        input_output_aliases={2: 0},
        compiler_params=_cparams("arbitrary", "arbitrary"),
        name="scatter_add_rows",
    )(idx.reshape(E, 1, M), ye, jnp.zeros((n_rows, D), F32))


N_CHUNKS = N_TOK // ML_CHUNK
LAT_CHUNKS = SEQ // ML_CHUNK
CTX_CHUNKS = CTX_LEN // ML_CHUNK


def _mlstm_kernel(q_ref, k_ref, v_ref, og_ref, g_ref, ng_ref, out_ref, acc_ref, c_ref, n_ref, m_ref, *, hb, out_chunks):
    L, d = ML_CHUNK, ML_HEAD_DIM
    row = lax.broadcasted_iota(jnp.int32, (L, L), 0)
    col = lax.broadcasted_iota(jnp.int32, (L, L), 1)
    eye = row == col
    masks = (col <= row, col >= row)
    lasts = (L - 1, 0)
    k_scale = ML_HEAD_DIM ** -0.5

    acc_ref[...] = jnp.zeros(acc_ref.shape, F32)
    c_ref[...] = jnp.zeros(c_ref.shape, F32)
    n_ref[...] = jnp.zeros(n_ref.shape, F32)
    m_ref[...] = jnp.zeros(m_ref.shape, F32)

    def col_of(r):
        return jnp.sum(jnp.where(eye, r, 0.0), axis=1, keepdims=True)

    def step(t, carry):
        chunk = (jnp.where(t < CTX_CHUNKS, t + LAT_CHUNKS, t - CTX_CHUNKS), N_CHUNKS - 1 - t)
        for j in range(hb):
            hs = slice(j * d, (j + 1) * d)
            for dr in range(2):
                si = 2 * j + dr
                off = pl.multiple_of(chunk[dr] * L, L)
                rows = pl.ds(off, L)
                q = q_ref[rows, hs]
                kf = k_ref[rows, hs].astype(F32) * k_scale
                v = v_ref[rows, hs]
                b_row = g_ref[j, 2 * dr:2 * dr + 1, rows]
                a_row = g_ref[j, 2 * dr + 1:2 * dr + 2, rows]
                last = lasts[dr]
                m = m_ref[si][0:1, 0:1]
                n = n_ref[si][0:1, :]
                C = c_ref[si]
                b_col = col_of(b_row)
                a_col = col_of(a_row)
                dlog = jnp.where(masks[dr], b_col + a_row, -jnp.inf)
                inter = b_col + m
                mj = jnp.maximum(inter, jnp.max(dlog, axis=1, keepdims=True))
                w_inter = jnp.exp(inter - mj)
                qk = lax.dot_general(q, kf.astype(BF16), (((1,), (1,)), ((), ())), preferred_element_type=F32)
                s = qk * jnp.exp(dlog - mj)
                num = (w_inter * jnp.dot(q, C.astype(BF16), preferred_element_type=F32)
                       + jnp.dot(s.astype(BF16), v, preferred_element_type=F32))
                qn = jnp.sum(q.astype(F32) * n, axis=1, keepdims=True)
                den = w_inter * qn + jnp.sum(s, axis=1, keepdims=True)
                h = num / jnp.maximum(jnp.abs(den), jnp.exp(-mj))
                acc_ref[rows, hs] += h
                m_new = mj[last:last + 1, :]
                b_end = b_row[:, last:last + 1]
                w_c = jnp.exp(b_end + m - m_new)
                kw = kf * jnp.exp(b_end + a_col - m_new)
                c_ref[si] = w_c * C + lax.dot_general(kw.astype(BF16), v, (((0,), (0,)), ((), ())),
                                                      preferred_element_type=F32)
                n_ref[si] = jnp.broadcast_to(w_c * n + jnp.sum(kw, axis=0, keepdims=True), (8, d))
                m_ref[si] = jnp.broadcast_to(m_new, (8, LANES))
        return carry

    lax.fori_loop(0, N_CHUNKS, step, 0)

    def finish(c, carry):
        rows = pl.ds(pl.multiple_of(c * L, L), L)
        for j in range(hb):
            hs = slice(j * d, (j + 1) * d)
            hh = acc_ref[rows, hs]
            hc = hh - jnp.mean(hh, axis=1, keepdims=True)
            hn = hc * lax.rsqrt(jnp.mean(hc * hc, axis=1, keepdims=True) + EPS) * ng_ref[j]
            out_ref[rows, hs] = (_sigmoid(og_ref[rows, hs].astype(F32)) * hn).astype(out_ref.dtype)
        return carry

    lax.fori_loop(0, out_chunks, finish, 0)


def mlstm(P, G, norm_g, n_rows, hb=2):
    w = hb * ML_HEAD_DIM
    per = BRANCH_WIDTH // w

    def colspec(col):
        return pl.BlockSpec((N_TOK, w), lambda i: (0, col * per + i))

    kern = functools.partial(_mlstm_kernel, hb=hb, out_chunks=n_rows // ML_CHUNK)
    return pl.pallas_call(
        kern,
        grid=(ML_HEADS // hb,),
        in_specs=[colspec(COL_ML_Q), colspec(COL_ML_K), colspec(COL_ML_V),
                  pl.BlockSpec((n_rows, w), lambda i: (0, COL_ML_O * per + i)),
                  pl.BlockSpec((hb, 4, N_TOK), lambda i: (i, 0, 0)),
                  pl.BlockSpec((hb, 1, ML_HEAD_DIM), lambda i: (i, 0, 0))],
        out_specs=pl.BlockSpec((n_rows, w), lambda i: (0, i)),
        out_shape=jax.ShapeDtypeStruct((n_rows, BRANCH_WIDTH), BF16),
        scratch_shapes=[pltpu.VMEM((N_TOK, w), F32),
                        pltpu.VMEM((2 * hb, ML_HEAD_DIM, ML_HEAD_DIM), F32),
                        pltpu.VMEM((2 * hb, 8, ML_HEAD_DIM), F32),
                        pltpu.VMEM((2 * hb, 8, LANES), F32)],
        compiler_params=_cparams("parallel"),
        name="mlstm_scan",
    )(P, P, P, P, G, norm_g.reshape(ML_HEADS, 1, ML_HEAD_DIM))


def _rope_tables():
    rows = SEQ // GRID_W
    row = jnp.repeat(jnp.arange(rows), GRID_W).astype(F32)
    col = (jnp.arange(SEQ) % GRID_W).astype(F32)
    n_freq = HEAD_DIM // 4
    inv = ROPE_THETA ** (-jnp.arange(n_freq, dtype=F32) / n_freq)
    ang_r = row[:, None] * inv[None, :]
    ang_c = col[:, None] * inv[None, :]
    cos = jnp.concatenate([jnp.cos(ang_r), jnp.cos(ang_r), jnp.cos(ang_c), jnp.cos(ang_c)], axis=1)
    sin = jnp.concatenate([-jnp.sin(ang_r), jnp.sin(ang_r), -jnp.sin(ang_c), jnp.sin(ang_c)], axis=1)
    cos = jnp.concatenate([cos, jnp.ones((CTX_LEN, HEAD_DIM), F32)], axis=0)
    sin = jnp.concatenate([sin, jnp.zeros((CTX_LEN, HEAD_DIM), F32)], axis=0)
    return cos, sin


def kernel(x, c, ctx, c_ctx, w_mod, b_mod, w_in, att_q_gain, att_k_gain, gm_ln_g, gm_ln_b,
           gm_w_s, gm_b_s, conv_w, conv_b, conv_ln_g, conv_ln_b, ml_gate_bias, ml_norm_g,
           w_branch, w_out, ln1_g, ln1_b, w_router, w_gate, w_up, w_down, ln2_g, ln2_b):
    cos_t, sin_t = _rope_tables()
    w_in_t = jnp.swapaxes(w_in, 1, 2)
    cc = jnp.zeros((16, D_MODEL), F32).at[0].set(c[0]).at[1].set(c_ctx)
    mods = modulation(cc, w_mod, b_mod)
    x_all = jnp.concatenate([x[0], ctx[0]], axis=0)
    h_all = modulate(x_all, mods[0])
    for l in range(DEPTH):
        last = l == DEPTH - 1
        n_rows = SEQ if last else N_TOK
        tm = (SEQ if last else N_TOK) // 8
        mod_l = mods[l]

        P = branch_projection(h_all, w_in_t, l)
        gbias = jnp.broadcast_to(ml_gate_bias[l].reshape(GATE_COLS, 1), (GATE_COLS, LANES))
        G = mlstm_gate_rows(h_all, w_in_t, l, gbias)
        q, k = qk_prepare(P, cos_t, sin_t, att_q_gain[l], att_k_gain[l])

        br_a = chunk_gmlp(P, gm_ln_g[l], gm_ln_b[l], gm_w_s[l], gm_b_s[l], n_rows)
        br_b = attention(q, k, P, True)
        if not last:
            br_b = jnp.concatenate([br_b, attention(q, k, P, False)], axis=0)
        br_c = conformer_conv(P, conv_w[l], conv_b[l], conv_ln_g[l], conv_ln_b[l], n_rows)
        G_heads = G.reshape(ML_N_GATES, ML_HEADS, N_TOK).transpose(1, 0, 2)
        br_d = mlstm(P, G_heads, ml_norm_g[l], n_rows)

        w_merge = merge_weights(w_in_t, l)
        y = merge_branches(h_all, [br_a, br_b, br_c, br_d], w_merge, w_branch[l].astype(BF16), n_rows, tm)
        x1, hm = out_proj_ln(y, w_out[l].astype(BF16), x_all, mod_l, ln1_g[l], ln1_b[l], n_rows, tm // 2)

        cap_l = CAPACITY_FACTOR * SEQ // N_EXPERTS
        aff = router_affinities(hm, w_router[l].T.astype(BF16), n_rows)
        idx = expert_choice_indices(aff, 0, SEQ, cap_l)
        if not last:
            idx_c = expert_choice_indices(aff, SEQ, CTX_LEN, CAPACITY_FACTOR * CTX_LEN // N_EXPERTS)
            idx = jnp.concatenate([idx, idx_c + SEQ], axis=1)
        w_router_pad = jnp.pad(w_router[l], ((0, 0), (0, LANES - N_EXPERTS))).astype(BF16)
        hid, gains = expert_gather_swiglu(idx, x1, mod_l, w_router_pad, w_gate, w_up, l, cap_l)
        ye = expert_down(hid, w_down, gains, l)
        y_moe = scatter_add_rows(idx, ye, n_rows)
        res = ffn_residual_ln(x1, y_moe, mod_l, ln2_g[l], ln2_b[l], None if last else mods[l + 1], n_rows)
        if not last:
            x_all, h_all = res
        else:
            x_all = res[0]
    return x_all[None]
```
